```python
import math
import jax
import jax.numpy as jnp
from jax import lax
import numpy as np

D_MODEL = 1024
BATCH = 4
SEQ = 4096
DEPTH = 4

HEAD_DIM = 64
RMS_EPS = 1e-6
D_FF = 2816

A_HEADS = 8
IDX_HEADS = 4
IDX_DIM = 64
TOPK_MAX = 256
Q_BLOCK = 128
A_WIDTH = A_HEADS * HEAD_DIM

S5_WIDTH = 512
S5_GROUP = 16
S5_GROUPS = S5_WIDTH // S5_GROUP
S5_STATE = 64
S5_DT_MIN = 1e-3
S5_DT_MAX = 1e-1

DIL_PAIRS = ((128, 1), (512, 4), (2048, 16))
C_HEADS_PER_GROUP = 4
C_HEADS = C_HEADS_PER_GROUP * len(DIL_PAIRS)
C_WIDTH = C_HEADS * HEAD_DIM
C_OUT = C_HEADS_PER_GROUP * HEAD_DIM

N_BRANCHES = 3
IN_SIZES = (A_WIDTH, HEAD_DIM, HEAD_DIM, IDX_HEADS * IDX_DIM, IDX_DIM, IDX_HEADS,
            S5_WIDTH, C_WIDTH, C_WIDTH, C_WIDTH, N_BRANCHES * D_MODEL)
D_IN = sum(IN_SIZES)

kernel_name = "hybrid_dsa_s5_dilated_macaron"


def rmsnorm(x, g):
    xf = x.astype(jnp.float32)
    y = xf * lax.rsqrt(jnp.mean(xf * xf, axis=-1, keepdims=True) + RMS_EPS)
    return (y * g.astype(jnp.float32)).astype(x.dtype)


def swiglu(x, w_gate, w_up, w_down):
    return (jax.nn.silu(x @ w_gate) * (x @ w_up)) @ w_down


def split_columns(proj):
    parts, start = [], 0
    for size in IN_SIZES:
        parts.append(proj[..., start:start + size])
        start += size
    return parts


def dsa_attention(q, k, v, q_idx, k_idx, w_idx):
    bsz, seq = q.shape[0], q.shape[1]
    topk = min(TOPK_MAX, seq // 4)
    n_blk = seq // Q_BLOCK
    key_pos = jnp.arange(seq)
    k_idx32 = k_idx.astype(jnp.float32)

    def one_block(i):
        start = i * Q_BLOCK
        qb = lax.dynamic_slice_in_dim(q, start, Q_BLOCK, axis=1)
        qib = lax.dynamic_slice_in_dim(q_idx, start, Q_BLOCK, axis=1).astype(jnp.float32)
        wb = lax.dynamic_slice_in_dim(w_idx, start, Q_BLOCK, axis=1).astype(jnp.float32)
        q_pos = start + jnp.arange(Q_BLOCK)
        logits = jnp.einsum('bthd,bsd->bths', qib, k_idx32) * (IDX_DIM ** -0.5)
        score = jnp.einsum('bths,bth->bts', jax.nn.relu(logits), wb) * (IDX_HEADS ** -0.5)
        causal = key_pos[None, :] <= q_pos[:, None]
        score = jnp.where(causal[None], score, -jnp.inf)
        _, sel = lax.top_k(score, topk)
        k_sel = jax.vmap(lambda kk, ii: kk[ii])(k, sel)
        v_sel = jax.vmap(lambda vv, ii: vv[ii])(v, sel)
        att = jnp.einsum('bthd,btkd->bthk', qb, k_sel).astype(jnp.float32) * (HEAD_DIM ** -0.5)
        valid = sel <= q_pos[None, :, None]
        att = jnp.where(valid[:, :, None, :], att, -jnp.inf)
        p = jax.nn.softmax(att, axis=-1).astype(v.dtype)
        return jnp.einsum('bthk,btkd->bthd', p, v_sel)

    out = lax.map(one_block, jnp.arange(n_blk))
    return jnp.moveaxis(out, 0, 1).reshape(bsz, seq, -1)


def s5_layer(u, lam_re, lam_im, log_dt, b_re, b_im, c_re, c_im, d_skip):
    bsz, seq, _ = u.shape
    u32 = u.astype(jnp.float32).reshape(bsz, seq, S5_GROUPS, S5_GROUP)
    lam = lax.complex(jnp.minimum(lam_re.astype(jnp.float32), -1e-4), lam_im.astype(jnp.float32))
    dt = jnp.exp(log_dt.astype(jnp.float32))[:, None]
    lam_bar = jnp.exp(lam * dt)
    b = lax.complex(b_re.astype(jnp.float32), b_im.astype(jnp.float32))
    b_bar = ((lam_bar - 1.0) / lam)[:, :, None] * b
    bu = jnp.einsum('gpc,bsgc->bsgp', b_bar, u32.astype(jnp.complex64))
    a = jnp.broadcast_to(lam_bar, bu.shape)

    def combine(e_i, e_j):
        a_i, s_i = e_i
        a_j, s_j = e_j
        return a_j * a_i, a_j * s_i + s_j

    _, states = lax.associative_scan(combine, (a, bu), axis=1)
    c = lax.complex(c_re.astype(jnp.float32), c_im.astype(jnp.float32))
    y = jnp.einsum('gcp,bsgp->bsgc', c, states).real + d_skip.astype(jnp.float32) * u32
    return y.reshape(bsz, seq, S5_WIDTH).astype(u.dtype)


def dilated_group(q, k, v, window, dilation):
    bsz, seq, nh, hd = q.shape
    span = window // dilation
    sub = seq // dilation
    n_blk = -(-sub // span)
    pad = n_blk * span - sub

    def to_blocks(t):
        t = t.reshape(bsz, sub, dilation, nh, hd).transpose(0, 2, 1, 3, 4)
        t = jnp.pad(t, ((0, 0), (0, 0), (0, pad), (0, 0), (0, 0)))
        return t.reshape(bsz, dilation, n_blk, span, nh, hd)

    def with_prev(t):
        prev = jnp.pad(t, ((0, 0), (0, 0), (1, 0), (0, 0), (0, 0), (0, 0)))[:, :, :-1]
        return jnp.concatenate([prev, t], axis=3)

    qb = to_blocks(q)
    kw = with_prev(to_blocks(k))
    vw = with_prev(to_blocks(v))
    s = jnp.einsum('brnqhd,brnkhd->brnhqk', qb, kw).astype(jnp.float32) * (hd ** -0.5)
    qi = jnp.arange(span)[:, None]
    kj = jnp.arange(2 * span)[None, :]
    band = (kj >= qi) & (kj <= qi + span)
    has_prev = (jnp.arange(n_blk)[:, None, None] > 0) | (kj[None] >= span)
    mask = band[None] & has_prev
    s = jnp.where(mask[None, None, :, None], s, -jnp.inf)
    m = jnp.max(s, axis=-1, keepdims=True)
    p = jnp.exp(s - m)
    den = jnp.sum(p, axis=-1)
    o = jnp.einsum('brnhqk,brnkhd->brnqhd', p, vw.astype(jnp.float32))
    o = o / jnp.swapaxes(den, -1, -2)[..., None]
    lse = m[..., 0] + jnp.log(den)
    o = o.reshape(bsz, dilation, n_blk * span, nh, hd)[:, :, :sub]
    o = o.transpose(0, 2, 1, 3, 4).reshape(bsz, seq, nh, hd)
    lse = jnp.swapaxes(lse, -1, -2).reshape(bsz, dilation, n_blk * span, nh)[:, :, :sub]
    lse = lse.transpose(0, 2, 1, 3).reshape(bsz, seq, nh)
    return o, lse


def dilated_mixture(q, k, v):
    bsz, seq = q.shape[0], q.shape[1]
    outs, lses = [], []
    for g, (window, dilation) in enumerate(DIL_PAIRS):
        hs = slice(g * C_HEADS_PER_GROUP, (g + 1) * C_HEADS_PER_GROUP)
        o, l = dilated_group(q[:, :, hs], k[:, :, hs], v[:, :, hs], window, dilation)
        outs.append(o)
        lses.append(l)
    o = jnp.stack(outs, axis=2)
    wts = jax.nn.softmax(jnp.stack(lses, axis=2), axis=2)
    out = jnp.sum(wts[..., None] * o, axis=2)
    return out.reshape(bsz, seq, C_OUT).astype(q.dtype)


def setup_inputs(seed: int = 0) -> dict:
    key = jax.random.key(seed)
    ks = iter(jax.random.split(key, 32))
    f32 = jnp.float32
    L = DEPTH

    def nrm(shape, scale):
        return jax.random.normal(next(ks), shape, f32) * scale

    def gain(shape):
        return 1.0 + 0.02 * jax.random.normal(next(ks), shape, f32)

    n = jnp.arange(S5_STATE, dtype=f32)
    return {
        "x": jax.random.normal(next(ks), (BATCH, SEQ, D_MODEL), f32),
        "ffn1_norm": gain((L, D_MODEL)),
        "ffn1_gate": nrm((L, D_MODEL, D_FF), D_MODEL ** -0.5),
        "ffn1_up": nrm((L, D_MODEL, D_FF), D_MODEL ** -0.5),
        "ffn1_down": nrm((L, D_FF, D_MODEL), D_FF ** -0.5),
        "mix_norm": gain((L, D_MODEL)),
        "w_in": nrm((L, D_MODEL, D_IN), D_MODEL ** -0.5),
        "a_q_norm": gain((L, HEAD_DIM)),
        "a_k_norm": gain((L, HEAD_DIM)),
        "s5_lam_re": -0.5 + nrm((L, S5_GROUPS, S5_STATE), 0.01),
        "s5_lam_im": math.pi * n + nrm((L, S5_GROUPS, S5_STATE), 0.01),
        "s5_log_dt": jax.random.uniform(next(ks), (L, S5_GROUPS), f32,
                                        minval=math.log(S5_DT_MIN), maxval=math.log(S5_DT_MAX)),
        "s5_b_re": nrm((L, S5_GROUPS, S5_STATE, S5_GROUP), (2 * S5_GROUP) ** -0.5),
        "s5_b_im": nrm((L, S5_GROUPS, S5_STATE, S5_GROUP), (2 * S5_GROUP) ** -0.5),
        "s5_c_re": nrm((L, S5_GROUPS, S5_GROUP, S5_STATE), (2 * S5_STATE) ** -0.5),
        "s5_c_im": nrm((L, S5_GROUPS, S5_GROUP, S5_STATE), (2 * S5_STATE) ** -0.5),
        "s5_d": nrm((L, S5_GROUPS, S5_GROUP), 1.0),
        "s5_glu_a": nrm((L, S5_WIDTH, S5_WIDTH), S5_WIDTH ** -0.5),
        "s5_glu_b": nrm((L, S5_WIDTH, S5_WIDTH), S5_WIDTH ** -0.5),
        "c_q_norm": gain((L, HEAD_DIM)),
        "c_k_norm": gain((L, HEAD_DIM)),
        "w_branch_a": nrm((L, A_WIDTH, D_MODEL), A_WIDTH ** -0.5),
        "w_branch_b": nrm((L, S5_WIDTH, D_MODEL), S5_WIDTH ** -0.5),
        "w_branch_c": nrm((L, C_OUT, D_MODEL), C_OUT ** -0.5),
        "w_out": nrm((L, D_MODEL, D_MODEL), (N_BRANCHES * D_MODEL) ** -0.5),
        "ffn2_norm": gain((L, D_MODEL)),
        "ffn2_gate": nrm((L, D_MODEL, D_FF), D_MODEL ** -0.5),
        "ffn2_up": nrm((L, D_MODEL, D_FF), D_MODEL ** -0.5),
        "ffn2_down": nrm((L, D_FF, D_MODEL), D_FF ** -0.5),
    }


def reference(x, ffn1_norm, ffn1_gate, ffn1_up, ffn1_down, mix_norm, w_in, a_q_norm, a_k_norm,
              s5_lam_re, s5_lam_im, s5_log_dt, s5_b_re, s5_b_im, s5_c_re, s5_c_im, s5_d,
              s5_glu_a, s5_glu_b, c_q_norm, c_k_norm, w_branch_a, w_branch_b, w_branch_c, w_out,
              ffn2_norm, ffn2_gate, ffn2_up, ffn2_down):
    bsz, seq, _ = x.shape
    for l in range(DEPTH):
        x = x + 0.5 * swiglu(rmsnorm(x, ffn1_norm[l]), ffn1_gate[l], ffn1_up[l], ffn1_down[l])

        h = rmsnorm(x, mix_norm[l])
        aq, ak, av, iq, ik, iw, su, cq, ck, cv, gates = split_columns(h @ w_in[l])

        aq = rmsnorm(aq.reshape(bsz, seq, A_HEADS, HEAD_DIM), a_q_norm[l])
        ak = rmsnorm(ak, a_k_norm[l])
        y_a = dsa_attention(aq, ak, av, iq.reshape(bsz, seq, IDX_HEADS, IDX_DIM), ik, iw)

        y_b = jax.nn.gelu(s5_layer(su, s5_lam_re[l], s5_lam_im[l], s5_log_dt[l], s5_b_re[l], s5_b_im[l],
                                   s5_c_re[l], s5_c_im[l], s5_d[l]))
        y_b = (y_b @ s5_glu_a[l]) * jax.nn.sigmoid(y_b @ s5_glu_b[l])

        cq = rmsnorm(cq.reshape(bsz, seq, C_HEADS, HEAD_DIM), c_q_norm[l])
        ck = rmsnorm(ck.reshape(bsz, seq, C_HEADS, HEAD_DIM), c_k_norm[l])
        y_c = dilated_mixture(cq, ck, cv.reshape(bsz, seq, C_HEADS, HEAD_DIM))

        g = jax.nn.sigmoid(gates.astype(jnp.float32)).astype(x.dtype)
        g_a = g[..., :D_MODEL]
        g_b = g[..., D_MODEL:2 * D_MODEL]
        g_c = g[..., 2 * D_MODEL:]
        merged = g_a * (y_a @ w_branch_a[l]) + g_b * (y_b @ w_branch_b[l]) + g_c * (y_c @ w_branch_c[l])
        x = x + merged @ w_out[l]

        x = x + 0.5 * swiglu(rmsnorm(x, ffn2_norm[l]), ffn2_gate[l], ffn2_up[l], ffn2_down[l])
    return x
```

```python
import functools
import math

import jax
import jax.numpy as jnp
from jax import lax
from jax.experimental import pallas as pl
from jax.experimental.pallas import tpu as pltpu

f32 = jnp.float32
bf16 = jnp.bfloat16
i32 = jnp.int32

D_MODEL = 1024
D_FF = 2816
HEAD_DIM = 64
RMS_EPS = 1e-6
A_HEADS = 8
A_WIDTH = A_HEADS * HEAD_DIM
IDX_HEADS = 4
IDX_DIM = 64
TOPK_MAX = 256
S5_WIDTH = 512
S5_GROUP = 16
S5_GROUPS = 32
S5_STATE = 64
S5_LANES = S5_GROUPS * S5_STATE
DIL_PAIRS = ((128, 1), (512, 4), (2048, 16))
C_HEADS_PER_GROUP = 4
C_OUT = C_HEADS_PER_GROUP * HEAD_DIM
C_WIDTH = 3 * C_OUT
IN_SIZES = (A_WIDTH, HEAD_DIM, HEAD_DIM, IDX_HEADS * IDX_DIM, IDX_DIM, IDX_HEADS,
            S5_WIDTH, C_WIDTH, C_WIDTH, C_WIDTH, 3 * D_MODEL)

LANES = 128
SUBLANES = 8
VMEM_LIMIT_BYTES = 56 * 1024 * 1024

TOKEN_TILE = 512
FF_CHUNK = 256
Q_TILE = 128
KEY_CHUNK = 512
S5_TIME_TILE = 256
S5_SLAB = 512
S5_SCAN_LANES = 512
DIL_SPAN = 128

NEG_INF = float("-inf")
INT_MIN = -2 ** 31
NEG_INF_KEY = INT_MIN + 0x7FFFFF

_NT = (((1,), (1,)), ((), ()))


def _params(*sem):
    return pltpu.CompilerParams(dimension_semantics=sem, vmem_limit_bytes=VMEM_LIMIT_BYTES)


def _resident(shape):
    nd = len(shape)
    return pl.BlockSpec(shape, lambda *_: (0,) * nd, pipeline_mode=pl.Buffered(1))


def _dot(a, b):
    return jnp.dot(a, b, preferred_element_type=f32)


def _dot_nt(a, b):
    return lax.dot_general(a, b, _NT, preferred_element_type=f32)


def _rmsnorm_rows(x, gain):
    return x * lax.rsqrt(jnp.mean(x * x, axis=-1, keepdims=True) + RMS_EPS) * gain


def _head_rmsnorm(z, gain_tiled, seg_mean):
    sq = z * z
    hi = sq.astype(bf16)
    lo = (sq - hi.astype(f32)).astype(bf16)
    ms = _dot(hi, seg_mean) + _dot(lo, seg_mean)
    return z * lax.rsqrt(ms + RMS_EPS) * gain_tiled


def _ffn_body(x_ref, g_ref, wg_ref, wu_ref, wd_ref, o_ref, acc_ref):
    x = x_ref[...]
    h = _rmsnorm_rows(x, g_ref[...]).astype(bf16)
    for c in range(D_FF // FF_CHUNK):
        sl = slice(c * FF_CHUNK, (c + 1) * FF_CHUNK)
        gate = _dot(h, wg_ref[:, sl])
        up = _dot(h, wu_ref[:, sl])
        act = (gate * jax.nn.sigmoid(gate) * up).astype(bf16)
        contrib = _dot(act, wd_ref[sl, :])
        if c == 0:
            acc_ref[...] = contrib
        else:
            acc_ref[...] += contrib
    o_ref[...] = x + 0.5 * acc_ref[...]


def _ffn(x2, gain, w_gate, w_up, w_down):
    n = x2.shape[0]
    tm = min(TOKEN_TILE, n)
    tile = pl.BlockSpec((tm, D_MODEL), lambda i: (i, 0))
    return pl.pallas_call(
        _ffn_body,
        grid=(n // tm,),
        in_specs=[tile, _resident((1, D_MODEL)), _resident((D_MODEL, D_FF)),
                  _resident((D_MODEL, D_FF)), _resident((D_FF, D_MODEL))],
        out_specs=tile,
        out_shape=jax.ShapeDtypeStruct((n, D_MODEL), f32),
        scratch_shapes=[pltpu.VMEM((tm, D_MODEL), f32)],
        compiler_params=_params("parallel"),
    )(x2, gain, w_gate, w_up, w_down)


_P_AQ, _P_AKK, _P_AVV, _P_IQ, _P_IKK, _P_IW, _P_SU, _P_C = 0, 512, 640, 768, 1024, 1152, 1280, 1792
_P_TOTAL = _P_C + 3 * C_WIDTH


def _in_proj_body(x_ref, g_ref, w_ref, seg_ref, aqg_ref, akg_ref, cqg_ref, ckg_ref,
                  aq_ref, akk_ref, avv_ref, iq_ref, ikk_ref, iw_ref, su_ref, c0_ref, c1_ref, c2_ref):
    h = _rmsnorm_rows(x_ref[...], g_ref[...]).astype(bf16)

    def proj(start, width):
        return _dot(h, w_ref[:, start:start + width])

    seg = seg_ref[...]
    aq = _head_rmsnorm(proj(_P_AQ, A_WIDTH), aqg_ref[...], seg) * (HEAD_DIM ** -0.5)
    aq_ref[...] = aq.astype(bf16)
    akk_ref[...] = _head_rmsnorm(proj(_P_AKK, LANES), akg_ref[...], seg[:LANES, :LANES]).astype(bf16)
    avv_ref[...] = proj(_P_AVV, LANES).astype(bf16)
    iq_ref[...] = proj(_P_IQ, IDX_HEADS * IDX_DIM).astype(bf16)
    ikk_ref[...] = proj(_P_IKK, LANES).astype(bf16)
    iw_ref[...] = proj(_P_IW, LANES)
    su_ref[...] = proj(_P_SU, S5_WIDTH)
    for g, c_ref in enumerate((c0_ref, c1_ref, c2_ref)):
        base = _P_C + g * C_WIDTH
        cq = _head_rmsnorm(proj(base, C_OUT), cqg_ref[...], seg[:C_OUT, :C_OUT])
        ck = _head_rmsnorm(proj(base + C_OUT, C_OUT), ckg_ref[...], seg[:C_OUT, :C_OUT])
        c_ref[:, 0:C_OUT] = cq.astype(bf16)
        c_ref[:, C_OUT:2 * C_OUT] = ck.astype(bf16)
        c_ref[:, 2 * C_OUT:] = proj(base + 2 * C_OUT, C_OUT).astype(bf16)


def _in_proj(x2, gain, w_packed, seg_mean, aq_gain, ak_gain, cq_gain, ck_gain):
    n = x2.shape[0]
    tm = min(TOKEN_TILE, n)
    widths = [(A_WIDTH, bf16), (LANES, bf16), (LANES, bf16), (IDX_HEADS * IDX_DIM, bf16), (LANES, bf16),
              (LANES, f32), (S5_WIDTH, f32), (C_WIDTH, bf16), (C_WIDTH, bf16), (C_WIDTH, bf16)]
    return pl.pallas_call(
        _in_proj_body,
        grid=(n // tm,),
        in_specs=[pl.BlockSpec((tm, D_MODEL), lambda i: (i, 0)), _resident((1, D_MODEL)),
                  _resident((D_MODEL, _P_TOTAL)), _resident((A_WIDTH, A_WIDTH)),
                  _resident((1, A_WIDTH)), _resident((1, LANES)), _resident((1, C_OUT)), _resident((1, C_OUT))],
        out_specs=[pl.BlockSpec((tm, w), lambda i: (i, 0)) for w, _ in widths],
        out_shape=[jax.ShapeDtypeStruct((n, w), dt) for w, dt in widths],
        compiler_params=_params("parallel"),
    )(x2, gain, w_packed, seg_mean, aq_gain, ak_gain, cq_gain, ck_gain)


def _float_order_key(x):
    bits = pltpu.bitcast(x, i32)
    return jnp.where(bits < 0, bits ^ 0x7FFFFFFF, bits)


def _fold_lanes(m):
    acc = m[:, :LANES]
    for t in range(1, KEY_CHUNK // LANES):
        acc = acc + m[:, t * LANES:(t + 1) * LANES]
    return acc


def _dsa_body(topk, aq_ref, iq_ref, iw_ref, kk_ref, vv_ref, ikk_ref, o_ref,
              key_ref, bias_ref, m_ref, l_ref, acc_ref):
    qb = pl.program_id(1)
    n_chunks = lax.shift_right_logical(qb * Q_TILE, int(math.log2(KEY_CHUNK))) + 1
    q_pos = qb * Q_TILE + lax.broadcasted_iota(i32, (Q_TILE, 1), 0)
    lane = lax.broadcasted_iota(i32, (Q_TILE, LANES), 1)
    low_half = lane < HEAD_DIM
    key_iota = lax.broadcasted_iota(i32, (Q_TILE, KEY_CHUNK), 1)

    def causal_mask(c):
        return (c * KEY_CHUNK + key_iota) <= q_pos

    def chunk_rows(c):
        return pl.ds(pl.multiple_of(c * KEY_CHUNK, KEY_CHUNK), KEY_CHUNK)

    iq = iq_ref[0]
    zero_b = jnp.zeros((), bf16)
    iq_heads = []
    for h in range(IDX_HEADS):
        pair = iq[:, (h // 2) * LANES:(h // 2 + 1) * LANES]
        iq_heads.append(jnp.where(low_half if h % 2 == 0 else ~low_half, pair, zero_b))
    w = iw_ref[0] * ((IDX_DIM ** -0.5) * (IDX_HEADS ** -0.5))

    def score_chunk(c, _):
        ik = ikk_ref[0, chunk_rows(c), :]
        score = jnp.zeros((Q_TILE, KEY_CHUNK), f32)
        for h in range(IDX_HEADS):
            score = score + jnp.maximum(_dot_nt(iq_heads[h], ik), 0.0) * w[:, h:h + 1]
        score = jnp.where(score == 0.0, 0.0, score)
        score = jnp.where(causal_mask(c), score, NEG_INF)
        key_ref[c] = _float_order_key(score)
        return 0

    lax.fori_loop(0, n_chunks, score_chunk, 0)

    def count(pred):
        def body(c, acc):
            return acc + _fold_lanes(jnp.where(pred(key_ref[c]), 1.0, 0.0))
        acc = lax.fori_loop(0, n_chunks, body, jnp.zeros((Q_TILE, LANES), f32))
        return jnp.sum(acc, axis=1, keepdims=True)

    def bit_step(i, prefix):
        cand_prefix = prefix | lax.shift_left(jnp.int32(1), 31 - i)
        cand = cand_prefix ^ INT_MIN
        cnt = count(lambda k: k >= cand)
        return jnp.where(cnt >= topk, cand_prefix, prefix)

    prefix = lax.fori_loop(0, 32, bit_step, jnp.zeros((Q_TILE, 1), i32))
    thr = prefix ^ INT_MIN
    n_gt = count(lambda k: k > thr)
    n_ge = count(lambda k: k >= thr)
    need = topk - n_gt
    all_rows = thr <= NEG_INF_KEY
    tie_overflow = jnp.where((n_ge - n_gt > need) & ~all_rows, 1.0, 0.0)
    has_overflow = jnp.max(tie_overflow) > 0.0

    @pl.when(jnp.logical_not(has_overflow))
    def _():
        def body(c, _):
            keep = (key_ref[c] >= thr) & causal_mask(c)
            bias_ref[c] = jnp.where(keep, 0.0, NEG_INF)
            return 0
        lax.fori_loop(0, n_chunks, body, 0)

    @pl.when(has_overflow)
    def _():
        r = lax.broadcasted_iota(i32, (KEY_CHUNK, KEY_CHUNK), 0)
        cidx = lax.broadcasted_iota(i32, (KEY_CHUNK, KEY_CHUNK), 1)
        before = jnp.where(r < cidx, 1.0, 0.0).astype(bf16)

        def body(c, seen):
            k = key_ref[c]
            tie = (k == thr) & causal_mask(c)
            tie_f = jnp.where(tie, 1.0, 0.0)
            rank = seen + _dot(tie_f.astype(bf16), before)
            keep = ((k > thr) | (tie & (rank < need)) | all_rows) & causal_mask(c)
            bias_ref[c] = jnp.where(keep, 0.0, NEG_INF)
            return seen + jnp.sum(tie_f, axis=1, keepdims=True)
        lax.fori_loop(0, n_chunks, body, jnp.zeros((Q_TILE, 1), f32))

    aq = aq_ref[0]
    q_rows = []
    for h in range(A_HEADS):
        pair = aq[:, (h // 2) * LANES:(h // 2 + 1) * LANES]
        q_rows.append(jnp.where(low_half if h % 2 == 0 else ~low_half, pair, zero_b))
    q_all = jnp.concatenate(q_rows, axis=0)
    m_ref[...] = jnp.full(m_ref.shape, NEG_INF, f32)
    l_ref[...] = jnp.zeros(l_ref.shape, f32)
    acc_ref[...] = jnp.zeros(acc_ref.shape, f32)

    def att_chunk(c, _):
        kc = kk_ref[0, chunk_rows(c), :]
        vc = vv_ref[0, chunk_rows(c), :]
        s = _dot_nt(q_all, kc).reshape(A_HEADS, Q_TILE, KEY_CHUNK) + bias_ref[c][None]
        s = s.reshape(A_HEADS * Q_TILE, KEY_CHUNK)
        m_old = m_ref[...]
        m_new = jnp.maximum(m_old, jnp.max(s, axis=1, keepdims=True))
        m_safe = jnp.where(m_new == NEG_INF, 0.0, m_new)
        alpha = jnp.exp(m_old - m_safe)
        p = jnp.exp(s - m_safe)
        l_ref[...] = l_ref[...] * alpha + jnp.sum(p, axis=1, keepdims=True)
        acc_ref[...] = acc_ref[...] * alpha + _dot(p.astype(bf16), vc)
        m_ref[...] = m_new
        return 0

    lax.fori_loop(0, n_chunks, att_chunk, 0)
    out = acc_ref[...] / l_ref[...]
    for j in range(A_HEADS // 2):
        even = out[(2 * j) * Q_TILE:(2 * j + 1) * Q_TILE]
        odd = out[(2 * j + 1) * Q_TILE:(2 * j + 2) * Q_TILE]
        o_ref[0, :, j * LANES:(j + 1) * LANES] = jnp.where(low_half, even, odd).astype(bf16)


def _dsa(aq, iq, iw, akk, avv, ikk):
    bsz, seq, _ = aq.shape
    topk = min(TOPK_MAX, seq // 4)
    n_kc = seq // KEY_CHUNK

    def q_spec(width):
        return pl.BlockSpec((1, Q_TILE, width), lambda b, q: (b, q, 0))

    kv_spec = pl.BlockSpec((1, seq, LANES), lambda b, q: (b, 0, 0))
    return pl.pallas_call(
        functools.partial(_dsa_body, topk),
        grid=(bsz, seq // Q_TILE),
        in_specs=[q_spec(A_WIDTH), q_spec(IDX_HEADS * IDX_DIM), q_spec(LANES), kv_spec, kv_spec, kv_spec],
        out_specs=q_spec(A_WIDTH),
        out_shape=jax.ShapeDtypeStruct((bsz, seq, A_WIDTH), bf16),
        scratch_shapes=[pltpu.VMEM((n_kc, Q_TILE, KEY_CHUNK), i32),
                        pltpu.VMEM((n_kc, Q_TILE, KEY_CHUNK), f32),
                        pltpu.VMEM((A_HEADS * Q_TILE, 1), f32),
                        pltpu.VMEM((A_HEADS * Q_TILE, 1), f32),
                        pltpu.VMEM((A_HEADS * Q_TILE, LANES), f32)],
        compiler_params=_params("parallel", "arbitrary"),
    )(aq, iq, iw, akk, avv, ikk)


def _cmul(ar, ai, br, bi):
    return ar * br - ai * bi, ar * bi + ai * br


def _s5_prepare(lam_re_ref, lam_im_ref, logdt_ref, bre_ref, bim_ref, cre_ref, cim_ref,
                bmat_re, bmat_im, cmat_re, cmat_im, shift_re, shift_im, pow_re, pow_im):
    lr = jnp.minimum(lam_re_ref[...], -1e-4)
    li = lam_im_ref[...]
    dt = jnp.exp(logdt_ref[...])
    mag = jnp.exp(lr * dt)
    l1r, l1i = mag * jnp.cos(li * dt), mag * jnp.sin(li * dt)
    den = lr * lr + li * li
    cr = ((l1r - 1.0) * lr + l1i * li) / den
    ci = (l1i * lr - (l1r - 1.0) * li) / den

    n_slab = S5_LANES // S5_SLAB
    grp_per_slab = S5_SLAB // S5_STATE
    row_g = lax.broadcasted_iota(i32, (LANES, S5_SLAB), 0) // S5_GROUP
    lane_g = lax.broadcasted_iota(i32, (LANES, S5_SLAB), 1) // S5_STATE
    diag_b = row_g == lane_g
    row_g2 = lax.broadcasted_iota(i32, (S5_SLAB, LANES), 0) // S5_STATE
    lane_g2 = lax.broadcasted_iota(i32, (S5_SLAB, LANES), 1) // S5_GROUP
    diag_c = row_g2 == lane_g2
    for j in range(n_slab):
        sl = slice(j * S5_SLAB, (j + 1) * S5_SLAB)
        bbr, bbi = _cmul(cr[:, sl], ci[:, sl], bre_ref[:, sl], bim_ref[:, sl])
        bmat_re[j] = jnp.where(diag_b, bbr, 0.0).astype(bf16)
        bmat_im[j] = jnp.where(diag_b, bbi, 0.0).astype(bf16)
        cmat_re[j] = jnp.where(diag_c, cre_ref[sl, :], 0.0).astype(bf16)
        cmat_im[j] = jnp.where(diag_c, -cim_ref[sl, :], 0.0).astype(bf16)

    l2r, l2i = _cmul(l1r, l1i, l1r, l1i)
    l3r, l3i = _cmul(l2r, l2i, l1r, l1i)
    l4r, l4i = _cmul(l2r, l2i, l2r, l2i)
    l5r, l5i = _cmul(l4r, l4i, l1r, l1i)
    l6r, l6i = _cmul(l4r, l4i, l2r, l2i)
    l7r, l7i = _cmul(l4r, l4i, l3r, l3i)
    l8r, l8i = _cmul(l4r, l4i, l4r, l4i)
    row = lax.broadcasted_iota(i32, (SUBLANES, S5_LANES), 0)
    for idx, (k, pr, pi) in enumerate(((1, l1r, l1i), (2, l2r, l2i), (4, l4r, l4i))):
        shift_re[idx] = jnp.where(row >= k, pr, 0.0)
        shift_im[idx] = jnp.where(row >= k, pi, 0.0)
    pr_acc = jnp.zeros((SUBLANES, S5_LANES), f32)
    pi_acc = jnp.zeros((SUBLANES, S5_LANES), f32)
    powers = ((l1r, l1i), (l2r, l2i), (l3r, l3i), (l4r, l4i), (l5r, l5i), (l6r, l6i), (l7r, l7i), (l8r, l8i))
    for i, (pr, pi) in enumerate(powers):
        pr_acc = jnp.where(row == i, pr, pr_acc)
        pi_acc = jnp.where(row == i, pi, pi_acc)
    pow_re[...] = pr_acc
    pow_im[...] = pi_acc


def _s5_body(u_ref, lam_re_ref, lam_im_ref, logdt_ref, bre_ref, bim_ref, cre_ref, cim_ref, d_ref,
             glu_a_ref, glu_b_ref, o_ref,
             bmat_re, bmat_im, cmat_re, cmat_im, shift_re, shift_im, pow_re, pow_im,
             carry_re, carry_im, st_re, st_im):
    t_idx = pl.program_id(1)

    @pl.when((pl.program_id(0) == 0) & (t_idx == 0))
    def _():
        _s5_prepare(lam_re_ref, lam_im_ref, logdt_ref, bre_ref, bim_ref, cre_ref, cim_ref,
                    bmat_re, bmat_im, cmat_re, cmat_im, shift_re, shift_im, pow_re, pow_im)

    @pl.when(t_idx == 0)
    def _():
        carry_re[...] = jnp.zeros(carry_re.shape, f32)
        carry_im[...] = jnp.zeros(carry_im.shape, f32)

    u = u_ref[0]
    ub = u.astype(bf16)
    n_slab = S5_LANES // S5_SLAB
    for j in range(n_slab):
        uj = ub[:, j * LANES:(j + 1) * LANES]
        st_re[:, j * S5_SLAB:(j + 1) * S5_SLAB] = _dot(uj, bmat_re[j])
        st_im[:, j * S5_SLAB:(j + 1) * S5_SLAB] = _dot(uj, bmat_im[j])

    n_tiles = u.shape[0] // SUBLANES
    for lb in range(S5_LANES // S5_SCAN_LANES):
        ls = slice(lb * S5_SCAN_LANES, (lb + 1) * S5_SCAN_LANES)
        coef = [(shift_re[i, :, ls], shift_im[i, :, ls]) for i in range(3)]
        pwr, pwi = pow_re[:, ls], pow_im[:, ls]

        def tile_step(n, carry, ls=ls, coef=coef, pwr=pwr, pwi=pwi):
            c_re, c_im = carry
            rows = pl.ds(pl.multiple_of(n * SUBLANES, SUBLANES), SUBLANES)
            xr, xi = st_re[rows, ls], st_im[rows, ls]
            for i, k in enumerate((1, 2, 4)):
                sr = pltpu.roll(xr, k, 0)
                si = pltpu.roll(xi, k, 0)
                dr, di = _cmul(coef[i][0], coef[i][1], sr, si)
                xr, xi = xr + dr, xi + di
            dr, di = _cmul(pwr, pwi, c_re, c_im)
            xr, xi = xr + dr, xi + di
            st_re[rows, ls] = xr
            st_im[rows, ls] = xi
            return xr[SUBLANES - 1:SUBLANES, :], xi[SUBLANES - 1:SUBLANES, :]

        c_re, c_im = lax.fori_loop(0, n_tiles, tile_step, (carry_re[:, ls], carry_im[:, ls]))
        carry_re[:, ls] = c_re
        carry_im[:, ls] = c_im

    ys = []
    for j in range(n_slab):
        sl = slice(j * S5_SLAB, (j + 1) * S5_SLAB)
        ys.append(_dot(st_re[:, sl].astype(bf16), cmat_re[j]) + _dot(st_im[:, sl].astype(bf16), cmat_im[j]))
    y = jnp.concatenate(ys, axis=1) + d_ref[...] * u
    g = jax.nn.gelu(y).astype(bf16)
    o_ref[0] = (_dot(g, glu_a_ref[...]) * jax.nn.sigmoid(_dot(g, glu_b_ref[...]))).astype(bf16)


def _s5(su, lam_re, lam_im, logdt, b_re, b_im, c_re, c_im, d_skip, glu_a, glu_b):
    bsz, seq, _ = su.shape
    tt = min(S5_TIME_TILE, seq)
    n_slab = S5_LANES // S5_SLAB
    tile = pl.BlockSpec((1, tt, S5_WIDTH), lambda b, t: (b, t, 0))
    return pl.pallas_call(
        _s5_body,
        grid=(bsz, seq // tt),
        in_specs=[tile, _resident((1, S5_LANES)), _resident((1, S5_LANES)), _resident((1, S5_LANES)),
                  _resident((LANES, S5_LANES)), _resident((LANES, S5_LANES)),
                  _resident((S5_LANES, LANES)), _resident((S5_LANES, LANES)), _resident((1, S5_WIDTH)),
                  _resident((S5_WIDTH, S5_WIDTH)), _resident((S5_WIDTH, S5_WIDTH))],
        out_specs=tile,
        out_shape=jax.ShapeDtypeStruct((bsz, seq, S5_WIDTH), bf16),
        scratch_shapes=[pltpu.VMEM((n_slab, LANES, S5_SLAB), bf16), pltpu.VMEM((n_slab, LANES, S5_SLAB), bf16),
                        pltpu.VMEM((n_slab, S5_SLAB, LANES), bf16), pltpu.VMEM((n_slab, S5_SLAB, LANES), bf16),
                        pltpu.VMEM((3, SUBLANES, S5_LANES), f32), pltpu.VMEM((3, SUBLANES, S5_LANES), f32),
                        pltpu.VMEM((SUBLANES, S5_LANES), f32), pltpu.VMEM((SUBLANES, S5_LANES), f32),
                        pltpu.VMEM((1, S5_LANES), f32), pltpu.VMEM((1, S5_LANES), f32),
                        pltpu.VMEM((tt, S5_LANES), f32), pltpu.VMEM((tt, S5_LANES), f32)],
        compiler_params=_params("arbitrary", "arbitrary"),
    )(su, lam_re, lam_im, logdt, b_re, b_im, c_re, c_im, d_skip, glu_a, glu_b)


def _dilated_body(cur_ref, prev_ref, o_ref, lse_ref):
    n = pl.program_id(2)
    cur = cur_ref[0]
    prev = prev_ref[0]
    q = cur[:, 0:C_OUT]
    k = jnp.concatenate([prev[:, C_OUT:2 * C_OUT], cur[:, C_OUT:2 * C_OUT]], axis=0)
    v = jnp.concatenate([prev[:, 2 * C_OUT:], cur[:, 2 * C_OUT:]], axis=0)
    qi = lax.broadcasted_iota(i32, (DIL_SPAN, 2 * DIL_SPAN), 0)
    kj = lax.broadcasted_iota(i32, (DIL_SPAN, 2 * DIL_SPAN), 1)
    mask = (kj >= qi) & (kj <= qi + DIL_SPAN) & ((kj >= DIL_SPAN) | (n > 0))
    low_half = lax.broadcasted_iota(i32, (DIL_SPAN, LANES), 1) < HEAD_DIM
    zero_b = jnp.zeros((), bf16)
    for j in range(C_OUT // LANES):
        ls = slice(j * LANES, (j + 1) * LANES)
        outs, lses = [], []
        for half in (low_half, ~low_half):
            s = _dot_nt(jnp.where(half, q[:, ls], zero_b), k[:, ls]) * (HEAD_DIM ** -0.5)
            s = jnp.where(mask, s, NEG_INF)
            m = jnp.max(s, axis=1, keepdims=True)
            p = jnp.exp(s - m)
            den = jnp.sum(p, axis=1, keepdims=True)
            outs.append(_dot(p.astype(bf16), v[:, ls]) / den)
            lses.append(jnp.broadcast_to(m + jnp.log(den), (DIL_SPAN, LANES)))
        o_ref[0, :, ls] = jnp.where(low_half, outs[0], outs[1]).astype(bf16)
        lse_ref[0, :, ls] = jnp.where(low_half, lses[0], lses[1])


def _dilated(qkv, dilation):
    bsz, seq, _ = qkv.shape
    sub = seq // dilation
    view = qkv.reshape(bsz, sub, dilation * C_WIDTH)
    cur = pl.BlockSpec((1, DIL_SPAN, C_WIDTH), lambda b, r, n: (b, n, r))
    prev = pl.BlockSpec((1, DIL_SPAN, C_WIDTH), lambda b, r, n: (b, jnp.maximum(n - 1, 0), r))
    out = pl.BlockSpec((1, DIL_SPAN, C_OUT), lambda b, r, n: (b, n, r))
    o, lse = pl.pallas_call(
        _dilated_body,
        grid=(bsz, dilation, sub // DIL_SPAN),
        in_specs=[cur, prev],
        out_specs=[out, out],
        out_shape=[jax.ShapeDtypeStruct((bsz, sub, dilation * C_OUT), bf16),
                   jax.ShapeDtypeStruct((bsz, sub, dilation * C_OUT), f32)],
        compiler_params=_params("parallel", "parallel", "arbitrary"),
    )(view, view)
    return o.reshape(bsz * seq, C_OUT), lse.reshape(bsz * seq, C_OUT)


def _merge_body(x_ref, ya_ref, yb_ref, o0_ref, o1_ref, o2_ref, l0_ref, l1_ref, l2_ref,
                g_ref, wgate_ref, wa_ref, wb_ref, wc_ref, wout_ref, out_ref):
    x = x_ref[...]
    h = _rmsnorm_rows(x, g_ref[...]).astype(bf16)
    l0, l1, l2 = l0_ref[...], l1_ref[...], l2_ref[...]
    mx = jnp.maximum(jnp.maximum(l0, l1), l2)
    e0, e1, e2 = jnp.exp(l0 - mx), jnp.exp(l1 - mx), jnp.exp(l2 - mx)
    yc = (e0 * o0_ref[...].astype(f32) + e1 * o1_ref[...].astype(f32) + e2 * o2_ref[...].astype(f32))
    yc = (yc / (e0 + e1 + e2)).astype(bf16)

    def gate(i):
        return jax.nn.sigmoid(_dot(h, wgate_ref[:, i * D_MODEL:(i + 1) * D_MODEL]))

    merged = gate(0) * _dot(ya_ref[...], wa_ref[...])
    merged = merged + gate(1) * _dot(yb_ref[...], wb_ref[...])
    merged = merged + gate(2) * _dot(yc, wc_ref[...])
    out_ref[...] = x + _dot(merged.astype(bf16), wout_ref[...])


def _merge(x2, ya, yb, os_, lses, gain, w_gates, w_a, w_b, w_c, w_out):
    n = x2.shape[0]
    tm = min(TOKEN_TILE, n)

    def tile(width):
        return pl.BlockSpec((tm, width), lambda i: (i, 0))

    return pl.pallas_call(
        _merge_body,
        grid=(n // tm,),
        in_specs=[tile(D_MODEL), tile(A_WIDTH), tile(S5_WIDTH)] + [tile(C_OUT)] * 6 +
                 [_resident((1, D_MODEL)), _resident((D_MODEL, 3 * D_MODEL)), _resident((A_WIDTH, D_MODEL)),
                  _resident((S5_WIDTH, D_MODEL)), _resident((C_OUT, D_MODEL)), _resident((D_MODEL, D_MODEL))],
        out_specs=tile(D_MODEL),
        out_shape=jax.ShapeDtypeStruct((n, D_MODEL), f32),
        compiler_params=_params("parallel"),
    )(x2, ya, yb, *os_, *lses, gain, w_gates, w_a, w_b, w_c, w_out)


def _pack_w_in(w_in_l):
    offs = [0]
    for s in IN_SIZES:
        offs.append(offs[-1] + s)
    col = lambda i: w_in_l[:, offs[i]:offs[i + 1]]
    aq, ak, av, iq, ik, iw, su, cq, ck, cv, gates = [col(i) for i in range(len(IN_SIZES))]
    iw_pad = jnp.pad(iw, ((0, 0), (0, LANES - IDX_HEADS)))
    parts = [aq, ak, ak, av, av, iq, ik, ik, iw_pad, su]
    for g in range(len(DIL_PAIRS)):
        gs = slice(g * C_OUT, (g + 1) * C_OUT)
        parts += [cq[:, gs], ck[:, gs], cv[:, gs]]
    return jnp.concatenate(parts, axis=1).astype(bf16), gates.astype(bf16)


def _segment_mean_matrix(width):
    seg = jnp.arange(width) // HEAD_DIM
    return jnp.where(seg[:, None] == seg[None, :], 1.0 / HEAD_DIM, 0.0).astype(bf16)


def _tile_row(v, reps):
    return jnp.tile(v.reshape(1, -1), (1, reps))


def kernel(x, ffn1_norm, ffn1_gate, ffn1_up, ffn1_down, mix_norm, w_in, a_q_norm, a_k_norm, s5_lam_re, s5_lam_im, s5_log_dt, s5_b_re, s5_b_im, s5_c_re, s5_c_im, s5_d, s5_glu_a, s5_glu_b, c_q_norm, c_k_norm, w_branch_a, w_branch_b, w_branch_c, w_out, ffn2_norm, ffn2_gate, ffn2_up, ffn2_down):
    bsz, seq, _ = x.shape
    depth = w_in.shape[0]
    n = bsz * seq
    x2 = x.reshape(n, D_MODEL)
    seg_mean = _segment_mean_matrix(A_WIDTH)
    grp_per_slab = S5_SLAB // S5_STATE
    for l in range(depth):
        x2 = _ffn(x2, ffn1_norm[l].reshape(1, -1), ffn1_gate[l].astype(bf16), ffn1_up[l].astype(bf16),
                  ffn1_down[l].astype(bf16))

        w_packed, w_gates = _pack_w_in(w_in[l])
        aq, akk, avv, iq, ikk, iw, su, c0, c1, c2 = _in_proj(
            x2, mix_norm[l].reshape(1, -1), w_packed, seg_mean,
            _tile_row(a_q_norm[l], A_HEADS), _tile_row(a_k_norm[l], 2),
            _tile_row(c_q_norm[l], C_HEADS_PER_GROUP), _tile_row(c_k_norm[l], C_HEADS_PER_GROUP))

        r3 = lambda a: a.reshape(bsz, seq, a.shape[-1])
        ya = _dsa(r3(aq), r3(iq), r3(iw), r3(akk), r3(avv), r3(ikk)).reshape(n, A_WIDTH)

        b_t = lambda b: jnp.tile(jnp.transpose(b, (2, 0, 1)).reshape(S5_GROUP, S5_LANES), (grp_per_slab, 1))
        c_t = lambda c: jnp.tile(jnp.transpose(c, (0, 2, 1)).reshape(S5_LANES, S5_GROUP), (1, grp_per_slab))
        yb = _s5(r3(su), s5_lam_re[l].reshape(1, -1), s5_lam_im[l].reshape(1, -1),
                 jnp.repeat(s5_log_dt[l], S5_STATE).reshape(1, -1),
                 b_t(s5_b_re[l]), b_t(s5_b_im[l]), c_t(s5_c_re[l]), c_t(s5_c_im[l]),
                 s5_d[l].reshape(1, -1), s5_glu_a[l].astype(bf16), s5_glu_b[l].astype(bf16)).reshape(n, S5_WIDTH)

        os_, lses = [], []
        for cg, (_, dilation) in zip((c0, c1, c2), DIL_PAIRS):
            o, lse = _dilated(r3(cg), dilation)
            os_.append(o)
            lses.append(lse)

        x2 = _merge(x2, ya, yb, os_, lses, mix_norm[l].reshape(1, -1), w_gates,
                    w_branch_a[l].astype(bf16), w_branch_b[l].astype(bf16), w_branch_c[l].astype(bf16),
                    w_out[l].astype(bf16))

        x2 = _ffn(x2, ffn2_norm[l].reshape(1, -1), ffn2_gate[l].astype(bf16), ffn2_up[l].astype(bf16),
                  ffn2_down[l].astype(bf16))
    return x2.reshape(bsz, seq, D_MODEL)
```

```python
import functools
import math

import jax
import jax.numpy as jnp
from jax import lax
from jax.experimental import pallas as pl
from jax.experimental.pallas import tpu as pltpu

f32 = jnp.float32
bf16 = jnp.bfloat16
i32 = jnp.int32

D_MODEL = 1024
D_FF = 2816
HEAD_DIM = 64
RMS_EPS = 1e-6
A_HEADS = 8
A_WIDTH = A_HEADS * HEAD_DIM
IDX_HEADS = 4
IDX_DIM = 64
TOPK_MAX = 256
S5_WIDTH = 512
S5_GROUP = 16
S5_GROUPS = 32
S5_STATE = 64
S5_LANES = S5_GROUPS * S5_STATE
DIL_PAIRS = ((128, 1), (512, 4), (2048, 16))
C_HEADS_PER_GROUP = 4
C_OUT = C_HEADS_PER_GROUP * HEAD_DIM
C_WIDTH = 3 * C_OUT
IN_SIZES = (A_WIDTH, HEAD_DIM, HEAD_DIM, IDX_HEADS * IDX_DIM, IDX_DIM, IDX_HEADS,
            S5_WIDTH, C_WIDTH, C_WIDTH, C_WIDTH, 3 * D_MODEL)

LANES = 128
SUBLANES = 8
VMEM_LIMIT_BYTES = 56 * 1024 * 1024

TOKEN_TILE = 512
FF_CHUNK = 256
Q_TILE = 128
KEY_CHUNK = 512
S5_TIME_TILE = 256
S5_SLAB = 512
S5_SCAN_LANES = 512
DIL_SPAN = 128

NEG_INF = float("-inf")
INT_MIN = -2 ** 31
NEG_INF_KEY = INT_MIN + 0x7FFFFF

_NT = (((1,), (1,)), ((), ()))


def _params(*sem):
    return pltpu.CompilerParams(dimension_semantics=sem, vmem_limit_bytes=VMEM_LIMIT_BYTES)


def _resident(shape, layer=None):
    nd = len(shape)
    if layer is None:
        return pl.BlockSpec(shape, lambda *_: (0,) * nd, pipeline_mode=pl.Buffered(1))
    return pl.BlockSpec((None,) + tuple(shape), lambda *_: (layer,) + (0,) * nd, pipeline_mode=pl.Buffered(1))


def _dot(a, b):
    return jnp.dot(a, b, preferred_element_type=f32)


def _dot_nt(a, b):
    return lax.dot_general(a, b, _NT, preferred_element_type=f32)


def _rmsnorm_rows(x, gain):
    return x * lax.rsqrt(jnp.mean(x * x, axis=-1, keepdims=True) + RMS_EPS) * gain


def _head_rmsnorm(z, gain_tiled, seg_mean):
    sq = z * z
    hi = sq.astype(bf16)
    lo = (sq - hi.astype(f32)).astype(bf16)
    ms = _dot(hi, seg_mean) + _dot(lo, seg_mean)
    return z * lax.rsqrt(ms + RMS_EPS) * gain_tiled


def _ffn_body(x_ref, g_ref, wg_ref, wu_ref, wd_ref, o_ref, acc_ref):
    x = x_ref[...]
    h = _rmsnorm_rows(x, g_ref[...]).astype(bf16)
    for c in range(D_FF // FF_CHUNK):
        sl = slice(c * FF_CHUNK, (c + 1) * FF_CHUNK)
        gate = _dot(h, wg_ref[:, sl])
        up = _dot(h, wu_ref[:, sl])
        act = (gate * jax.nn.sigmoid(gate) * up).astype(bf16)
        contrib = _dot(act, wd_ref[sl, :])
        if c == 0:
            acc_ref[...] = contrib
        else:
            acc_ref[...] += contrib
    o_ref[...] = x + 0.5 * acc_ref[...]


def _ffn(x2, layer, gain, w_gate, w_up, w_down):
    n = x2.shape[0]
    tm = min(TOKEN_TILE, n)
    tile = pl.BlockSpec((tm, D_MODEL), lambda i: (i, 0))
    return pl.pallas_call(
        _ffn_body,
        grid=(n // tm,),
        in_specs=[tile, _resident((1, D_MODEL), layer), _resident((D_MODEL, D_FF), layer),
                  _resident((D_MODEL, D_FF), layer), _resident((D_FF, D_MODEL), layer)],
        out_specs=tile,
        out_shape=jax.ShapeDtypeStruct((n, D_MODEL), f32),
        scratch_shapes=[pltpu.VMEM((tm, D_MODEL), f32)],
        compiler_params=_params("parallel"),
        name="ffn",
    )(x2, gain, w_gate, w_up, w_down)


_P_AQ, _P_AKK, _P_AVV, _P_IQ, _P_IKK, _P_IW, _P_SU, _P_C = 0, 512, 640, 768, 1024, 1152, 1280, 1792
_P_TOTAL = _P_C + 3 * C_WIDTH


def _in_proj_body(x_ref, g_ref, w_ref, seg_ref, aqg_ref, akg_ref, cqg_ref, ckg_ref,
                  aq_ref, akk_ref, avv_ref, iq_ref, ikk_ref, iw_ref, su_ref, c0_ref, c1_ref, c2_ref):
    h = _rmsnorm_rows(x_ref[...], g_ref[...]).astype(bf16)

    def proj(start, width):
        return _dot(h, w_ref[:, start:start + width])

    seg = seg_ref[...]
    aq = _head_rmsnorm(proj(_P_AQ, A_WIDTH), aqg_ref[...], seg) * (HEAD_DIM ** -0.5)
    aq_ref[...] = aq.astype(bf16)
    akk_ref[...] = _head_rmsnorm(proj(_P_AKK, LANES), akg_ref[...], seg[:LANES, :LANES]).astype(bf16)
    avv_ref[...] = proj(_P_AVV, LANES).astype(bf16)
    iq_ref[...] = proj(_P_IQ, IDX_HEADS * IDX_DIM).astype(bf16)
    ikk_ref[...] = proj(_P_IKK, LANES).astype(bf16)
    iw_ref[...] = proj(_P_IW, LANES)
    su_ref[...] = proj(_P_SU, S5_WIDTH)
    for g, c_ref in enumerate((c0_ref, c1_ref, c2_ref)):
        base = _P_C + g * C_WIDTH
        cq = _head_rmsnorm(proj(base, C_OUT), cqg_ref[...], seg[:C_OUT, :C_OUT])
        ck = _head_rmsnorm(proj(base + C_OUT, C_OUT), ckg_ref[...], seg[:C_OUT, :C_OUT])
        c_ref[:, 0:C_OUT] = cq.astype(bf16)
        c_ref[:, C_OUT:2 * C_OUT] = ck.astype(bf16)
        c_ref[:, 2 * C_OUT:] = proj(base + 2 * C_OUT, C_OUT).astype(bf16)


def _in_proj(x2, layer, gain, w_packed, seg_mean, aq_gain, ak_gain, cq_gain, ck_gain):
    n = x2.shape[0]
    tm = min(TOKEN_TILE, n)
    widths = [(A_WIDTH, bf16), (LANES, bf16), (LANES, bf16), (IDX_HEADS * IDX_DIM, bf16), (LANES, bf16),
              (LANES, f32), (S5_WIDTH, f32), (C_WIDTH, bf16), (C_WIDTH, bf16), (C_WIDTH, bf16)]
    return pl.pallas_call(
        _in_proj_body,
        grid=(n // tm,),
        in_specs=[pl.BlockSpec((tm, D_MODEL), lambda i: (i, 0)), _resident((1, D_MODEL), layer),
                  _resident((D_MODEL, _P_TOTAL), layer), _resident((A_WIDTH, A_WIDTH)),
                  _resident((1, A_WIDTH), layer), _resident((1, LANES), layer),
                  _resident((1, C_OUT), layer), _resident((1, C_OUT), layer)],
        out_specs=[pl.BlockSpec((tm, w), lambda i: (i, 0)) for w, _ in widths],
        out_shape=[jax.ShapeDtypeStruct((n, w), dt) for w, dt in widths],
        compiler_params=_params("parallel"),
        name="in_proj",
    )(x2, gain, w_packed, seg_mean, aq_gain, ak_gain, cq_gain, ck_gain)


def _float_order_key(x):
    bits = pltpu.bitcast(x, i32)
    return jnp.where(bits < 0, bits ^ 0x7FFFFFFF, bits)


def _fold_lanes(m):
    acc = m[:, :LANES]
    for t in range(1, KEY_CHUNK // LANES):
        acc = acc + m[:, t * LANES:(t + 1) * LANES]
    return acc


def _dsa_body(topk, aq_ref, iq_ref, iw_ref, kk_ref, vv_ref, ikk_ref, o_ref,
              key_ref, bias_ref, s_ref, m_ref, l_ref, acc_ref):
    qb = pl.program_id(1)
    n_chunks = lax.shift_right_logical(qb * Q_TILE, int(math.log2(KEY_CHUNK))) + 1
    q_pos = qb * Q_TILE + lax.broadcasted_iota(i32, (Q_TILE, 1), 0)
    lane = lax.broadcasted_iota(i32, (Q_TILE, LANES), 1)
    low_half = lane < HEAD_DIM
    key_iota = lax.broadcasted_iota(i32, (Q_TILE, KEY_CHUNK), 1)

    def causal_mask(c):
        return (c * KEY_CHUNK + key_iota) <= q_pos

    def chunk_rows(c):
        return pl.ds(pl.multiple_of(c * KEY_CHUNK, KEY_CHUNK), KEY_CHUNK)

    iq = iq_ref[0]
    zero_b = jnp.zeros((), bf16)
    iq_heads = []
    for h in range(IDX_HEADS):
        pair = iq[:, (h // 2) * LANES:(h // 2 + 1) * LANES]
        iq_heads.append(jnp.where(low_half if h % 2 == 0 else ~low_half, pair, zero_b))
    w = iw_ref[0] * ((IDX_DIM ** -0.5) * (IDX_HEADS ** -0.5))

    def score_chunk(c, _):
        ik = ikk_ref[0, chunk_rows(c), :]
        score = jnp.zeros((Q_TILE, KEY_CHUNK), f32)
        for h in range(IDX_HEADS):
            score = score + jnp.maximum(_dot_nt(iq_heads[h], ik), 0.0) * w[:, h:h + 1]
        score = jnp.where(score == 0.0, 0.0, score)
        score = jnp.where(causal_mask(c), score, NEG_INF)
        key_ref[c] = _float_order_key(score)
        return 0

    lax.fori_loop(0, n_chunks, score_chunk, 0)

    def count(pred):
        def body(c, acc):
            return acc + _fold_lanes(jnp.where(pred(key_ref[c]), 1.0, 0.0))
        acc = lax.fori_loop(0, n_chunks, body, jnp.zeros((Q_TILE, LANES), f32))
        return jnp.sum(acc, axis=1, keepdims=True)

    def bit_step(i, prefix):
        cand_prefix = prefix | lax.shift_left(jnp.int32(1), 31 - i)
        cand = cand_prefix ^ INT_MIN
        cnt = count(lambda k: k >= cand)
        return jnp.where(cnt >= topk, cand_prefix, prefix)

    prefix = lax.fori_loop(0, 32, bit_step, jnp.zeros((Q_TILE, 1), i32))
    thr = prefix ^ INT_MIN
    n_gt = count(lambda k: k > thr)
    n_ge = count(lambda k: k >= thr)
    need = topk - n_gt
    all_rows = thr <= NEG_INF_KEY
    tie_overflow = jnp.where((n_ge - n_gt > need) & ~all_rows, 1.0, 0.0)
    has_overflow = jnp.max(tie_overflow) > 0.0

    @pl.when(jnp.logical_not(has_overflow))
    def _():
        def body(c, _):
            keep = (key_ref[c] >= thr) & causal_mask(c)
            bias_ref[c] = jnp.where(keep, 0.0, NEG_INF)
            return 0
        lax.fori_loop(0, n_chunks, body, 0)

    @pl.when(has_overflow)
    def _():
        r = lax.broadcasted_iota(i32, (KEY_CHUNK, KEY_CHUNK), 0)
        cidx = lax.broadcasted_iota(i32, (KEY_CHUNK, KEY_CHUNK), 1)
        before = jnp.where(r < cidx, 1.0, 0.0).astype(bf16)

        def body(c, seen):
            k = key_ref[c]
            tie = (k == thr) & causal_mask(c)
            tie_f = jnp.where(tie, 1.0, 0.0)
            rank = seen + _dot(tie_f.astype(bf16), before)
            keep = ((k > thr) | (tie & (rank < need)) | all_rows) & causal_mask(c)
            bias_ref[c] = jnp.where(keep, 0.0, NEG_INF)
            return seen + jnp.sum(tie_f, axis=1, keepdims=True)
        lax.fori_loop(0, n_chunks, body, jnp.zeros((Q_TILE, 1), f32))

    aq = aq_ref[0]
    q_rows = []
    for h in range(A_HEADS):
        pair = aq[:, (h // 2) * LANES:(h // 2 + 1) * LANES]
        q_rows.append(jnp.where(low_half if h % 2 == 0 else ~low_half, pair, zero_b))
    q_all = jnp.concatenate(q_rows, axis=0)
    n_tiles = KEY_CHUNK // LANES

    m_ref[...] = jnp.full(m_ref.shape, NEG_INF, f32)

    def logits_chunk(c, _):
        s = _dot_nt(q_all, kk_ref[0, chunk_rows(c), :])
        bias = bias_ref[c]
        for h in range(A_HEADS):
            rows = slice(h * Q_TILE, (h + 1) * Q_TILE)
            sh = s[rows] + bias
            s_ref[c, rows, :] = sh
            mx = sh[:, :LANES]
            for t in range(1, n_tiles):
                mx = jnp.maximum(mx, sh[:, t * LANES:(t + 1) * LANES])
            m_ref[rows, :] = jnp.maximum(m_ref[rows, :], mx)
        return 0

    lax.fori_loop(0, n_chunks, logits_chunk, 0)
    m = jnp.max(m_ref[...], axis=1, keepdims=True)
    m_ref[...] = jnp.broadcast_to(m, m_ref.shape)

    l_ref[...] = jnp.zeros(l_ref.shape, f32)
    acc_ref[...] = jnp.zeros(acc_ref.shape, f32)

    def softmax_chunk(c, _):
        m_b = m_ref[...]
        ps = []
        l_add = None
        for t in range(n_tiles):
            p_t = jnp.exp(s_ref[c, :, t * LANES:(t + 1) * LANES] - m_b)
            l_add = p_t if l_add is None else l_add + p_t
            ps.append(p_t.astype(bf16))
        l_ref[...] += l_add
        acc_ref[...] += _dot(jnp.concatenate(ps, axis=1), vv_ref[0, chunk_rows(c), :])
        return 0

    lax.fori_loop(0, n_chunks, softmax_chunk, 0)
    out = acc_ref[...] / jnp.sum(l_ref[...], axis=1, keepdims=True)
    for j in range(A_HEADS // 2):
        even = out[(2 * j) * Q_TILE:(2 * j + 1) * Q_TILE]
        odd = out[(2 * j + 1) * Q_TILE:(2 * j + 2) * Q_TILE]
        o_ref[0, :, j * LANES:(j + 1) * LANES] = jnp.where(low_half, even, odd).astype(bf16)


def _dsa(aq, iq, iw, akk, avv, ikk):
    bsz, seq, _ = aq.shape
    topk = min(TOPK_MAX, seq // 4)
    n_kc = seq // KEY_CHUNK

    def q_spec(width):
        return pl.BlockSpec((1, Q_TILE, width), lambda b, q: (b, q, 0))

    kv_spec = pl.BlockSpec((1, seq, LANES), lambda b, q: (b, 0, 0))
    return pl.pallas_call(
        functools.partial(_dsa_body, topk),
        grid=(bsz, seq // Q_TILE),
        in_specs=[q_spec(A_WIDTH), q_spec(IDX_HEADS * IDX_DIM), q_spec(LANES), kv_spec, kv_spec, kv_spec],
        out_specs=q_spec(A_WIDTH),
        out_shape=jax.ShapeDtypeStruct((bsz, seq, A_WIDTH), bf16),
        scratch_shapes=[pltpu.VMEM((n_kc, Q_TILE, KEY_CHUNK), i32),
                        pltpu.VMEM((n_kc, Q_TILE, KEY_CHUNK), f32),
                        pltpu.VMEM((n_kc, A_HEADS * Q_TILE, KEY_CHUNK), f32),
                        pltpu.VMEM((A_HEADS * Q_TILE, LANES), f32),
                        pltpu.VMEM((A_HEADS * Q_TILE, LANES), f32),
                        pltpu.VMEM((A_HEADS * Q_TILE, LANES), f32)],
        compiler_params=_params("parallel", "arbitrary"),
        name="dsa",
    )(aq, iq, iw, akk, avv, ikk)


def _cmul(ar, ai, br, bi):
    return ar * br - ai * bi, ar * bi + ai * br


def _s5_prepare(lam_re_ref, lam_im_ref, logdt_ref, bre_ref, bim_ref, cre_ref, cim_ref,
                bmat_re, bmat_im, cmat_re, cmat_im, shift_re, shift_im, pow_re, pow_im):
    lr = jnp.minimum(lam_re_ref[...], -1e-4)
    li = lam_im_ref[...]
    dt = jnp.exp(logdt_ref[...])
    mag = jnp.exp(lr * dt)
    l1r, l1i = mag * jnp.cos(li * dt), mag * jnp.sin(li * dt)
    den = lr * lr + li * li
    cr = ((l1r - 1.0) * lr + l1i * li) / den
    ci = (l1i * lr - (l1r - 1.0) * li) / den

    n_slab = S5_LANES // S5_SLAB
    grp_per_slab = S5_SLAB // S5_STATE
    row_g = lax.broadcasted_iota(i32, (LANES, S5_SLAB), 0) // S5_GROUP
    lane_g = lax.broadcasted_iota(i32, (LANES, S5_SLAB), 1) // S5_STATE
    diag_b = row_g == lane_g
    row_g2 = lax.broadcasted_iota(i32, (S5_SLAB, LANES), 0) // S5_STATE
    lane_g2 = lax.broadcasted_iota(i32, (S5_SLAB, LANES), 1) // S5_GROUP
    diag_c = row_g2 == lane_g2
    for j in range(n_slab):
        sl = slice(j * S5_SLAB, (j + 1) * S5_SLAB)
        bbr, bbi = _cmul(cr[:, sl], ci[:, sl], bre_ref[:, sl], bim_ref[:, sl])
        bmat_re[j] = jnp.where(diag_b, bbr, 0.0).astype(bf16)
        bmat_im[j] = jnp.where(diag_b, bbi, 0.0).astype(bf16)
        cmat_re[j] = jnp.where(diag_c, cre_ref[sl, :], 0.0).astype(bf16)
        cmat_im[j] = jnp.where(diag_c, -cim_ref[sl, :], 0.0).astype(bf16)

    l2r, l2i = _cmul(l1r, l1i, l1r, l1i)
    l3r, l3i = _cmul(l2r, l2i, l1r, l1i)
    l4r, l4i = _cmul(l2r, l2i, l2r, l2i)
    l5r, l5i = _cmul(l4r, l4i, l1r, l1i)
    l6r, l6i = _cmul(l4r, l4i, l2r, l2i)
    l7r, l7i = _cmul(l4r, l4i, l3r, l3i)
    l8r, l8i = _cmul(l4r, l4i, l4r, l4i)
    row = lax.broadcasted_iota(i32, (SUBLANES, S5_LANES), 0)
    for idx, (k, pr, pi) in enumerate(((1, l1r, l1i), (2, l2r, l2i), (4, l4r, l4i))):
        shift_re[idx] = jnp.where(row >= k, pr, 0.0)
        shift_im[idx] = jnp.where(row >= k, pi, 0.0)
    pr_acc = jnp.zeros((SUBLANES, S5_LANES), f32)
    pi_acc = jnp.zeros((SUBLANES, S5_LANES), f32)
    powers = ((l1r, l1i), (l2r, l2i), (l3r, l3i), (l4r, l4i), (l5r, l5i), (l6r, l6i), (l7r, l7i), (l8r, l8i))
    for i, (pr, pi) in enumerate(powers):
        pr_acc = jnp.where(row == i, pr, pr_acc)
        pi_acc = jnp.where(row == i, pi, pi_acc)
    pow_re[...] = pr_acc
    pow_im[...] = pi_acc


def _s5_body(u_ref, lam_re_ref, lam_im_ref, logdt_ref, bre_ref, bim_ref, cre_ref, cim_ref, d_ref,
             glu_a_ref, glu_b_ref, o_ref,
             bmat_re, bmat_im, cmat_re, cmat_im, shift_re, shift_im, pow_re, pow_im,
             carry_re, carry_im, st_re, st_im):
    t_idx = pl.program_id(1)

    @pl.when((pl.program_id(0) == 0) & (t_idx == 0))
    def _():
        _s5_prepare(lam_re_ref, lam_im_ref, logdt_ref, bre_ref, bim_ref, cre_ref, cim_ref,
                    bmat_re, bmat_im, cmat_re, cmat_im, shift_re, shift_im, pow_re, pow_im)

    @pl.when(t_idx == 0)
    def _():
        carry_re[...] = jnp.zeros(carry_re.shape, f32)
        carry_im[...] = jnp.zeros(carry_im.shape, f32)

    u = u_ref[0]
    ub = u.astype(bf16)
    n_slab = S5_LANES // S5_SLAB
    for j in range(n_slab):
        uj = ub[:, j * LANES:(j + 1) * LANES]
        st_re[:, j * S5_SLAB:(j + 1) * S5_SLAB] = _dot(uj, bmat_re[j])
        st_im[:, j * S5_SLAB:(j + 1) * S5_SLAB] = _dot(uj, bmat_im[j])

    n_tiles = u.shape[0] // SUBLANES
    for lb in range(S5_LANES // S5_SCAN_LANES):
        ls = slice(lb * S5_SCAN_LANES, (lb + 1) * S5_SCAN_LANES)
        coef = [(shift_re[i, :, ls], shift_im[i, :, ls]) for i in range(3)]
        pwr, pwi = pow_re[:, ls], pow_im[:, ls]

        def tile_step(n, carry, ls=ls, coef=coef, pwr=pwr, pwi=pwi):
            c_re, c_im = carry
            rows = pl.ds(pl.multiple_of(n * SUBLANES, SUBLANES), SUBLANES)
            xr, xi = st_re[rows, ls], st_im[rows, ls]
            for i, k in enumerate((1, 2, 4)):
                sr = pltpu.roll(xr, k, 0)
                si = pltpu.roll(xi, k, 0)
                dr, di = _cmul(coef[i][0], coef[i][1], sr, si)
                xr, xi = xr + dr, xi + di
            dr, di = _cmul(pwr, pwi, c_re, c_im)
            xr, xi = xr + dr, xi + di
            st_re[rows, ls] = xr
            st_im[rows, ls] = xi
            return xr[SUBLANES - 1:SUBLANES, :], xi[SUBLANES - 1:SUBLANES, :]

        c_re, c_im = lax.fori_loop(0, n_tiles, tile_step, (carry_re[:, ls], carry_im[:, ls]))
        carry_re[:, ls] = c_re
        carry_im[:, ls] = c_im

    ys = []
    for j in range(n_slab):
        sl = slice(j * S5_SLAB, (j + 1) * S5_SLAB)
        ys.append(_dot(st_re[:, sl].astype(bf16), cmat_re[j]) + _dot(st_im[:, sl].astype(bf16), cmat_im[j]))
    y = jnp.concatenate(ys, axis=1) + d_ref[...] * u
    g = jax.nn.gelu(y).astype(bf16)
    o_ref[0] = (_dot(g, glu_a_ref[...]) * jax.nn.sigmoid(_dot(g, glu_b_ref[...]))).astype(bf16)


def _s5(su, layer, lam_re, lam_im, logdt, b_re, b_im, c_re, c_im, d_skip, glu_a, glu_b):
    bsz, seq, _ = su.shape
    tt = min(S5_TIME_TILE, seq)
    n_slab = S5_LANES // S5_SLAB
    tile = pl.BlockSpec((1, tt, S5_WIDTH), lambda b, t: (b, t, 0))
    return pl.pallas_call(
        _s5_body,
        grid=(bsz, seq // tt),
        in_specs=[tile, _resident((1, S5_LANES)), _resident((1, S5_LANES)), _resident((1, S5_LANES)),
                  _resident((LANES, S5_LANES)), _resident((LANES, S5_LANES)),
                  _resident((S5_LANES, LANES)), _resident((S5_LANES, LANES)), _resident((1, S5_WIDTH)),
                  _resident((S5_WIDTH, S5_WIDTH), layer), _resident((S5_WIDTH, S5_WIDTH), layer)],
        out_specs=tile,
        out_shape=jax.ShapeDtypeStruct((bsz, seq, S5_WIDTH), bf16),
        scratch_shapes=[pltpu.VMEM((n_slab, LANES, S5_SLAB), bf16), pltpu.VMEM((n_slab, LANES, S5_SLAB), bf16),
                        pltpu.VMEM((n_slab, S5_SLAB, LANES), bf16), pltpu.VMEM((n_slab, S5_SLAB, LANES), bf16),
                        pltpu.VMEM((3, SUBLANES, S5_LANES), f32), pltpu.VMEM((3, SUBLANES, S5_LANES), f32),
                        pltpu.VMEM((SUBLANES, S5_LANES), f32), pltpu.VMEM((SUBLANES, S5_LANES), f32),
                        pltpu.VMEM((1, S5_LANES), f32), pltpu.VMEM((1, S5_LANES), f32),
                        pltpu.VMEM((tt, S5_LANES), f32), pltpu.VMEM((tt, S5_LANES), f32)],
        compiler_params=_params("arbitrary", "arbitrary"),
        name="s5",
    )(su, lam_re, lam_im, logdt, b_re, b_im, c_re, c_im, d_skip, glu_a, glu_b)


def _dilated_body(cur_ref, prev_ref, o_ref, lse_ref):
    n = pl.program_id(2)
    cur = cur_ref[0]
    prev = prev_ref[0]
    q = cur[:, 0:C_OUT]
    k = jnp.concatenate([prev[:, C_OUT:2 * C_OUT], cur[:, C_OUT:2 * C_OUT]], axis=0)
    v = jnp.concatenate([prev[:, 2 * C_OUT:], cur[:, 2 * C_OUT:]], axis=0)
    qi = lax.broadcasted_iota(i32, (DIL_SPAN, 2 * DIL_SPAN), 0)
    kj = lax.broadcasted_iota(i32, (DIL_SPAN, 2 * DIL_SPAN), 1)
    mask = (kj >= qi) & (kj <= qi + DIL_SPAN) & ((kj >= DIL_SPAN) | (n > 0))
    low_half = lax.broadcasted_iota(i32, (DIL_SPAN, LANES), 1) < HEAD_DIM
    zero_b = jnp.zeros((), bf16)
    for j in range(C_OUT // LANES):
        ls = slice(j * LANES, (j + 1) * LANES)
        outs, lses = [], []
        for half in (low_half, ~low_half):
            s = _dot_nt(jnp.where(half, q[:, ls], zero_b), k[:, ls]) * (HEAD_DIM ** -0.5)
            s = jnp.where(mask, s, NEG_INF)
            m = jnp.max(s, axis=1, keepdims=True)
            p = jnp.exp(s - m)
            den = jnp.sum(p, axis=1, keepdims=True)
            outs.append(_dot(p.astype(bf16), v[:, ls]) / den)
            lses.append(jnp.broadcast_to(m + jnp.log(den), (DIL_SPAN, LANES)))
        o_ref[0, :, ls] = jnp.where(low_half, outs[0], outs[1]).astype(bf16)
        lse_ref[0, :, ls] = jnp.where(low_half, lses[0], lses[1])


def _dilated(qkv, dilation):
    bsz, seq, _ = qkv.shape
    sub = seq // dilation
    view = qkv.reshape(bsz, sub, dilation * C_WIDTH)
    cur = pl.BlockSpec((1, DIL_SPAN, C_WIDTH), lambda b, r, n: (b, n, r))
    prev = pl.BlockSpec((1, DIL_SPAN, C_WIDTH), lambda b, r, n: (b, jnp.maximum(n - 1, 0), r))
    out = pl.BlockSpec((1, DIL_SPAN, C_OUT), lambda b, r, n: (b, n, r))
    o, lse = pl.pallas_call(
        _dilated_body,
        grid=(bsz, dilation, sub // DIL_SPAN),
        in_specs=[cur, prev],
        out_specs=[out, out],
        out_shape=[jax.ShapeDtypeStruct((bsz, sub, dilation * C_OUT), bf16),
                   jax.ShapeDtypeStruct((bsz, sub, dilation * C_OUT), f32)],
        compiler_params=_params("parallel", "parallel", "arbitrary"),
        name=f"dilated_{dilation}",
    )(view, view)
    return o.reshape(bsz * seq, C_OUT), lse.reshape(bsz * seq, C_OUT)


def _merge_body(x_ref, ya_ref, yb_ref, o0_ref, o1_ref, o2_ref, l0_ref, l1_ref, l2_ref,
                g_ref, wgate_ref, wa_ref, wb_ref, wc_ref, wout_ref, out_ref):
    x = x_ref[...]
    h = _rmsnorm_rows(x, g_ref[...]).astype(bf16)
    l0, l1, l2 = l0_ref[...], l1_ref[...], l2_ref[...]
    mx = jnp.maximum(jnp.maximum(l0, l1), l2)
    e0, e1, e2 = jnp.exp(l0 - mx), jnp.exp(l1 - mx), jnp.exp(l2 - mx)
    yc = (e0 * o0_ref[...].astype(f32) + e1 * o1_ref[...].astype(f32) + e2 * o2_ref[...].astype(f32))
    yc = (yc / (e0 + e1 + e2)).astype(bf16)

    def gate(i):
        return jax.nn.sigmoid(_dot(h, wgate_ref[:, i * D_MODEL:(i + 1) * D_MODEL]))

    merged = gate(0) * _dot(ya_ref[...], wa_ref[...])
    merged = merged + gate(1) * _dot(yb_ref[...], wb_ref[...])
    merged = merged + gate(2) * _dot(yc, wc_ref[...])
    out_ref[...] = x + _dot(merged.astype(bf16), wout_ref[...])


def _merge(x2, layer, ya, yb, os_, lses, gain, w_gates, w_a, w_b, w_c, w_out):
    n = x2.shape[0]
    tm = min(TOKEN_TILE, n)

    def tile(width):
        return pl.BlockSpec((tm, width), lambda i: (i, 0))

    return pl.pallas_call(
        _merge_body,
        grid=(n // tm,),
        in_specs=[tile(D_MODEL), tile(A_WIDTH), tile(S5_WIDTH)] + [tile(C_OUT)] * 6 +
                 [_resident((1, D_MODEL), layer), _resident((D_MODEL, 3 * D_MODEL), layer),
                  _resident((A_WIDTH, D_MODEL), layer), _resident((S5_WIDTH, D_MODEL), layer),
                  _resident((C_OUT, D_MODEL), layer), _resident((D_MODEL, D_MODEL), layer)],
        out_specs=tile(D_MODEL),
        out_shape=jax.ShapeDtypeStruct((n, D_MODEL), f32),
        compiler_params=_params("parallel"),
        name="merge",
    )(x2, ya, yb, *os_, *lses, gain, w_gates, w_a, w_b, w_c, w_out)


def _pack_w_in(w_in):
    offs = [0]
    for s in IN_SIZES:
        offs.append(offs[-1] + s)
    col = lambda i: w_in[:, :, offs[i]:offs[i + 1]]
    aq, ak, av, iq, ik, iw, su, cq, ck, cv, gates = [col(i) for i in range(len(IN_SIZES))]
    iw_pad = jnp.pad(iw, ((0, 0), (0, 0), (0, LANES - IDX_HEADS)))
    parts = [aq, ak, ak, av, av, iq, ik, ik, iw_pad, su]
    for g in range(len(DIL_PAIRS)):
        gs = slice(g * C_OUT, (g + 1) * C_OUT)
        parts += [cq[:, :, gs], ck[:, :, gs], cv[:, :, gs]]
    return jnp.concatenate(parts, axis=2).astype(bf16), gates.astype(bf16)


def _segment_mean_matrix(width):
    seg = jnp.arange(width) // HEAD_DIM
    return jnp.where(seg[:, None] == seg[None, :], 1.0 / HEAD_DIM, 0.0).astype(bf16)


def _tile_gain(g, reps):
    return jnp.tile(g[:, None, :], (1, 1, reps))


def kernel(x, ffn1_norm, ffn1_gate, ffn1_up, ffn1_down, mix_norm, w_in, a_q_norm, a_k_norm, s5_lam_re, s5_lam_im, s5_log_dt, s5_b_re, s5_b_im, s5_c_re, s5_c_im, s5_d, s5_glu_a, s5_glu_b, c_q_norm, c_k_norm, w_branch_a, w_branch_b, w_branch_c, w_out, ffn2_norm, ffn2_gate, ffn2_up, ffn2_down):
    bsz, seq, _ = x.shape
    depth = w_in.shape[0]
    n = bsz * seq
    x2 = x.reshape(n, D_MODEL)
    seg_mean = _segment_mean_matrix(A_WIDTH)
    grp_per_slab = S5_SLAB // S5_STATE
    cast = lambda w: w.astype(bf16)
    row = lambda g: g[:, None, :]
    ffn1 = (row(ffn1_norm), cast(ffn1_gate), cast(ffn1_up), cast(ffn1_down))
    ffn2 = (row(ffn2_norm), cast(ffn2_gate), cast(ffn2_up), cast(ffn2_down))
    w_packed, w_gates = _pack_w_in(w_in)
    head_gains = (_tile_gain(a_q_norm, A_HEADS), _tile_gain(a_k_norm, 2),
                  _tile_gain(c_q_norm, C_HEADS_PER_GROUP), _tile_gain(c_k_norm, C_HEADS_PER_GROUP))
    glu = (cast(s5_glu_a), cast(s5_glu_b))
    merge_w = (row(mix_norm), w_gates, cast(w_branch_a), cast(w_branch_b), cast(w_branch_c), cast(w_out))
    r3 = lambda a: a.reshape(bsz, seq, a.shape[-1])
    b_t = lambda b: jnp.tile(jnp.transpose(b, (2, 0, 1)).reshape(S5_GROUP, S5_LANES), (grp_per_slab, 1))
    c_t = lambda c: jnp.tile(jnp.transpose(c, (0, 2, 1)).reshape(S5_LANES, S5_GROUP), (1, grp_per_slab))
    for l in range(depth):
        x2 = _ffn(x2, l, *ffn1)
        aq, akk, avv, iq, ikk, iw, su, c0, c1, c2 = _in_proj(x2, l, row(mix_norm), w_packed, seg_mean, *head_gains)
        ya = _dsa(r3(aq), r3(iq), r3(iw), r3(akk), r3(avv), r3(ikk)).reshape(n, A_WIDTH)
        yb = _s5(r3(su), l, s5_lam_re[l].reshape(1, -1), s5_lam_im[l].reshape(1, -1),
                 jnp.repeat(s5_log_dt[l], S5_STATE).reshape(1, -1),
                 b_t(s5_b_re[l]), b_t(s5_b_im[l]), c_t(s5_c_re[l]), c_t(s5_c_im[l]),
                 s5_d[l].reshape(1, -1), *glu).reshape(n, S5_WIDTH)
        os_, lses = [], []
        for cg, (_, dilation) in zip((c0, c1, c2), DIL_PAIRS):
            o, lse = _dilated(r3(cg), dilation)
            os_.append(o)
            lses.append(lse)
        x2 = _merge(x2, l, ya, yb, os_, lses, *merge_w)
        x2 = _ffn(x2, l, *ffn2)
    return x2.reshape(bsz, seq, D_MODEL)
```

```python
import functools
import math

import jax
import jax.numpy as jnp
from jax import lax
from jax.experimental import pallas as pl
from jax.experimental.pallas import tpu as pltpu

f32 = jnp.float32
bf16 = jnp.bfloat16
i32 = jnp.int32

D_MODEL = 1024
D_FF = 2816
HEAD_DIM = 64
RMS_EPS = 1e-6
A_HEADS = 8
A_WIDTH = A_HEADS * HEAD_DIM
IDX_HEADS = 4
IDX_DIM = 64
TOPK_MAX = 256
S5_WIDTH = 512
S5_GROUP = 16
S5_GROUPS = 32
S5_STATE = 64
S5_LANES = S5_GROUPS * S5_STATE
DIL_PAIRS = ((128, 1), (512, 4), (2048, 16))
C_HEADS_PER_GROUP = 4
C_OUT = C_HEADS_PER_GROUP * HEAD_DIM
C_WIDTH = 3 * C_OUT
IN_SIZES = (A_WIDTH, HEAD_DIM, HEAD_DIM, IDX_HEADS * IDX_DIM, IDX_DIM, IDX_HEADS,
            S5_WIDTH, C_WIDTH, C_WIDTH, C_WIDTH, 3 * D_MODEL)

LANES = 128
SUBLANES = 8
VMEM_LIMIT_BYTES = 56 * 1024 * 1024

TOKEN_TILE = 512
FF_CHUNK = 256
Q_TILE = 128
KEY_CHUNK = 512
COUNT_ROWS = 64
S5_TIME_TILE = 256
S5_SLAB = 512
S5_SCAN_LANES = 512
DIL_SPAN = 128
DIL_ROWS = 512

NEG_INF = float("-inf")
INT_MIN = -2 ** 31
NEG_INF_KEY = INT_MIN + 0x7FFFFF

_NT = (((1,), (1,)), ((), ()))


def _params(*sem):
    return pltpu.CompilerParams(dimension_semantics=sem, vmem_limit_bytes=VMEM_LIMIT_BYTES)


def _resident(shape, layer=None):
    nd = len(shape)
    if layer is None:
        return pl.BlockSpec(shape, lambda *_: (0,) * nd, pipeline_mode=pl.Buffered(1))
    return pl.BlockSpec((None,) + tuple(shape), lambda *_: (layer,) + (0,) * nd, pipeline_mode=pl.Buffered(1))


def _dot(a, b):
    return jnp.dot(a, b, preferred_element_type=f32)


def _dot_nt(a, b):
    return lax.dot_general(a, b, _NT, preferred_element_type=f32)


def _rmsnorm_rows(x, gain):
    return x * lax.rsqrt(jnp.mean(x * x, axis=-1, keepdims=True) + RMS_EPS) * gain


def _head_rmsnorm(z, gain_tiled, seg_mean):
    ms = _dot((z * z).astype(bf16), seg_mean)
    return z * lax.rsqrt(ms + RMS_EPS) * gain_tiled


def _ffn_body(x_ref, g_ref, wg_ref, wu_ref, wd_ref, o_ref, acc_ref):
    x = x_ref[...]
    h = _rmsnorm_rows(x, g_ref[...]).astype(bf16)
    for c in range(D_FF // FF_CHUNK):
        sl = slice(c * FF_CHUNK, (c + 1) * FF_CHUNK)
        gate = _dot(h, wg_ref[:, sl])
        up = _dot(h, wu_ref[:, sl])
        act = (gate * jax.nn.sigmoid(gate) * up).astype(bf16)
        contrib = _dot(act, wd_ref[sl, :])
        if c == 0:
            acc_ref[...] = contrib
        else:
            acc_ref[...] += contrib
    o_ref[...] = x + 0.5 * acc_ref[...]


def _ffn(x2, layer, gain, w_gate, w_up, w_down):
    n = x2.shape[0]
    tm = min(TOKEN_TILE, n)
    tile = pl.BlockSpec((tm, D_MODEL), lambda i: (i, 0))
    return pl.pallas_call(
        _ffn_body,
        grid=(n // tm,),
        in_specs=[tile, _resident((1, D_MODEL), layer), _resident((D_MODEL, D_FF), layer),
                  _resident((D_MODEL, D_FF), layer), _resident((D_FF, D_MODEL), layer)],
        out_specs=tile,
        out_shape=jax.ShapeDtypeStruct((n, D_MODEL), f32),
        scratch_shapes=[pltpu.VMEM((tm, D_MODEL), f32)],
        compiler_params=_params("parallel"),
        name="ffn",
    )(x2, gain, w_gate, w_up, w_down)


_P_AQ = 0
_P_AKK = _P_AQ + A_WIDTH
_P_IQ = _P_AKK + LANES
_P_IKK = _P_IQ + IDX_HEADS * IDX_DIM
_P_SU = _P_IKK + LANES
_P_C = _P_SU + S5_WIDTH
_P_TOTAL = _P_C + 3 * C_WIDTH
_T_ROWS = LANES + SUBLANES


def _in_proj_body(x_ref, g_ref, w_ref, wt_ref, seg_ref, aqg_ref, akg_ref, cqg_ref, ckg_ref,
                  aq_ref, akk_ref, vvt_ref, iq_ref, ikk_ref, iwt_ref, su_ref, c0_ref, c1_ref, c2_ref):
    h = _rmsnorm_rows(x_ref[...], g_ref[...]).astype(bf16)

    def proj(start, width):
        return _dot(h, w_ref[:, start:start + width])

    seg = seg_ref[...]
    aq = _head_rmsnorm(proj(_P_AQ, A_WIDTH), aqg_ref[...], seg) * (HEAD_DIM ** -0.5)
    aq_ref[...] = aq.astype(bf16)
    akk_ref[...] = _head_rmsnorm(proj(_P_AKK, LANES), akg_ref[...], seg[:LANES, :LANES]).astype(bf16)
    iq_ref[...] = proj(_P_IQ, IDX_HEADS * IDX_DIM).astype(bf16)
    ikk_ref[...] = proj(_P_IKK, LANES).astype(bf16)
    su_ref[...] = proj(_P_SU, S5_WIDTH)
    vvt_ref[0] = _dot_nt(wt_ref[:LANES, :], h).astype(bf16)
    iwt_ref[0] = _dot_nt(wt_ref[LANES:, :], h)
    for g, c_ref in enumerate((c0_ref, c1_ref, c2_ref)):
        base = _P_C + g * C_WIDTH
        cq = _head_rmsnorm(proj(base, C_OUT), cqg_ref[...], seg[:C_OUT, :C_OUT])
        ck = _head_rmsnorm(proj(base + C_OUT, C_OUT), ckg_ref[...], seg[:C_OUT, :C_OUT])
        c_ref[:, 0:C_OUT] = cq.astype(bf16)
        c_ref[:, C_OUT:2 * C_OUT] = ck.astype(bf16)
        c_ref[:, 2 * C_OUT:] = proj(base + 2 * C_OUT, C_OUT).astype(bf16)


def _in_proj(x2, layer, gain, w_packed, w_t, seg_mean, aq_gain, ak_gain, cq_gain, ck_gain):
    n = x2.shape[0]
    tm = min(TOKEN_TILE, n)
    rows = lambda w: pl.BlockSpec((tm, w), lambda i: (i, 0))
    cols = lambda r: pl.BlockSpec((1, r, tm), lambda i: (i, 0, 0))
    outs = [(rows(A_WIDTH), (n, A_WIDTH), bf16), (rows(LANES), (n, LANES), bf16),
            (cols(LANES), (n // tm, LANES, tm), bf16), (rows(IDX_HEADS * IDX_DIM), (n, IDX_HEADS * IDX_DIM), bf16),
            (rows(LANES), (n, LANES), bf16), (cols(SUBLANES), (n // tm, SUBLANES, tm), f32),
            (rows(S5_WIDTH), (n, S5_WIDTH), f32)] + [(rows(C_WIDTH), (n, C_WIDTH), bf16)] * 3
    return pl.pallas_call(
        _in_proj_body,
        grid=(n // tm,),
        in_specs=[rows(D_MODEL), _resident((1, D_MODEL), layer),
                  _resident((D_MODEL, _P_TOTAL), layer), _resident((_T_ROWS, D_MODEL), layer),
                  _resident((A_WIDTH, A_WIDTH)),
                  _resident((1, A_WIDTH), layer), _resident((1, LANES), layer),
                  _resident((1, C_OUT), layer), _resident((1, C_OUT), layer)],
        out_specs=[o[0] for o in outs],
        out_shape=[jax.ShapeDtypeStruct(o[1], o[2]) for o in outs],
        compiler_params=_params("parallel"),
        name="in_proj",
    )(x2, gain, w_packed, w_t, seg_mean, aq_gain, ak_gain, cq_gain, ck_gain)


def _float_order_key(x):
    bits = pltpu.bitcast(x, i32)
    return jnp.where(bits < 0, bits ^ 0x7FFFFFFF, bits)


def _fold_rows(x, op, keep=SUBLANES):
    return op(x.reshape(x.shape[0] // keep, keep, x.shape[1]), axis=0)


def _dsa_body(topk, aq_ref, iq_ref, iwt_ref, kk_ref, vvt_ref, ikk_ref, rank_ref, o_ref,
              key_ref, bias_ref, s_ref, m_ref, l_ref, acc_ref):
    qb = pl.program_id(1)
    n_chunks = lax.shift_right_logical(qb * Q_TILE, int(math.log2(KEY_CHUNK))) + 1
    q_pos = qb * Q_TILE + lax.broadcasted_iota(i32, (1, Q_TILE), 1)
    key_iota = lax.broadcasted_iota(i32, (KEY_CHUNK, Q_TILE), 0)
    low_half = lax.broadcasted_iota(i32, (Q_TILE, LANES), 1) < HEAD_DIM
    zero_b = jnp.zeros((), bf16)

    def chunk_rows(c):
        return pl.ds(pl.multiple_of(c * KEY_CHUNK, KEY_CHUNK), KEY_CHUNK)

    def stack_heads(x, n_heads):
        blocks = []
        for h in range(n_heads):
            pair = x[:, (h // 2) * LANES:(h // 2 + 1) * LANES]
            blocks.append(jnp.where(low_half if h % 2 == 0 else ~low_half, pair, zero_b))
        return jnp.concatenate(blocks, axis=0)

    iq_all = stack_heads(iq_ref[0], IDX_HEADS)
    w_t = iwt_ref[0] * ((IDX_DIM ** -0.5) * (IDX_HEADS ** -0.5))

    def score_chunk(c, _):
        logits = _dot_nt(ikk_ref[0, chunk_rows(c), :], iq_all)
        score = None
        for h in range(IDX_HEADS):
            term = jnp.maximum(logits[:, h * Q_TILE:(h + 1) * Q_TILE], 0.0) * w_t[h:h + 1, :]
            score = term if score is None else score + term
        score = jnp.where(score == 0.0, 0.0, score)
        score = jnp.where(c * KEY_CHUNK + key_iota <= q_pos, score, NEG_INF)
        key_ref[c] = _float_order_key(score)
        return 0

    lax.fori_loop(0, n_chunks, score_chunk, 0)

    def count(pred):
        def body(c, acc):
            return acc + _fold_rows(jnp.where(pred(key_ref[c]), 1.0, 0.0), jnp.sum, COUNT_ROWS)
        acc = lax.fori_loop(0, n_chunks, body, jnp.zeros((COUNT_ROWS, Q_TILE), f32))
        return jnp.sum(acc, axis=0, keepdims=True)

    def bit_step(i, prefix):
        cand_prefix = prefix | lax.shift_left(jnp.int32(1), 31 - i)
        cand = cand_prefix ^ INT_MIN
        cnt = count(lambda k: k >= cand)
        return jnp.where(cnt >= topk, cand_prefix, prefix)

    prefix = lax.fori_loop(0, 32, bit_step, jnp.zeros((1, Q_TILE), i32))
    thr = prefix ^ INT_MIN
    n_gt = count(lambda k: k > thr)
    need = jnp.where(thr <= NEG_INF_KEY, 0.0, topk - n_gt)
    rank_mat = rank_ref[...]

    def select_chunk(c, seen):
        k = key_ref[c]
        eq = k == thr
        tie = jnp.where(eq, 1.0, 0.0)
        rank = _dot(rank_mat, tie.astype(bf16)) + seen
        order = jnp.where(eq, rank, jnp.where(k > thr, -1.0, 2.0 ** 30))
        bias_ref[c] = jnp.where(order < need, 0.0, NEG_INF)
        return seen + jnp.sum(_fold_rows(tie, jnp.sum, COUNT_ROWS), axis=0, keepdims=True)

    lax.fori_loop(0, n_chunks, select_chunk, jnp.zeros((1, Q_TILE), f32))

    q_all = stack_heads(aq_ref[0], A_HEADS)

    m_ref[...] = jnp.full(m_ref.shape, NEG_INF, f32)

    def logits_chunk(c, _):
        s = _dot_nt(kk_ref[0, chunk_rows(c), :], q_all)
        bias = bias_ref[c]
        for h in range(A_HEADS):
            cols = slice(h * Q_TILE, (h + 1) * Q_TILE)
            sh = s[:, cols] + bias
            s_ref[c, :, cols] = sh
            m_ref[:, cols] = jnp.maximum(m_ref[:, cols], _fold_rows(sh, jnp.max))
        return 0

    lax.fori_loop(0, n_chunks, logits_chunk, 0)
    m = jnp.max(m_ref[...], axis=0, keepdims=True)

    l_ref[...] = jnp.zeros(l_ref.shape, f32)
    acc_ref[...] = jnp.zeros(acc_ref.shape, f32)

    def softmax_chunk(c, _):
        p = jnp.exp(s_ref[c] - m)
        l_ref[...] += _fold_rows(p, jnp.sum)
        acc_ref[...] += _dot(vvt_ref[0, c], p.astype(bf16))
        return 0

    lax.fori_loop(0, n_chunks, softmax_chunk, 0)
    out_t = acc_ref[...] / jnp.sum(l_ref[...], axis=0, keepdims=True)
    top_rows = lax.broadcasted_iota(i32, (LANES, Q_TILE), 0) < HEAD_DIM
    for j in range(A_HEADS // 2):
        even = out_t[:, (2 * j) * Q_TILE:(2 * j + 1) * Q_TILE]
        odd = out_t[:, (2 * j + 1) * Q_TILE:(2 * j + 2) * Q_TILE]
        o_ref[0, :, j * LANES:(j + 1) * LANES] = jnp.where(top_rows, even, odd).T.astype(bf16)


def _tie_rank_matrix():
    i = jnp.arange(KEY_CHUNK)
    return (i[None, :] < i[:, None]).astype(bf16)


def _dsa(aq, iq, iwt, akk, vvt, ikk):
    bsz, seq, _ = aq.shape
    topk = min(TOPK_MAX, seq // 4)
    n_kc = seq // KEY_CHUNK
    q_per_chunk = KEY_CHUNK // Q_TILE

    def q_spec(width):
        return pl.BlockSpec((1, Q_TILE, width), lambda b, q: (b, q, 0))

    kv_spec = pl.BlockSpec((1, seq, LANES), lambda b, q: (b, 0, 0))
    return pl.pallas_call(
        functools.partial(_dsa_body, topk),
        grid=(bsz, seq // Q_TILE),
        in_specs=[q_spec(A_WIDTH), q_spec(IDX_HEADS * IDX_DIM),
                  pl.BlockSpec((1, SUBLANES, Q_TILE), lambda b, q: (b * n_kc + q // q_per_chunk, 0, q % q_per_chunk)),
                  kv_spec, pl.BlockSpec((1, n_kc, LANES, KEY_CHUNK), lambda b, q: (b, 0, 0, 0)), kv_spec,
                  _resident((KEY_CHUNK, KEY_CHUNK))],
        out_specs=q_spec(A_WIDTH),
        out_shape=jax.ShapeDtypeStruct((bsz, seq, A_WIDTH), bf16),
        scratch_shapes=[pltpu.VMEM((n_kc, KEY_CHUNK, Q_TILE), i32),
                        pltpu.VMEM((n_kc, KEY_CHUNK, Q_TILE), f32),
                        pltpu.VMEM((n_kc, KEY_CHUNK, A_HEADS * Q_TILE), f32),
                        pltpu.VMEM((SUBLANES, A_HEADS * Q_TILE), f32),
                        pltpu.VMEM((SUBLANES, A_HEADS * Q_TILE), f32),
                        pltpu.VMEM((LANES, A_HEADS * Q_TILE), f32)],
        compiler_params=_params("parallel", "arbitrary"),
        name="dsa",
    )(aq, iq, iwt, akk, vvt, ikk, _tie_rank_matrix())


def _cmul(ar, ai, br, bi):
    return ar * br - ai * bi, ar * bi + ai * br


def _s5_prepare(lam_re_ref, lam_im_ref, logdt_ref, bre_ref, bim_ref, cre_ref, cim_ref,
                bmat_re, bmat_im, cmat_re, cmat_im, shift_re, shift_im, pow_re, pow_im):
    lr = jnp.minimum(lam_re_ref[...], -1e-4)
    li = lam_im_ref[...]
    dt = jnp.exp(logdt_ref[...])
    mag = jnp.exp(lr * dt)
    l1r, l1i = mag * jnp.cos(li * dt), mag * jnp.sin(li * dt)
    den = lr * lr + li * li
    cr = ((l1r - 1.0) * lr + l1i * li) / den
    ci = (l1i * lr - (l1r - 1.0) * li) / den

    n_slab = S5_LANES // S5_SLAB
    row_g = lax.broadcasted_iota(i32, (LANES, S5_SLAB), 0) // S5_GROUP
    lane_g = lax.broadcasted_iota(i32, (LANES, S5_SLAB), 1) // S5_STATE
    diag_b = row_g == lane_g
    row_g2 = lax.broadcasted_iota(i32, (S5_SLAB, LANES), 0) // S5_STATE
    lane_g2 = lax.broadcasted_iota(i32, (S5_SLAB, LANES), 1) // S5_GROUP
    diag_c = row_g2 == lane_g2
    for j in range(n_slab):
        sl = slice(j * S5_SLAB, (j + 1) * S5_SLAB)
        bbr, bbi = _cmul(cr[:, sl], ci[:, sl], bre_ref[:, sl], bim_ref[:, sl])
        bmat_re[j] = jnp.where(diag_b, bbr, 0.0).astype(bf16)
        bmat_im[j] = jnp.where(diag_b, bbi, 0.0).astype(bf16)
        cmat_re[j] = jnp.where(diag_c, cre_ref[sl, :], 0.0).astype(bf16)
        cmat_im[j] = jnp.where(diag_c, -cim_ref[sl, :], 0.0).astype(bf16)

    l2r, l2i = _cmul(l1r, l1i, l1r, l1i)
    l3r, l3i = _cmul(l2r, l2i, l1r, l1i)
    l4r, l4i = _cmul(l2r, l2i, l2r, l2i)
    l5r, l5i = _cmul(l4r, l4i, l1r, l1i)
    l6r, l6i = _cmul(l4r, l4i, l2r, l2i)
    l7r, l7i = _cmul(l4r, l4i, l3r, l3i)
    l8r, l8i = _cmul(l4r, l4i, l4r, l4i)
    row = lax.broadcasted_iota(i32, (SUBLANES, S5_LANES), 0)
    for idx, (k, pr, pi) in enumerate(((1, l1r, l1i), (2, l2r, l2i), (4, l4r, l4i))):
        shift_re[idx] = jnp.where(row >= k, pr, 0.0)
        shift_im[idx] = jnp.where(row >= k, pi, 0.0)
    pr_acc = jnp.zeros((SUBLANES, S5_LANES), f32)
    pi_acc = jnp.zeros((SUBLANES, S5_LANES), f32)
    powers = ((l1r, l1i), (l2r, l2i), (l3r, l3i), (l4r, l4i), (l5r, l5i), (l6r, l6i), (l7r, l7i), (l8r, l8i))
    for i, (pr, pi) in enumerate(powers):
        pr_acc = jnp.where(row == i, pr, pr_acc)
        pi_acc = jnp.where(row == i, pi, pi_acc)
    pow_re[...] = pr_acc
    pow_im[...] = pi_acc


def _s5_body(u_ref, lam_re_ref, lam_im_ref, logdt_ref, bre_ref, bim_ref, cre_ref, cim_ref, d_ref,
             glu_a_ref, glu_b_ref, o_ref,
             bmat_re, bmat_im, cmat_re, cmat_im, shift_re, shift_im, pow_re, pow_im,
             carry_re, carry_im, st_re, st_im):
    t_idx = pl.program_id(1)

    @pl.when((pl.program_id(0) == 0) & (t_idx == 0))
    def _():
        _s5_prepare(lam_re_ref, lam_im_ref, logdt_ref, bre_ref, bim_ref, cre_ref, cim_ref,
                    bmat_re, bmat_im, cmat_re, cmat_im, shift_re, shift_im, pow_re, pow_im)

    @pl.when(t_idx == 0)
    def _():
        carry_re[...] = jnp.zeros(carry_re.shape, f32)
        carry_im[...] = jnp.zeros(carry_im.shape, f32)

    u = u_ref[0]
    ub = u.astype(bf16)
    n_slab = S5_LANES // S5_SLAB
    for j in range(n_slab):
        uj = ub[:, j * LANES:(j + 1) * LANES]
        st_re[:, j * S5_SLAB:(j + 1) * S5_SLAB] = _dot(uj, bmat_re[j])
        st_im[:, j * S5_SLAB:(j + 1) * S5_SLAB] = _dot(uj, bmat_im[j])

    n_tiles = u.shape[0] // SUBLANES
    for lb in range(S5_LANES // S5_SCAN_LANES):
        ls = slice(lb * S5_SCAN_LANES, (lb + 1) * S5_SCAN_LANES)
        coef = [(shift_re[i, :, ls], shift_im[i, :, ls]) for i in range(3)]
        pwr, pwi = pow_re[:, ls], pow_im[:, ls]

        def tile_step(n, carry, ls=ls, coef=coef, pwr=pwr, pwi=pwi):
            c_re, c_im = carry
            rows = pl.ds(pl.multiple_of(n * SUBLANES, SUBLANES), SUBLANES)
            xr, xi = st_re[rows, ls], st_im[rows, ls]
            for i, k in enumerate((1, 2, 4)):
                sr = pltpu.roll(xr, k, 0)
                si = pltpu.roll(xi, k, 0)
                dr, di = _cmul(coef[i][0], coef[i][1], sr, si)
                xr, xi = xr + dr, xi + di
            dr, di = _cmul(pwr, pwi, c_re, c_im)
            xr, xi = xr + dr, xi + di
            st_re[rows, ls] = xr
            st_im[rows, ls] = xi
            return xr[SUBLANES - 1:SUBLANES, :], xi[SUBLANES - 1:SUBLANES, :]

        c_re, c_im = lax.fori_loop(0, n_tiles, tile_step, (carry_re[:, ls], carry_im[:, ls]))
        carry_re[:, ls] = c_re
        carry_im[:, ls] = c_im

    ys = []
    for j in range(n_slab):
        sl = slice(j * S5_SLAB, (j + 1) * S5_SLAB)
        ys.append(_dot(st_re[:, sl].astype(bf16), cmat_re[j]) + _dot(st_im[:, sl].astype(bf16), cmat_im[j]))
    y = jnp.concatenate(ys, axis=1) + d_ref[...] * u
    g = jax.nn.gelu(y).astype(bf16)
    o_ref[0] = (_dot(g, glu_a_ref[...]) * jax.nn.sigmoid(_dot(g, glu_b_ref[...]))).astype(bf16)


def _s5(su, layer, lam_re, lam_im, logdt, b_re, b_im, c_re, c_im, d_skip, glu_a, glu_b):
    bsz, seq, _ = su.shape
    tt = min(S5_TIME_TILE, seq)
    n_slab = S5_LANES // S5_SLAB
    tile = pl.BlockSpec((1, tt, S5_WIDTH), lambda b, t: (b, t, 0))
    return pl.pallas_call(
        _s5_body,
        grid=(bsz, seq // tt),
        in_specs=[tile, _resident((1, S5_LANES)), _resident((1, S5_LANES)), _resident((1, S5_LANES)),
                  _resident((LANES, S5_LANES)), _resident((LANES, S5_LANES)),
                  _resident((S5_LANES, LANES)), _resident((S5_LANES, LANES)), _resident((1, S5_WIDTH)),
                  _resident((S5_WIDTH, S5_WIDTH), layer), _resident((S5_WIDTH, S5_WIDTH), layer)],
        out_specs=tile,
        out_shape=jax.ShapeDtypeStruct((bsz, seq, S5_WIDTH), bf16),
        scratch_shapes=[pltpu.VMEM((n_slab, LANES, S5_SLAB), bf16), pltpu.VMEM((n_slab, LANES, S5_SLAB), bf16),
                        pltpu.VMEM((n_slab, S5_SLAB, LANES), bf16), pltpu.VMEM((n_slab, S5_SLAB, LANES), bf16),
                        pltpu.VMEM((3, SUBLANES, S5_LANES), f32), pltpu.VMEM((3, SUBLANES, S5_LANES), f32),
                        pltpu.VMEM((SUBLANES, S5_LANES), f32), pltpu.VMEM((SUBLANES, S5_LANES), f32),
                        pltpu.VMEM((1, S5_LANES), f32), pltpu.VMEM((1, S5_LANES), f32),
                        pltpu.VMEM((tt, S5_LANES), f32), pltpu.VMEM((tt, S5_LANES), f32)],
        compiler_params=_params("arbitrary", "arbitrary"),
        name="s5",
    )(su, lam_re, lam_im, logdt, b_re, b_im, c_re, c_im, d_skip, glu_a, glu_b)


def _dilated_body(cur_ref, prev_ref, o_ref, lse_ref):
    n = pl.program_id(2)
    rows = cur_ref.shape[1]
    qi = lax.broadcasted_iota(i32, (DIL_SPAN, 2 * DIL_SPAN), 0)
    kj = lax.broadcasted_iota(i32, (DIL_SPAN, 2 * DIL_SPAN), 1)
    band = (kj >= qi) & (kj <= qi + DIL_SPAN)
    band_first = band & ((kj >= DIL_SPAN) | (n > 0))
    low_half = lax.broadcasted_iota(i32, (DIL_SPAN, LANES), 1) < HEAD_DIM
    zero_b = jnp.zeros((), bf16)
    for blk in range(rows // DIL_SPAN):
        r0 = blk * DIL_SPAN
        q = cur_ref[0, r0:r0 + DIL_SPAN, 0:C_OUT]
        if blk == 0:
            k = jnp.concatenate([prev_ref[0, :, C_OUT:2 * C_OUT], cur_ref[0, 0:DIL_SPAN, C_OUT:2 * C_OUT]], axis=0)
            v = jnp.concatenate([prev_ref[0, :, 2 * C_OUT:], cur_ref[0, 0:DIL_SPAN, 2 * C_OUT:]], axis=0)
        else:
            k = cur_ref[0, r0 - DIL_SPAN:r0 + DIL_SPAN, C_OUT:2 * C_OUT]
            v = cur_ref[0, r0 - DIL_SPAN:r0 + DIL_SPAN, 2 * C_OUT:]
        mask = band_first if blk == 0 else band
        for j in range(C_OUT // LANES):
            ls = slice(j * LANES, (j + 1) * LANES)
            outs, lses = [], []
            for half in (low_half, ~low_half):
                s = _dot_nt(jnp.where(half, q[:, ls], zero_b), k[:, ls]) * (HEAD_DIM ** -0.5)
                s = jnp.where(mask, s, NEG_INF)
                m = jnp.max(s, axis=1, keepdims=True)
                p = jnp.exp(s - m)
                den = jnp.sum(p, axis=1, keepdims=True)
                outs.append(_dot(p.astype(bf16), v[:, ls]) / den)
                lses.append(jnp.broadcast_to(m + jnp.log(den), (DIL_SPAN, LANES)))
            o_ref[0, r0:r0 + DIL_SPAN, ls] = jnp.where(low_half, outs[0], outs[1]).astype(bf16)
            lse_ref[0, r0:r0 + DIL_SPAN, ls] = jnp.where(low_half, lses[0], lses[1])


def _dilated(qkv, dilation):
    bsz, seq, _ = qkv.shape
    sub = seq // dilation
    rows = min(DIL_ROWS, sub)
    blocks_per_step = rows // DIL_SPAN
    view = qkv.reshape(bsz, sub, dilation * C_WIDTH)
    cur = pl.BlockSpec((1, rows, C_WIDTH), lambda b, r, n: (b, n, r))
    prev = pl.BlockSpec((1, DIL_SPAN, C_WIDTH), lambda b, r, n: (b, jnp.maximum(n * blocks_per_step - 1, 0), r))
    out = pl.BlockSpec((1, rows, C_OUT), lambda b, r, n: (b, n, r))
    o, lse = pl.pallas_call(
        _dilated_body,
        grid=(bsz, dilation, sub // rows),
        in_specs=[cur, prev],
        out_specs=[out, out],
        out_shape=[jax.ShapeDtypeStruct((bsz, sub, dilation * C_OUT), bf16),
                   jax.ShapeDtypeStruct((bsz, sub, dilation * C_OUT), f32)],
        compiler_params=_params("parallel", "parallel", "arbitrary"),
        name=f"dilated_{dilation}",
    )(view, view)
    return o.reshape(bsz * seq, C_OUT), lse.reshape(bsz * seq, C_OUT)


def _merge_body(x_ref, ya_ref, yb_ref, o0_ref, o1_ref, o2_ref, l0_ref, l1_ref, l2_ref,
                g_ref, wgate_ref, wa_ref, wb_ref, wc_ref, wout_ref, out_ref):
    x = x_ref[...]
    h = _rmsnorm_rows(x, g_ref[...]).astype(bf16)
    l0, l1, l2 = l0_ref[...], l1_ref[...], l2_ref[...]
    mx = jnp.maximum(jnp.maximum(l0, l1), l2)
    e0, e1, e2 = jnp.exp(l0 - mx), jnp.exp(l1 - mx), jnp.exp(l2 - mx)
    yc = (e0 * o0_ref[...].astype(f32) + e1 * o1_ref[...].astype(f32) + e2 * o2_ref[...].astype(f32))
    yc = (yc / (e0 + e1 + e2)).astype(bf16)

    def gate(i):
        return jax.nn.sigmoid(_dot(h, wgate_ref[:, i * D_MODEL:(i + 1) * D_MODEL]))

    merged = gate(0) * _dot(ya_ref[...], wa_ref[...])
    merged = merged + gate(1) * _dot(yb_ref[...], wb_ref[...])
    merged = merged + gate(2) * _dot(yc, wc_ref[...])
    out_ref[...] = x + _dot(merged.astype(bf16), wout_ref[...])


def _merge(x2, layer, ya, yb, os_, lses, gain, w_gates, w_a, w_b, w_c, w_out):
    n = x2.shape[0]
    tm = min(TOKEN_TILE, n)

    def tile(width):
        return pl.BlockSpec((tm, width), lambda i: (i, 0))

    return pl.pallas_call(
        _merge_body,
        grid=(n // tm,),
        in_specs=[tile(D_MODEL), tile(A_WIDTH), tile(S5_WIDTH)] + [tile(C_OUT)] * 6 +
                 [_resident((1, D_MODEL), layer), _resident((D_MODEL, 3 * D_MODEL), layer),
                  _resident((A_WIDTH, D_MODEL), layer), _resident((S5_WIDTH, D_MODEL), layer),
                  _resident((C_OUT, D_MODEL), layer), _resident((D_MODEL, D_MODEL), layer)],
        out_specs=tile(D_MODEL),
        out_shape=jax.ShapeDtypeStruct((n, D_MODEL), f32),
        compiler_params=_params("parallel"),
        name="merge",
    )(x2, ya, yb, *os_, *lses, gain, w_gates, w_a, w_b, w_c, w_out)


def _pack_w_in(w_in):
    offs = [0]
    for s in IN_SIZES:
        offs.append(offs[-1] + s)
    col = lambda i: w_in[:, :, offs[i]:offs[i + 1]]
    aq, ak, av, iq, ik, iw, su, cq, ck, cv, gates = [col(i) for i in range(len(IN_SIZES))]
    parts = [aq, ak, ak, iq, ik, ik, su]
    for g in range(len(DIL_PAIRS)):
        gs = slice(g * C_OUT, (g + 1) * C_OUT)
        parts += [cq[:, :, gs], ck[:, :, gs], cv[:, :, gs]]
    iw_pad = jnp.pad(iw, ((0, 0), (0, 0), (0, SUBLANES - IDX_HEADS)))
    w_t = jnp.transpose(jnp.concatenate([av, av, iw_pad], axis=2), (0, 2, 1))
    return jnp.concatenate(parts, axis=2).astype(bf16), w_t.astype(bf16), gates.astype(bf16)


def _segment_mean_matrix(width):
    seg = jnp.arange(width) // HEAD_DIM
    return jnp.where(seg[:, None] == seg[None, :], 1.0 / HEAD_DIM, 0.0).astype(bf16)


def _tile_gain(g, reps):
    return jnp.tile(g[:, None, :], (1, 1, reps))


def kernel(x, ffn1_norm, ffn1_gate, ffn1_up, ffn1_down, mix_norm, w_in, a_q_norm, a_k_norm, s5_lam_re, s5_lam_im, s5_log_dt, s5_b_re, s5_b_im, s5_c_re, s5_c_im, s5_d, s5_glu_a, s5_glu_b, c_q_norm, c_k_norm, w_branch_a, w_branch_b, w_branch_c, w_out, ffn2_norm, ffn2_gate, ffn2_up, ffn2_down):
    bsz, seq, _ = x.shape
    depth = w_in.shape[0]
    n = bsz * seq
    x2 = x.reshape(n, D_MODEL)
    assert TOKEN_TILE == KEY_CHUNK and seq % KEY_CHUNK == 0
    seg_mean = _segment_mean_matrix(A_WIDTH)
    grp_per_slab = S5_SLAB // S5_STATE
    cast = lambda w: w.astype(bf16)
    row = lambda g: g[:, None, :]
    ffn1 = (row(ffn1_norm), cast(ffn1_gate), cast(ffn1_up), cast(ffn1_down))
    ffn2 = (row(ffn2_norm), cast(ffn2_gate), cast(ffn2_up), cast(ffn2_down))
    w_packed, w_t, w_gates = _pack_w_in(w_in)
    head_gains = (_tile_gain(a_q_norm, A_HEADS), _tile_gain(a_k_norm, 2),
                  _tile_gain(c_q_norm, C_HEADS_PER_GROUP), _tile_gain(c_k_norm, C_HEADS_PER_GROUP))
    glu = (cast(s5_glu_a), cast(s5_glu_b))
    merge_w = (row(mix_norm), w_gates, cast(w_branch_a), cast(w_branch_b), cast(w_branch_c), cast(w_out))
    r3 = lambda a: a.reshape(bsz, seq, a.shape[-1])
    b_t = lambda b: jnp.tile(jnp.transpose(b, (2, 0, 1)).reshape(S5_GROUP, S5_LANES), (grp_per_slab, 1))
    c_t = lambda c: jnp.tile(jnp.transpose(c, (0, 2, 1)).reshape(S5_LANES, S5_GROUP), (1, grp_per_slab))
    for l in range(depth):
        x2 = _ffn(x2, l, *ffn1)
        aq, akk, vvt, iq, ikk, iwt, su, c0, c1, c2 = _in_proj(x2, l, row(mix_norm), w_packed, w_t, seg_mean,
                                                               *head_gains)
        vvt = vvt.reshape(bsz, seq // KEY_CHUNK, LANES, KEY_CHUNK)
        ya = _dsa(r3(aq), r3(iq), iwt, r3(akk), vvt, r3(ikk)).reshape(n, A_WIDTH)
        yb = _s5(r3(su), l, s5_lam_re[l].reshape(1, -1), s5_lam_im[l].reshape(1, -1),
                 jnp.repeat(s5_log_dt[l], S5_STATE).reshape(1, -1),
                 b_t(s5_b_re[l]), b_t(s5_b_im[l]), c_t(s5_c_re[l]), c_t(s5_c_im[l]),
                 s5_d[l].reshape(1, -1), *glu).reshape(n, S5_WIDTH)
        os_, lses = [], []
        for cg, (_, dilation) in zip((c0, c1, c2), DIL_PAIRS):
            o, lse = _dilated(r3(cg), dilation)
            os_.append(o)
            lses.append(lse)
        x2 = _merge(x2, l, ya, yb, os_, lses, *merge_w)
        x2 = _ffn(x2, l, *ffn2)
    return x2.reshape(bsz, seq, D_MODEL)
```

```python
import functools
import math

import jax
import jax.numpy as jnp
from jax import lax
from jax.experimental import pallas as pl
from jax.experimental.pallas import tpu as pltpu

f32 = jnp.float32
bf16 = jnp.bfloat16
i32 = jnp.int32

D_MODEL = 1024
D_FF = 2816
HEAD_DIM = 64
RMS_EPS = 1e-6
A_HEADS = 8
A_WIDTH = A_HEADS * HEAD_DIM
IDX_HEADS = 4
IDX_DIM = 64
TOPK_MAX = 256
S5_WIDTH = 512
S5_GROUP = 16
S5_GROUPS = 32
S5_STATE = 64
S5_LANES = S5_GROUPS * S5_STATE
DIL_PAIRS = ((128, 1), (512, 4), (2048, 16))
C_HEADS_PER_GROUP = 4
C_OUT = C_HEADS_PER_GROUP * HEAD_DIM
C_WIDTH = 3 * C_OUT
IN_SIZES = (A_WIDTH, HEAD_DIM, HEAD_DIM, IDX_HEADS * IDX_DIM, IDX_DIM, IDX_HEADS,
            S5_WIDTH, C_WIDTH, C_WIDTH, C_WIDTH, 3 * D_MODEL)

LANES = 128
SUBLANES = 8
VMEM_LIMIT_BYTES = 56 * 1024 * 1024

TOKEN_TILE = 512
FF_CHUNK = 256
Q_TILE = 128
KEY_CHUNK = 512
COUNT_ROWS = 64
S5_TIME_TILE = 256
S5_SLAB = 512
S5_SCAN_LANES = 512
DIL_SPAN = 128
DIL_ROWS = 512

NEG_INF = float("-inf")
LOG2_E = math.log2(math.e)
INT_MIN = -2 ** 31
NEG_INF_KEY = INT_MIN + 0x7FFFFF

_NT = (((1,), (1,)), ((), ()))


def _params(*sem):
    return pltpu.CompilerParams(dimension_semantics=sem, vmem_limit_bytes=VMEM_LIMIT_BYTES)


def _resident(shape, layer=None):
    nd = len(shape)
    if layer is None:
        return pl.BlockSpec(shape, lambda *_: (0,) * nd, pipeline_mode=pl.Buffered(1))
    return pl.BlockSpec((None,) + tuple(shape), lambda *_: (layer,) + (0,) * nd, pipeline_mode=pl.Buffered(1))


def _dot(a, b):
    return jnp.dot(a, b, preferred_element_type=f32)


def _dot_nt(a, b):
    return lax.dot_general(a, b, _NT, preferred_element_type=f32)


def _rmsnorm_rows(x, gain):
    return x * lax.rsqrt(jnp.mean(x * x, axis=-1, keepdims=True) + RMS_EPS) * gain


def _head_rmsnorm(z, gain_tiled, seg_mean):
    ms = _dot((z * z).astype(bf16), seg_mean)
    return z * lax.rsqrt(ms + RMS_EPS) * gain_tiled


def _ffn_body(x_ref, g_ref, wg_ref, wu_ref, wd_ref, o_ref, acc_ref):
    x = x_ref[...]
    h = _rmsnorm_rows(x, g_ref[...]).astype(bf16)
    for c in range(D_FF // FF_CHUNK):
        sl = slice(c * FF_CHUNK, (c + 1) * FF_CHUNK)
        gate = _dot(h, wg_ref[:, sl])
        up = _dot(h, wu_ref[:, sl])
        act = (gate * jax.nn.sigmoid(gate) * up).astype(bf16)
        contrib = _dot(act, wd_ref[sl, :])
        if c == 0:
            acc_ref[...] = contrib
        else:
            acc_ref[...] += contrib
    o_ref[...] = x + 0.5 * acc_ref[...]


def _ffn(x2, layer, gain, w_gate, w_up, w_down):
    n = x2.shape[0]
    tm = min(TOKEN_TILE, n)
    tile = pl.BlockSpec((tm, D_MODEL), lambda i: (i, 0))
    return pl.pallas_call(
        _ffn_body,
        grid=(n // tm,),
        in_specs=[tile, _resident((1, D_MODEL), layer), _resident((D_MODEL, D_FF), layer),
                  _resident((D_MODEL, D_FF), layer), _resident((D_FF, D_MODEL), layer)],
        out_specs=tile,
        out_shape=jax.ShapeDtypeStruct((n, D_MODEL), f32),
        scratch_shapes=[pltpu.VMEM((tm, D_MODEL), f32)],
        compiler_params=_params("parallel"),
        name="ffn",
    )(x2, gain, w_gate, w_up, w_down)


_P_AQ = 0
_P_AKK = _P_AQ + A_WIDTH
_P_IQ = _P_AKK + LANES
_P_IKK = _P_IQ + IDX_HEADS * IDX_DIM
_P_SU = _P_IKK + LANES
_P_C = _P_SU + S5_WIDTH
_P_TOTAL = _P_C + 3 * C_WIDTH
_T_ROWS = LANES + SUBLANES


def _in_proj_body(x_ref, g_ref, w_ref, wt_ref, seg_ref, aqg_ref, akg_ref, cqg_ref, ckg_ref,
                  aq_ref, akk_ref, vvt_ref, iq_ref, ikk_ref, iwt_ref, su_ref, c0_ref, c1_ref, c2_ref):
    h = _rmsnorm_rows(x_ref[...], g_ref[...]).astype(bf16)

    def proj(start, width):
        return _dot(h, w_ref[:, start:start + width])

    seg = seg_ref[...]
    aq = _head_rmsnorm(proj(_P_AQ, A_WIDTH), aqg_ref[...], seg) * (HEAD_DIM ** -0.5)
    aq_ref[...] = aq.astype(bf16)
    akk_ref[...] = _head_rmsnorm(proj(_P_AKK, LANES), akg_ref[...], seg[:LANES, :LANES]).astype(bf16)
    iq_ref[...] = proj(_P_IQ, IDX_HEADS * IDX_DIM).astype(bf16)
    ikk_ref[...] = proj(_P_IKK, LANES).astype(bf16)
    su_ref[...] = proj(_P_SU, S5_WIDTH)
    vvt_ref[0] = _dot_nt(wt_ref[:LANES, :], h).astype(bf16)
    iwt_ref[0] = _dot_nt(wt_ref[LANES:, :], h)
    for g, c_ref in enumerate((c0_ref, c1_ref, c2_ref)):
        base = _P_C + g * C_WIDTH
        cq = _head_rmsnorm(proj(base, C_OUT), cqg_ref[...], seg[:C_OUT, :C_OUT])
        ck = _head_rmsnorm(proj(base + C_OUT, C_OUT), ckg_ref[...], seg[:C_OUT, :C_OUT])
        c_ref[:, 0:C_OUT] = cq.astype(bf16)
        c_ref[:, C_OUT:2 * C_OUT] = ck.astype(bf16)
        c_ref[:, 2 * C_OUT:] = proj(base + 2 * C_OUT, C_OUT).astype(bf16)


def _in_proj(x2, layer, gain, w_packed, w_t, seg_mean, aq_gain, ak_gain, cq_gain, ck_gain):
    n = x2.shape[0]
    tm = min(TOKEN_TILE, n)
    rows = lambda w: pl.BlockSpec((tm, w), lambda i: (i, 0))
    cols = lambda r: pl.BlockSpec((1, r, tm), lambda i: (i, 0, 0))
    outs = [(rows(A_WIDTH), (n, A_WIDTH), bf16), (rows(LANES), (n, LANES), bf16),
            (cols(LANES), (n // tm, LANES, tm), bf16), (rows(IDX_HEADS * IDX_DIM), (n, IDX_HEADS * IDX_DIM), bf16),
            (rows(LANES), (n, LANES), bf16), (cols(SUBLANES), (n // tm, SUBLANES, tm), f32),
            (rows(S5_WIDTH), (n, S5_WIDTH), f32)] + [(rows(C_WIDTH), (n, C_WIDTH), bf16)] * 3
    return pl.pallas_call(
        _in_proj_body,
        grid=(n // tm,),
        in_specs=[rows(D_MODEL), _resident((1, D_MODEL), layer),
                  _resident((D_MODEL, _P_TOTAL), layer), _resident((_T_ROWS, D_MODEL), layer),
                  _resident((A_WIDTH, A_WIDTH)),
                  _resident((1, A_WIDTH), layer), _resident((1, LANES), layer),
                  _resident((1, C_OUT), layer), _resident((1, C_OUT), layer)],
        out_specs=[o[0] for o in outs],
        out_shape=[jax.ShapeDtypeStruct(o[1], o[2]) for o in outs],
        compiler_params=_params("parallel"),
        name="in_proj",
    )(x2, gain, w_packed, w_t, seg_mean, aq_gain, ak_gain, cq_gain, ck_gain)


def _float_order_key(x):
    bits = pltpu.bitcast(x, i32)
    return jnp.where(bits < 0, bits ^ 0x7FFFFFFF, bits)


def _fold_rows(x, op, keep=SUBLANES):
    return op(x.reshape(x.shape[0] // keep, keep, x.shape[1]), axis=0)


def _dsa_body(topk, aq_ref, iq_ref, iwt_ref, kk_ref, vvt_ref, ikk_ref, rank_ref, o_ref,
              key_ref, s_ref, m_ref, l_ref, acc_ref):
    qb = pl.program_id(1)
    n_chunks = lax.shift_right_logical(qb * Q_TILE, int(math.log2(KEY_CHUNK))) + 1
    q_pos = qb * Q_TILE + lax.broadcasted_iota(i32, (1, Q_TILE), 1)
    key_iota = lax.broadcasted_iota(i32, (KEY_CHUNK, Q_TILE), 0)
    low_half = lax.broadcasted_iota(i32, (Q_TILE, LANES), 1) < HEAD_DIM
    zero_b = jnp.zeros((), bf16)

    def chunk_rows(c):
        return pl.ds(pl.multiple_of(c * KEY_CHUNK, KEY_CHUNK), KEY_CHUNK)

    def stack_heads(x, n_heads):
        blocks = []
        for h in range(n_heads):
            pair = x[:, (h // 2) * LANES:(h // 2 + 1) * LANES]
            blocks.append(jnp.where(low_half if h % 2 == 0 else ~low_half, pair, zero_b))
        return jnp.concatenate(blocks, axis=0)

    iq_all = stack_heads(iq_ref[0], IDX_HEADS)
    w_t = iwt_ref[0] * ((IDX_DIM ** -0.5) * (IDX_HEADS ** -0.5))

    def score_chunk(c, _, diagonal):
        logits = _dot_nt(ikk_ref[0, chunk_rows(c), :], iq_all)
        score = None
        for h in range(IDX_HEADS):
            term = jnp.maximum(logits[:, h * Q_TILE:(h + 1) * Q_TILE], 0.0) * w_t[h:h + 1, :]
            score = term if score is None else score + term
        score = jnp.where(score == 0.0, 0.0, score)
        if diagonal:
            score = jnp.where(c * KEY_CHUNK + key_iota <= q_pos, score, NEG_INF)
        key_ref[c] = _float_order_key(score)
        return 0

    lax.fori_loop(0, n_chunks - 1, functools.partial(score_chunk, diagonal=False), 0)
    score_chunk(n_chunks - 1, 0, diagonal=True)

    def count(pred):
        def body(c, acc):
            return acc + _fold_rows(jnp.where(pred(key_ref[c]), 1.0, 0.0), jnp.sum, COUNT_ROWS)
        acc = lax.fori_loop(0, n_chunks, body, jnp.zeros((COUNT_ROWS, Q_TILE), f32))
        return jnp.sum(acc, axis=0, keepdims=True)

    def bit_step(i, prefix):
        cand_prefix = prefix | lax.shift_left(jnp.int32(1), 31 - i)
        cand = cand_prefix ^ INT_MIN
        cnt = count(lambda k: k >= cand)
        return jnp.where(cnt >= topk, cand_prefix, prefix)

    prefix = lax.fori_loop(0, 32, bit_step, jnp.zeros((1, Q_TILE), i32))
    thr = prefix ^ INT_MIN
    n_gt = count(lambda k: k > thr)
    need = jnp.where(thr <= NEG_INF_KEY, 0.0, topk - n_gt)
    rank_mat = rank_ref[...]

    q_all = stack_heads(aq_ref[0], A_HEADS)

    m_ref[...] = jnp.full(m_ref.shape, NEG_INF, f32)

    def logits_chunk(c, seen):
        k = key_ref[c]
        eq = k == thr
        tie = jnp.where(eq, 1.0, 0.0)
        rank = _dot(rank_mat, tie.astype(bf16)) + seen
        order = jnp.where(eq, rank, jnp.where(k > thr, -1.0, 2.0 ** 30))
        bias = jnp.where(order < need, 0.0, NEG_INF)
        s = _dot_nt(kk_ref[0, chunk_rows(c), :], q_all)
        for h in range(A_HEADS):
            cols = slice(h * Q_TILE, (h + 1) * Q_TILE)
            sh = (s[:, cols] + bias) * LOG2_E
            s_ref[c, :, cols] = sh
            m_ref[:, cols] = jnp.maximum(m_ref[:, cols], _fold_rows(sh, jnp.max))
        return seen + jnp.sum(_fold_rows(tie, jnp.sum, COUNT_ROWS), axis=0, keepdims=True)

    lax.fori_loop(0, n_chunks, logits_chunk, jnp.zeros((1, Q_TILE), f32))
    m = jnp.max(m_ref[...], axis=0, keepdims=True)

    l_ref[...] = jnp.zeros(l_ref.shape, f32)
    acc_ref[...] = jnp.zeros(acc_ref.shape, f32)

    def softmax_chunk(c, _):
        p = jnp.exp2(s_ref[c] - m)
        l_ref[...] += _fold_rows(p, jnp.sum)
        acc_ref[...] += _dot(vvt_ref[0, c], p.astype(bf16))
        return 0

    lax.fori_loop(0, n_chunks, softmax_chunk, 0)
    out_t = acc_ref[...] / jnp.sum(l_ref[...], axis=0, keepdims=True)
    top_rows = lax.broadcasted_iota(i32, (LANES, Q_TILE), 0) < HEAD_DIM
    for j in range(A_HEADS // 2):
        even = out_t[:, (2 * j) * Q_TILE:(2 * j + 1) * Q_TILE]
        odd = out_t[:, (2 * j + 1) * Q_TILE:(2 * j + 2) * Q_TILE]
        o_ref[0, :, j * LANES:(j + 1) * LANES] = jnp.where(top_rows, even, odd).T.astype(bf16)


def _tie_rank_matrix():
    i = jnp.arange(KEY_CHUNK)
    return (i[None, :] < i[:, None]).astype(bf16)


def _dsa(aq, iq, iwt, akk, vvt, ikk):
    bsz, seq, _ = aq.shape
    topk = min(TOPK_MAX, seq // 4)
    n_kc = seq // KEY_CHUNK
    q_per_chunk = KEY_CHUNK // Q_TILE

    def q_spec(width):
        return pl.BlockSpec((1, Q_TILE, width), lambda b, q: (b, q, 0))

    kv_spec = pl.BlockSpec((1, seq, LANES), lambda b, q: (b, 0, 0))
    return pl.pallas_call(
        functools.partial(_dsa_body, topk),
        grid=(bsz, seq // Q_TILE),
        in_specs=[q_spec(A_WIDTH), q_spec(IDX_HEADS * IDX_DIM),
                  pl.BlockSpec((1, SUBLANES, Q_TILE), lambda b, q: (b * n_kc + q // q_per_chunk, 0, q % q_per_chunk)),
                  kv_spec, pl.BlockSpec((1, n_kc, LANES, KEY_CHUNK), lambda b, q: (b, 0, 0, 0)), kv_spec,
                  _resident((KEY_CHUNK, KEY_CHUNK))],
        out_specs=q_spec(A_WIDTH),
        out_shape=jax.ShapeDtypeStruct((bsz, seq, A_WIDTH), bf16),
        scratch_shapes=[pltpu.VMEM((n_kc, KEY_CHUNK, Q_TILE), i32),
                        pltpu.VMEM((n_kc, KEY_CHUNK, A_HEADS * Q_TILE), f32),
                        pltpu.VMEM((SUBLANES, A_HEADS * Q_TILE), f32),
                        pltpu.VMEM((SUBLANES, A_HEADS * Q_TILE), f32),
                        pltpu.VMEM((LANES, A_HEADS * Q_TILE), f32)],
        compiler_params=_params("parallel", "arbitrary"),
        name="dsa",
    )(aq, iq, iwt, akk, vvt, ikk, _tie_rank_matrix())


def _cmul(ar, ai, br, bi):
    return ar * br - ai * bi, ar * bi + ai * br


def _s5_prepare(lam_re_ref, lam_im_ref, logdt_ref, bre_ref, bim_ref, cre_ref, cim_ref,
                bmat_re, bmat_im, cmat_re, cmat_im, shift_re, shift_im, pow_re, pow_im):
    lr = jnp.minimum(lam_re_ref[...], -1e-4)
    li = lam_im_ref[...]
    dt = jnp.exp(logdt_ref[...])
    mag = jnp.exp(lr * dt)
    l1r, l1i = mag * jnp.cos(li * dt), mag * jnp.sin(li * dt)
    den = lr * lr + li * li
    cr = ((l1r - 1.0) * lr + l1i * li) / den
    ci = (l1i * lr - (l1r - 1.0) * li) / den

    n_slab = S5_LANES // S5_SLAB
    row_g = lax.broadcasted_iota(i32, (LANES, S5_SLAB), 0) // S5_GROUP
    lane_g = lax.broadcasted_iota(i32, (LANES, S5_SLAB), 1) // S5_STATE
    diag_b = row_g == lane_g
    row_g2 = lax.broadcasted_iota(i32, (S5_SLAB, LANES), 0) // S5_STATE
    lane_g2 = lax.broadcasted_iota(i32, (S5_SLAB, LANES), 1) // S5_GROUP
    diag_c = row_g2 == lane_g2
    for j in range(n_slab):
        sl = slice(j * S5_SLAB, (j + 1) * S5_SLAB)
        bbr, bbi = _cmul(cr[:, sl], ci[:, sl], bre_ref[:, sl], bim_ref[:, sl])
        bmat_re[j] = jnp.where(diag_b, bbr, 0.0).astype(bf16)
        bmat_im[j] = jnp.where(diag_b, bbi, 0.0).astype(bf16)
        cmat_re[j] = jnp.where(diag_c, cre_ref[sl, :], 0.0).astype(bf16)
        cmat_im[j] = jnp.where(diag_c, -cim_ref[sl, :], 0.0).astype(bf16)

    l2r, l2i = _cmul(l1r, l1i, l1r, l1i)
    l3r, l3i = _cmul(l2r, l2i, l1r, l1i)
    l4r, l4i = _cmul(l2r, l2i, l2r, l2i)
    l5r, l5i = _cmul(l4r, l4i, l1r, l1i)
    l6r, l6i = _cmul(l4r, l4i, l2r, l2i)
    l7r, l7i = _cmul(l4r, l4i, l3r, l3i)
    l8r, l8i = _cmul(l4r, l4i, l4r, l4i)
    row = lax.broadcasted_iota(i32, (SUBLANES, S5_LANES), 0)
    for idx, (k, pr, pi) in enumerate(((1, l1r, l1i), (2, l2r, l2i), (4, l4r, l4i))):
        shift_re[idx] = jnp.where(row >= k, pr, 0.0)
        shift_im[idx] = jnp.where(row >= k, pi, 0.0)
    pr_acc = jnp.zeros((SUBLANES, S5_LANES), f32)
    pi_acc = jnp.zeros((SUBLANES, S5_LANES), f32)
    powers = ((l1r, l1i), (l2r, l2i), (l3r, l3i), (l4r, l4i), (l5r, l5i), (l6r, l6i), (l7r, l7i), (l8r, l8i))
    for i, (pr, pi) in enumerate(powers):
        pr_acc = jnp.where(row == i, pr, pr_acc)
        pi_acc = jnp.where(row == i, pi, pi_acc)
    pow_re[...] = pr_acc
    pow_im[...] = pi_acc


def _s5_body(u_ref, lam_re_ref, lam_im_ref, logdt_ref, bre_ref, bim_ref, cre_ref, cim_ref, d_ref,
             glu_a_ref, glu_b_ref, o_ref,
             bmat_re, bmat_im, cmat_re, cmat_im, shift_re, shift_im, pow_re, pow_im,
             carry_re, carry_im, st_re, st_im):
    t_idx = pl.program_id(1)

    @pl.when((pl.program_id(0) == 0) & (t_idx == 0))
    def _():
        _s5_prepare(lam_re_ref, lam_im_ref, logdt_ref, bre_ref, bim_ref, cre_ref, cim_ref,
                    bmat_re, bmat_im, cmat_re, cmat_im, shift_re, shift_im, pow_re, pow_im)

    @pl.when(t_idx == 0)
    def _():
        carry_re[...] = jnp.zeros(carry_re.shape, f32)
        carry_im[...] = jnp.zeros(carry_im.shape, f32)

    u = u_ref[0]
    ub = u.astype(bf16)
    n_slab = S5_LANES // S5_SLAB
    n_tiles = u.shape[0] // SUBLANES
    ys = []
    for j in range(n_slab):
        sl = slice(j * S5_SLAB, (j + 1) * S5_SLAB)
        uj = ub[:, j * LANES:(j + 1) * LANES]
        st_re[:, sl] = _dot(uj, bmat_re[j])
        st_im[:, sl] = _dot(uj, bmat_im[j])

        coef = [(shift_re[i, :, sl], shift_im[i, :, sl]) for i in range(3)]
        pwr, pwi = pow_re[:, sl], pow_im[:, sl]
        c_re, c_im = carry_re[:, sl], carry_im[:, sl]
        for n in range(n_tiles):
            rows = slice(n * SUBLANES, (n + 1) * SUBLANES)
            xr, xi = st_re[rows, sl], st_im[rows, sl]
            for i, k in enumerate((1, 2, 4)):
                dr, di = _cmul(coef[i][0], coef[i][1], pltpu.roll(xr, k, 0), pltpu.roll(xi, k, 0))
                xr, xi = xr + dr, xi + di
            dr, di = _cmul(pwr, pwi, c_re, c_im)
            xr, xi = xr + dr, xi + di
            st_re[rows, sl] = xr
            st_im[rows, sl] = xi
            c_re, c_im = xr[SUBLANES - 1:SUBLANES, :], xi[SUBLANES - 1:SUBLANES, :]
        carry_re[:, sl] = c_re
        carry_im[:, sl] = c_im
        ys.append(_dot(st_re[:, sl].astype(bf16), cmat_re[j]) + _dot(st_im[:, sl].astype(bf16), cmat_im[j]))
    y = jnp.concatenate(ys, axis=1) + d_ref[...] * u
    g = jax.nn.gelu(y).astype(bf16)
    o_ref[0] = (_dot(g, glu_a_ref[...]) * jax.nn.sigmoid(_dot(g, glu_b_ref[...]))).astype(bf16)


def _s5(su, layer, lam_re, lam_im, logdt, b_re, b_im, c_re, c_im, d_skip, glu_a, glu_b):
    bsz, seq, _ = su.shape
    tt = min(S5_TIME_TILE, seq)
    n_slab = S5_LANES // S5_SLAB
    tile = pl.BlockSpec((1, tt, S5_WIDTH), lambda b, t: (b, t, 0))
    return pl.pallas_call(
        _s5_body,
        grid=(bsz, seq // tt),
        in_specs=[tile, _resident((1, S5_LANES)), _resident((1, S5_LANES)), _resident((1, S5_LANES)),
                  _resident((LANES, S5_LANES)), _resident((LANES, S5_LANES)),
                  _resident((S5_LANES, LANES)), _resident((S5_LANES, LANES)), _resident((1, S5_WIDTH)),
                  _resident((S5_WIDTH, S5_WIDTH), layer), _resident((S5_WIDTH, S5_WIDTH), layer)],
        out_specs=tile,
        out_shape=jax.ShapeDtypeStruct((bsz, seq, S5_WIDTH), bf16),
        scratch_shapes=[pltpu.VMEM((n_slab, LANES, S5_SLAB), bf16), pltpu.VMEM((n_slab, LANES, S5_SLAB), bf16),
                        pltpu.VMEM((n_slab, S5_SLAB, LANES), bf16), pltpu.VMEM((n_slab, S5_SLAB, LANES), bf16),
                        pltpu.VMEM((3, SUBLANES, S5_LANES), f32), pltpu.VMEM((3, SUBLANES, S5_LANES), f32),
                        pltpu.VMEM((SUBLANES, S5_LANES), f32), pltpu.VMEM((SUBLANES, S5_LANES), f32),
                        pltpu.VMEM((1, S5_LANES), f32), pltpu.VMEM((1, S5_LANES), f32),
                        pltpu.VMEM((tt, S5_LANES), f32), pltpu.VMEM((tt, S5_LANES), f32)],
        compiler_params=_params("arbitrary", "arbitrary"),
        name="s5",
    )(su, lam_re, lam_im, logdt, b_re, b_im, c_re, c_im, d_skip, glu_a, glu_b)


def _dilated_body(cur_ref, prev_ref, o_ref, lse_ref):
    n = pl.program_id(2)
    rows = cur_ref.shape[1]
    qi = lax.broadcasted_iota(i32, (DIL_SPAN, 2 * DIL_SPAN), 0)
    kj = lax.broadcasted_iota(i32, (DIL_SPAN, 2 * DIL_SPAN), 1)
    band = (kj >= qi) & (kj <= qi + DIL_SPAN)
    band_first = band & ((kj >= DIL_SPAN) | (n > 0))
    low_half = lax.broadcasted_iota(i32, (DIL_SPAN, LANES), 1) < HEAD_DIM
    zero_b = jnp.zeros((), bf16)
    for blk in range(rows // DIL_SPAN):
        r0 = blk * DIL_SPAN
        q = cur_ref[0, r0:r0 + DIL_SPAN, 0:C_OUT]
        if blk == 0:
            k = jnp.concatenate([prev_ref[0, :, C_OUT:2 * C_OUT], cur_ref[0, 0:DIL_SPAN, C_OUT:2 * C_OUT]], axis=0)
            v = jnp.concatenate([prev_ref[0, :, 2 * C_OUT:], cur_ref[0, 0:DIL_SPAN, 2 * C_OUT:]], axis=0)
        else:
            k = cur_ref[0, r0 - DIL_SPAN:r0 + DIL_SPAN, C_OUT:2 * C_OUT]
            v = cur_ref[0, r0 - DIL_SPAN:r0 + DIL_SPAN, 2 * C_OUT:]
        mask = band_first if blk == 0 else band
        for j in range(C_OUT // LANES):
            ls = slice(j * LANES, (j + 1) * LANES)
            outs, lses = [], []
            for half in (low_half, ~low_half):
                s = _dot_nt(jnp.where(half, q[:, ls], zero_b), k[:, ls]) * (HEAD_DIM ** -0.5)
                s = jnp.where(mask, s, NEG_INF)
                m = jnp.max(s, axis=1, keepdims=True)
                p = jnp.exp(s - m)
                den = jnp.sum(p, axis=1, keepdims=True)
                outs.append(_dot(p.astype(bf16), v[:, ls]) / den)
                lses.append(jnp.broadcast_to(m + jnp.log(den), (DIL_SPAN, LANES)))
            o_ref[0, r0:r0 + DIL_SPAN, ls] = jnp.where(low_half, outs[0], outs[1]).astype(bf16)
            lse_ref[0, r0:r0 + DIL_SPAN, ls] = jnp.where(low_half, lses[0], lses[1])


def _dilated(qkv, dilation):
    bsz, seq, _ = qkv.shape
    sub = seq // dilation
    rows = min(DIL_ROWS, sub)
    blocks_per_step = rows // DIL_SPAN
    view = qkv.reshape(bsz, sub, dilation * C_WIDTH)
    cur = pl.BlockSpec((1, rows, C_WIDTH), lambda b, r, n: (b, n, r))
    prev = pl.BlockSpec((1, DIL_SPAN, C_WIDTH), lambda b, r, n: (b, jnp.maximum(n * blocks_per_step - 1, 0), r))
    out = pl.BlockSpec((1, rows, C_OUT), lambda b, r, n: (b, n, r))
    o, lse = pl.pallas_call(
        _dilated_body,
        grid=(bsz, dilation, sub // rows),
        in_specs=[cur, prev],
        out_specs=[out, out],
        out_shape=[jax.ShapeDtypeStruct((bsz, sub, dilation * C_OUT), bf16),
                   jax.ShapeDtypeStruct((bsz, sub, dilation * C_OUT), f32)],
        compiler_params=_params("parallel", "parallel", "arbitrary"),
        name=f"dilated_{dilation}",
    )(view, view)
    return o.reshape(bsz * seq, C_OUT), lse.reshape(bsz * seq, C_OUT)


def _merge_body(x_ref, ya_ref, yb_ref, o0_ref, o1_ref, o2_ref, l0_ref, l1_ref, l2_ref,
                g_ref, wgate_ref, wa_ref, wb_ref, wc_ref, wout_ref, out_ref):
    x = x_ref[...]
    h = _rmsnorm_rows(x, g_ref[...]).astype(bf16)
    l0, l1, l2 = l0_ref[...], l1_ref[...], l2_ref[...]
    mx = jnp.maximum(jnp.maximum(l0, l1), l2)
    e0, e1, e2 = jnp.exp(l0 - mx), jnp.exp(l1 - mx), jnp.exp(l2 - mx)
    yc = (e0 * o0_ref[...].astype(f32) + e1 * o1_ref[...].astype(f32) + e2 * o2_ref[...].astype(f32))
    yc = (yc / (e0 + e1 + e2)).astype(bf16)

    def gate(i):
        return jax.nn.sigmoid(_dot(h, wgate_ref[:, i * D_MODEL:(i + 1) * D_MODEL]))

    merged = gate(0) * _dot(ya_ref[...], wa_ref[...])
    merged = merged + gate(1) * _dot(yb_ref[...], wb_ref[...])
    merged = merged + gate(2) * _dot(yc, wc_ref[...])
    out_ref[...] = x + _dot(merged.astype(bf16), wout_ref[...])


def _merge(x2, layer, ya, yb, os_, lses, gain, w_gates, w_a, w_b, w_c, w_out):
    n = x2.shape[0]
    tm = min(TOKEN_TILE, n)

    def tile(width):
        return pl.BlockSpec((tm, width), lambda i: (i, 0))

    return pl.pallas_call(
        _merge_body,
        grid=(n // tm,),
        in_specs=[tile(D_MODEL), tile(A_WIDTH), tile(S5_WIDTH)] + [tile(C_OUT)] * 6 +
                 [_resident((1, D_MODEL), layer), _resident((D_MODEL, 3 * D_MODEL), layer),
                  _resident((A_WIDTH, D_MODEL), layer), _resident((S5_WIDTH, D_MODEL), layer),
                  _resident((C_OUT, D_MODEL), layer), _resident((D_MODEL, D_MODEL), layer)],
        out_specs=tile(D_MODEL),
        out_shape=jax.ShapeDtypeStruct((n, D_MODEL), f32),
        compiler_params=_params("parallel"),
        name="merge",
    )(x2, ya, yb, *os_, *lses, gain, w_gates, w_a, w_b, w_c, w_out)


def _pack_w_in(w_in):
    offs = [0]
    for s in IN_SIZES:
        offs.append(offs[-1] + s)
    col = lambda i: w_in[:, :, offs[i]:offs[i + 1]]
    aq, ak, av, iq, ik, iw, su, cq, ck, cv, gates = [col(i) for i in range(len(IN_SIZES))]
    parts = [aq, ak, ak, iq, ik, ik, su]
    for g in range(len(DIL_PAIRS)):
        gs = slice(g * C_OUT, (g + 1) * C_OUT)
        parts += [cq[:, :, gs], ck[:, :, gs], cv[:, :, gs]]
    iw_pad = jnp.pad(iw, ((0, 0), (0, 0), (0, SUBLANES - IDX_HEADS)))
    w_t = jnp.transpose(jnp.concatenate([av, av, iw_pad], axis=2), (0, 2, 1))
    return jnp.concatenate(parts, axis=2).astype(bf16), w_t.astype(bf16), gates.astype(bf16)


def _segment_mean_matrix(width):
    seg = jnp.arange(width) // HEAD_DIM
    return jnp.where(seg[:, None] == seg[None, :], 1.0 / HEAD_DIM, 0.0).astype(bf16)


def _tile_gain(g, reps):
    return jnp.tile(g[:, None, :], (1, 1, reps))


def kernel(x, ffn1_norm, ffn1_gate, ffn1_up, ffn1_down, mix_norm, w_in, a_q_norm, a_k_norm, s5_lam_re, s5_lam_im, s5_log_dt, s5_b_re, s5_b_im, s5_c_re, s5_c_im, s5_d, s5_glu_a, s5_glu_b, c_q_norm, c_k_norm, w_branch_a, w_branch_b, w_branch_c, w_out, ffn2_norm, ffn2_gate, ffn2_up, ffn2_down):
    bsz, seq, _ = x.shape
    depth = w_in.shape[0]
    n = bsz * seq
    x2 = x.reshape(n, D_MODEL)
    assert TOKEN_TILE == KEY_CHUNK and seq % KEY_CHUNK == 0
    seg_mean = _segment_mean_matrix(A_WIDTH)
    grp_per_slab = S5_SLAB // S5_STATE
    cast = lambda w: w.astype(bf16)
    row = lambda g: g[:, None, :]
    ffn1 = (row(ffn1_norm), cast(ffn1_gate), cast(ffn1_up), cast(ffn1_down))
    ffn2 = (row(ffn2_norm), cast(ffn2_gate), cast(ffn2_up), cast(ffn2_down))
    w_packed, w_t, w_gates = _pack_w_in(w_in)
    head_gains = (_tile_gain(a_q_norm, A_HEADS), _tile_gain(a_k_norm, 2),
                  _tile_gain(c_q_norm, C_HEADS_PER_GROUP), _tile_gain(c_k_norm, C_HEADS_PER_GROUP))
    glu = (cast(s5_glu_a), cast(s5_glu_b))
    merge_w = (row(mix_norm), w_gates, cast(w_branch_a), cast(w_branch_b), cast(w_branch_c), cast(w_out))
    r3 = lambda a: a.reshape(bsz, seq, a.shape[-1])
    b_t = lambda b: jnp.tile(jnp.transpose(b, (2, 0, 1)).reshape(S5_GROUP, S5_LANES), (grp_per_slab, 1))
    c_t = lambda c: jnp.tile(jnp.transpose(c, (0, 2, 1)).reshape(S5_LANES, S5_GROUP), (1, grp_per_slab))
    for l in range(depth):
        x2 = _ffn(x2, l, *ffn1)
        aq, akk, vvt, iq, ikk, iwt, su, c0, c1, c2 = _in_proj(x2, l, row(mix_norm), w_packed, w_t, seg_mean,
                                                               *head_gains)
        vvt = vvt.reshape(bsz, seq // KEY_CHUNK, LANES, KEY_CHUNK)
        ya = _dsa(r3(aq), r3(iq), iwt, r3(akk), vvt, r3(ikk)).reshape(n, A_WIDTH)
        yb = _s5(r3(su), l, s5_lam_re[l].reshape(1, -1), s5_lam_im[l].reshape(1, -1),
                 jnp.repeat(s5_log_dt[l], S5_STATE).reshape(1, -1),
                 b_t(s5_b_re[l]), b_t(s5_b_im[l]), c_t(s5_c_re[l]), c_t(s5_c_im[l]),
                 s5_d[l].reshape(1, -1), *glu).reshape(n, S5_WIDTH)
        os_, lses = [], []
        for cg, (_, dilation) in zip((c0, c1, c2), DIL_PAIRS):
            o, lse = _dilated(r3(cg), dilation)
            os_.append(o)
            lses.append(lse)
        x2 = _merge(x2, l, ya, yb, os_, lses, *merge_w)
        x2 = _ffn(x2, l, *ffn2)
    return x2.reshape(bsz, seq, D_MODEL)
```

```python
import functools
import math

import jax
import jax.numpy as jnp
from jax import lax
from jax.experimental import pallas as pl
from jax.experimental.pallas import tpu as pltpu

f32 = jnp.float32
bf16 = jnp.bfloat16
i32 = jnp.int32

D_MODEL = 1024
D_FF = 2816
HEAD_DIM = 64
RMS_EPS = 1e-6
A_HEADS = 8
A_WIDTH = A_HEADS * HEAD_DIM
IDX_HEADS = 4
IDX_DIM = 64
TOPK_MAX = 256
S5_WIDTH = 512
S5_GROUP = 16
S5_GROUPS = 32
S5_STATE = 64
S5_LANES = S5_GROUPS * S5_STATE
DIL_PAIRS = ((128, 1), (512, 4), (2048, 16))
C_HEADS_PER_GROUP = 4
C_OUT = C_HEADS_PER_GROUP * HEAD_DIM
C_WIDTH = 3 * C_OUT
IN_SIZES = (A_WIDTH, HEAD_DIM, HEAD_DIM, IDX_HEADS * IDX_DIM, IDX_DIM, IDX_HEADS,
            S5_WIDTH, C_WIDTH, C_WIDTH, C_WIDTH, 3 * D_MODEL)

LANES = 128
SUBLANES = 8
VMEM_LIMIT_BYTES = 56 * 1024 * 1024

TOKEN_TILE = 512
FF_CHUNK = 256
Q_TILE = 128
KEY_CHUNK = 512
COUNT_ROWS = 64
RANK_ROWS = 256
LOOP_CHUNKS = 2
S5_TIME_TILE = 256
S5_SLAB = 512
S5_SCAN_LANES = 512
DIL_SPAN = 128
DIL_ROWS = 512

NEG_INF = float("-inf")
LOG2_E = math.log2(math.e)
INT_MIN = -2 ** 31
NEG_INF_KEY = INT_MIN + 0x7FFFFF

_NT = (((1,), (1,)), ((), ()))


def _params(*sem):
    return pltpu.CompilerParams(dimension_semantics=sem, vmem_limit_bytes=VMEM_LIMIT_BYTES)


def _resident(shape, layer=None):
    nd = len(shape)
    if layer is None:
        return pl.BlockSpec(shape, lambda *_: (0,) * nd, pipeline_mode=pl.Buffered(1))
    return pl.BlockSpec((None,) + tuple(shape), lambda *_: (layer,) + (0,) * nd, pipeline_mode=pl.Buffered(1))


def _dot(a, b):
    return jnp.dot(a, b, preferred_element_type=f32)


def _dot_nt(a, b):
    return lax.dot_general(a, b, _NT, preferred_element_type=f32)


def _rmsnorm_rows(x, gain):
    return x * lax.rsqrt(jnp.mean(x * x, axis=-1, keepdims=True) + RMS_EPS) * gain


def _head_rmsnorm(z, gain_tiled, seg_mean):
    ms = _dot((z * z).astype(bf16), seg_mean)
    return z * lax.rsqrt(ms + RMS_EPS) * gain_tiled


def _ffn_body(x_ref, g_ref, wg_ref, wu_ref, wd_ref, o_ref, acc_ref):
    x = x_ref[...]
    h = _rmsnorm_rows(x, g_ref[...]).astype(bf16)
    for c in range(D_FF // FF_CHUNK):
        sl = slice(c * FF_CHUNK, (c + 1) * FF_CHUNK)
        gate = _dot(h, wg_ref[:, sl])
        up = _dot(h, wu_ref[:, sl])
        act = (gate * jax.nn.sigmoid(gate) * up).astype(bf16)
        contrib = _dot(act, wd_ref[sl, :])
        if c == 0:
            acc_ref[...] = contrib
        else:
            acc_ref[...] += contrib
    o_ref[...] = x + 0.5 * acc_ref[...]


def _ffn(x2, layer, gain, w_gate, w_up, w_down):
    n = x2.shape[0]
    tm = min(TOKEN_TILE, n)
    tile = pl.BlockSpec((tm, D_MODEL), lambda i: (i, 0))
    return pl.pallas_call(
        _ffn_body,
        grid=(n // tm,),
        in_specs=[tile, _resident((1, D_MODEL), layer), _resident((D_MODEL, D_FF), layer),
                  _resident((D_MODEL, D_FF), layer), _resident((D_FF, D_MODEL), layer)],
        out_specs=tile,
        out_shape=jax.ShapeDtypeStruct((n, D_MODEL), f32),
        scratch_shapes=[pltpu.VMEM((tm, D_MODEL), f32)],
        compiler_params=_params("parallel"),
        name="ffn",
    )(x2, gain, w_gate, w_up, w_down)


_P_AQ = 0
_P_AKK = _P_AQ + A_WIDTH
_P_IQ = _P_AKK + LANES
_P_IKK = _P_IQ + IDX_HEADS * IDX_DIM
_P_SU = _P_IKK + LANES
_P_C = _P_SU + S5_WIDTH
_P_TOTAL = _P_C + 3 * C_WIDTH
_T_ROWS = LANES + SUBLANES


def _in_proj_body(x_ref, g_ref, w_ref, wt_ref, seg_ref, aqg_ref, akg_ref, cqg_ref, ckg_ref,
                  aq_ref, akk_ref, vvt_ref, iq_ref, ikk_ref, iwt_ref, su_ref, c0_ref, c1_ref, c2_ref):
    h = _rmsnorm_rows(x_ref[...], g_ref[...]).astype(bf16)
    full = _dot(h, w_ref[...])

    def proj(start, width):
        return full[:, start:start + width]

    seg = seg_ref[...]
    aq = _head_rmsnorm(proj(_P_AQ, A_WIDTH), aqg_ref[...], seg) * (HEAD_DIM ** -0.5)
    aq_ref[...] = aq.astype(bf16)
    akk_ref[...] = _head_rmsnorm(proj(_P_AKK, LANES), akg_ref[...], seg[:LANES, :LANES]).astype(bf16)
    iq_ref[...] = proj(_P_IQ, IDX_HEADS * IDX_DIM).astype(bf16)
    ikk_ref[...] = proj(_P_IKK, LANES).astype(bf16)
    su_ref[...] = proj(_P_SU, S5_WIDTH)
    vvt_ref[0] = _dot_nt(wt_ref[:LANES, :], h).astype(bf16)
    iwt_ref[0] = _dot_nt(wt_ref[LANES:, :], h)
    for g, c_ref in enumerate((c0_ref, c1_ref, c2_ref)):
        base = _P_C + g * C_WIDTH
        cq = _head_rmsnorm(proj(base, C_OUT), cqg_ref[...], seg[:C_OUT, :C_OUT])
        ck = _head_rmsnorm(proj(base + C_OUT, C_OUT), ckg_ref[...], seg[:C_OUT, :C_OUT])
        c_ref[:, 0:C_OUT] = cq.astype(bf16)
        c_ref[:, C_OUT:2 * C_OUT] = ck.astype(bf16)
        c_ref[:, 2 * C_OUT:] = proj(base + 2 * C_OUT, C_OUT).astype(bf16)


def _in_proj(x2, layer, gain, w_packed, w_t, seg_mean, aq_gain, ak_gain, cq_gain, ck_gain):
    n = x2.shape[0]
    tm = min(TOKEN_TILE, n)
    rows = lambda w: pl.BlockSpec((tm, w), lambda i: (i, 0))
    cols = lambda r: pl.BlockSpec((1, r, tm), lambda i: (i, 0, 0))
    outs = [(rows(A_WIDTH), (n, A_WIDTH), bf16), (rows(LANES), (n, LANES), bf16),
            (cols(LANES), (n // tm, LANES, tm), bf16), (rows(IDX_HEADS * IDX_DIM), (n, IDX_HEADS * IDX_DIM), bf16),
            (rows(LANES), (n, LANES), bf16), (cols(SUBLANES), (n // tm, SUBLANES, tm), f32),
            (rows(S5_WIDTH), (n, S5_WIDTH), f32)] + [(rows(C_WIDTH), (n, C_WIDTH), bf16)] * 3
    return pl.pallas_call(
        _in_proj_body,
        grid=(n // tm,),
        in_specs=[rows(D_MODEL), _resident((1, D_MODEL), layer),
                  _resident((D_MODEL, _P_TOTAL), layer), _resident((_T_ROWS, D_MODEL), layer),
                  _resident((A_WIDTH, A_WIDTH)),
                  _resident((1, A_WIDTH), layer), _resident((1, LANES), layer),
                  _resident((1, C_OUT), layer), _resident((1, C_OUT), layer)],
        out_specs=[o[0] for o in outs],
        out_shape=[jax.ShapeDtypeStruct(o[1], o[2]) for o in outs],
        compiler_params=_params("parallel"),
        name="in_proj",
    )(x2, gain, w_packed, w_t, seg_mean, aq_gain, ak_gain, cq_gain, ck_gain)


def _float_order_key(x):
    bits = pltpu.bitcast(x, i32)
    return jnp.where(bits < 0, bits ^ 0x7FFFFFFF, bits)


def _fold_rows(x, op, keep=SUBLANES):
    return op(x.reshape(x.shape[0] // keep, keep, x.shape[1]), axis=0)


def _dsa_body(topk, aq_ref, iq_ref, iwt_ref, kk_ref, vvt_ref, ikk_ref, rank_ref, o_ref,
              key_ref, s_ref, m_ref, l_ref, acc_ref):
    qb = pl.program_id(1)
    n_chunks = lax.shift_right_logical(qb * Q_TILE, int(math.log2(KEY_CHUNK))) + 1
    q_pos = qb * Q_TILE + lax.broadcasted_iota(i32, (1, Q_TILE), 1)
    key_iota = lax.broadcasted_iota(i32, (KEY_CHUNK, Q_TILE), 0)
    low_half = lax.broadcasted_iota(i32, (Q_TILE, LANES), 1) < HEAD_DIM
    zero_b = jnp.zeros((), bf16)

    def stack_heads(x, n_heads):
        blocks = []
        for h in range(n_heads):
            pair = x[:, (h // 2) * LANES:(h // 2 + 1) * LANES]
            blocks.append(jnp.where(low_half if h % 2 == 0 else ~low_half, pair, zero_b))
        return jnp.concatenate(blocks, axis=0)

    iq_all = stack_heads(iq_ref[0], IDX_HEADS)
    w_t = iwt_ref[0] * ((IDX_DIM ** -0.5) * (IDX_HEADS ** -0.5))

    def key_rows(c, part, rows):
        return pl.ds(pl.multiple_of(c * KEY_CHUNK + part * rows, rows), rows)

    def chunk_loop(body, carry, end=n_chunks):
        n_full = lax.shift_right_logical(end, int(math.log2(LOOP_CHUNKS)))

        def multi(i, cr):
            for j in range(LOOP_CHUNKS):
                cr = body(i * LOOP_CHUNKS + j, cr)
            return cr

        carry = lax.fori_loop(0, n_full, multi, carry)
        return lax.fori_loop(n_full * LOOP_CHUNKS, end, body, carry)

    def score_chunk(c, _, diagonal=False):
        logits = _dot_nt(ikk_ref[0, key_rows(c, 0, KEY_CHUNK), :], iq_all)
        score = None
        for h in range(IDX_HEADS):
            term = jnp.maximum(logits[:, h * Q_TILE:(h + 1) * Q_TILE], 0.0) * w_t[h:h + 1, :]
            score = term if score is None else score + term
        score = jnp.where(score == 0.0, 0.0, score)
        if diagonal:
            score = jnp.where(c * KEY_CHUNK + key_iota <= q_pos, score, NEG_INF)
        key_ref[c] = _float_order_key(score)
        return 0

    chunk_loop(score_chunk, 0, end=n_chunks - 1)
    score_chunk(n_chunks - 1, 0, diagonal=True)

    def count(pred):
        def body(c, acc):
            return acc + _fold_rows(jnp.where(pred(key_ref[c]), 1.0, 0.0), jnp.sum, COUNT_ROWS)
        return jnp.sum(chunk_loop(body, jnp.zeros((COUNT_ROWS, Q_TILE), f32)), axis=0, keepdims=True)

    def bit_step(i, prefix):
        cand_prefix = prefix | lax.shift_left(jnp.int32(1), 31 - i)
        cand = cand_prefix ^ INT_MIN
        cnt = count(lambda k: k >= cand)
        return jnp.where(cnt >= topk, cand_prefix, prefix)

    prefix = lax.fori_loop(0, 32, bit_step, jnp.zeros((1, Q_TILE), i32))
    thr = prefix ^ INT_MIN
    n_gt = count(lambda k: k > thr)
    need = jnp.where(thr <= NEG_INF_KEY, 0.0, topk - n_gt)
    rank_mat = rank_ref[...]

    q_all = stack_heads(aq_ref[0], A_HEADS)

    m_ref[...] = jnp.full(m_ref.shape, NEG_INF, f32)

    def logits_chunk(c, seen):
        for part in range(KEY_CHUNK // RANK_ROWS):
            rows = slice(part * RANK_ROWS, (part + 1) * RANK_ROWS)
            k = key_ref[c, rows, :]
            eq = k == thr
            tie = jnp.where(eq, 1.0, 0.0)
            rank = _dot(rank_mat, tie.astype(bf16)) + seen
            seen = seen + jnp.sum(_fold_rows(tie, jnp.sum, COUNT_ROWS), axis=0, keepdims=True)
            order = jnp.where(eq, rank, jnp.where(k > thr, -1.0, 2.0 ** 30))
            bias = jnp.where(order < need, 0.0, NEG_INF)
            s = _dot_nt(kk_ref[0, key_rows(c, part, RANK_ROWS), :], q_all)
            for h in range(A_HEADS):
                cols = slice(h * Q_TILE, (h + 1) * Q_TILE)
                sh = (s[:, cols] + bias) * LOG2_E
                s_ref[c, rows, cols] = sh
                m_ref[:, cols] = jnp.maximum(m_ref[:, cols], _fold_rows(sh, jnp.max))
        return seen

    chunk_loop(logits_chunk, jnp.zeros((1, Q_TILE), f32))
    m = jnp.max(m_ref[...], axis=0, keepdims=True)

    l_ref[...] = jnp.zeros(l_ref.shape, f32)
    acc_ref[...] = jnp.zeros(acc_ref.shape, f32)

    def softmax_chunk(c, _):
        for part in range(KEY_CHUNK // RANK_ROWS):
            rows = slice(part * RANK_ROWS, (part + 1) * RANK_ROWS)
            p = jnp.exp2(s_ref[c, rows, :] - m)
            l_ref[...] += _fold_rows(p, jnp.sum)
            acc_ref[...] += _dot(vvt_ref[0, c, :, rows], p.astype(bf16))
        return 0

    chunk_loop(softmax_chunk, 0)
    out_t = acc_ref[...] / jnp.sum(l_ref[...], axis=0, keepdims=True)
    top_rows = lax.broadcasted_iota(i32, (LANES, Q_TILE), 0) < HEAD_DIM
    for j in range(A_HEADS // 2):
        even = out_t[:, (2 * j) * Q_TILE:(2 * j + 1) * Q_TILE]
        odd = out_t[:, (2 * j + 1) * Q_TILE:(2 * j + 2) * Q_TILE]
        o_ref[0, :, j * LANES:(j + 1) * LANES] = jnp.where(top_rows, even, odd).T.astype(bf16)


def _tie_rank_matrix():
    i = jnp.arange(RANK_ROWS)
    return (i[None, :] < i[:, None]).astype(bf16)


def _dsa(aq, iq, iwt, akk, vvt, ikk):
    bsz, seq, _ = aq.shape
    topk = min(TOPK_MAX, seq // 4)
    n_kc = seq // KEY_CHUNK
    q_per_chunk = KEY_CHUNK // Q_TILE

    def q_spec(width):
        return pl.BlockSpec((1, Q_TILE, width), lambda b, q: (b, q, 0))

    kv_spec = pl.BlockSpec((1, seq, LANES), lambda b, q: (b, 0, 0))
    return pl.pallas_call(
        functools.partial(_dsa_body, topk),
        grid=(bsz, seq // Q_TILE),
        in_specs=[q_spec(A_WIDTH), q_spec(IDX_HEADS * IDX_DIM),
                  pl.BlockSpec((1, SUBLANES, Q_TILE), lambda b, q: (b * n_kc + q // q_per_chunk, 0, q % q_per_chunk)),
                  kv_spec, pl.BlockSpec((1, n_kc, LANES, KEY_CHUNK), lambda b, q: (b, 0, 0, 0)), kv_spec,
                  _resident((RANK_ROWS, RANK_ROWS))],
        out_specs=q_spec(A_WIDTH),
        out_shape=jax.ShapeDtypeStruct((bsz, seq, A_WIDTH), bf16),
        scratch_shapes=[pltpu.VMEM((n_kc, KEY_CHUNK, Q_TILE), i32),
                        pltpu.VMEM((n_kc, KEY_CHUNK, A_HEADS * Q_TILE), f32),
                        pltpu.VMEM((SUBLANES, A_HEADS * Q_TILE), f32),
                        pltpu.VMEM((SUBLANES, A_HEADS * Q_TILE), f32),
                        pltpu.VMEM((LANES, A_HEADS * Q_TILE), f32)],
        compiler_params=_params("parallel", "arbitrary"),
        name="dsa",
    )(aq, iq, iwt, akk, vvt, ikk, _tie_rank_matrix())


def _cmul(ar, ai, br, bi):
    return ar * br - ai * bi, ar * bi + ai * br


def _s5_prepare(lam_re_ref, lam_im_ref, logdt_ref, bre_ref, bim_ref, cre_ref, cim_ref,
                bmat_re, bmat_im, cmat_re, cmat_im, shift_re, shift_im, pow_re, pow_im):
    lr = jnp.minimum(lam_re_ref[...], -1e-4)
    li = lam_im_ref[...]
    dt = jnp.exp(logdt_ref[...])
    mag = jnp.exp(lr * dt)
    l1r, l1i = mag * jnp.cos(li * dt), mag * jnp.sin(li * dt)
    den = lr * lr + li * li
    cr = ((l1r - 1.0) * lr + l1i * li) / den
    ci = (l1i * lr - (l1r - 1.0) * li) / den

    n_slab = S5_LANES // S5_SLAB
    row_g = lax.broadcasted_iota(i32, (LANES, S5_SLAB), 0) // S5_GROUP
    lane_g = lax.broadcasted_iota(i32, (LANES, S5_SLAB), 1) // S5_STATE
    diag_b = row_g == lane_g
    row_g2 = lax.broadcasted_iota(i32, (S5_SLAB, LANES), 0) // S5_STATE
    lane_g2 = lax.broadcasted_iota(i32, (S5_SLAB, LANES), 1) // S5_GROUP
    diag_c = row_g2 == lane_g2
    for j in range(n_slab):
        sl = slice(j * S5_SLAB, (j + 1) * S5_SLAB)
        bbr, bbi = _cmul(cr[:, sl], ci[:, sl], bre_ref[:, sl], bim_ref[:, sl])
        bmat_re[j] = jnp.where(diag_b, bbr, 0.0).astype(bf16)
        bmat_im[j] = jnp.where(diag_b, bbi, 0.0).astype(bf16)
        cmat_re[j] = jnp.where(diag_c, cre_ref[sl, :], 0.0).astype(bf16)
        cmat_im[j] = jnp.where(diag_c, -cim_ref[sl, :], 0.0).astype(bf16)

    l2r, l2i = _cmul(l1r, l1i, l1r, l1i)
    l3r, l3i = _cmul(l2r, l2i, l1r, l1i)
    l4r, l4i = _cmul(l2r, l2i, l2r, l2i)
    l5r, l5i = _cmul(l4r, l4i, l1r, l1i)
    l6r, l6i = _cmul(l4r, l4i, l2r, l2i)
    l7r, l7i = _cmul(l4r, l4i, l3r, l3i)
    l8r, l8i = _cmul(l4r, l4i, l4r, l4i)
    row = lax.broadcasted_iota(i32, (SUBLANES, S5_LANES), 0)
    for idx, (k, pr, pi) in enumerate(((1, l1r, l1i), (2, l2r, l2i), (4, l4r, l4i))):
        shift_re[idx] = jnp.where(row >= k, pr, 0.0)
        shift_im[idx] = jnp.where(row >= k, pi, 0.0)
    pr_acc = jnp.zeros((SUBLANES, S5_LANES), f32)
    pi_acc = jnp.zeros((SUBLANES, S5_LANES), f32)
    powers = ((l1r, l1i), (l2r, l2i), (l3r, l3i), (l4r, l4i), (l5r, l5i), (l6r, l6i), (l7r, l7i), (l8r, l8i))
    for i, (pr, pi) in enumerate(powers):
        pr_acc = jnp.where(row == i, pr, pr_acc)
        pi_acc = jnp.where(row == i, pi, pi_acc)
    pow_re[...] = pr_acc
    pow_im[...] = pi_acc


def _s5_body(u_ref, lam_re_ref, lam_im_ref, logdt_ref, bre_ref, bim_ref, cre_ref, cim_ref, d_ref,
             glu_a_ref, glu_b_ref, o_ref,
             bmat_re, bmat_im, cmat_re, cmat_im, shift_re, shift_im, pow_re, pow_im,
             carry_re, carry_im, st_re, st_im):
    t_idx = pl.program_id(1)

    @pl.when((pl.program_id(0) == 0) & (t_idx == 0))
    def _():
        _s5_prepare(lam_re_ref, lam_im_ref, logdt_ref, bre_ref, bim_ref, cre_ref, cim_ref,
                    bmat_re, bmat_im, cmat_re, cmat_im, shift_re, shift_im, pow_re, pow_im)

    @pl.when(t_idx == 0)
    def _():
        carry_re[...] = jnp.zeros(carry_re.shape, f32)
        carry_im[...] = jnp.zeros(carry_im.shape, f32)

    u = u_ref[0]
    ub = u.astype(bf16)
    n_slab = S5_LANES // S5_SLAB
    n_tiles = u.shape[0] // SUBLANES
    ys = []
    for j in range(n_slab):
        sl = slice(j * S5_SLAB, (j + 1) * S5_SLAB)
        uj = ub[:, j * LANES:(j + 1) * LANES]
        st_re[:, sl] = _dot(uj, bmat_re[j])
        st_im[:, sl] = _dot(uj, bmat_im[j])

        coef = [(shift_re[i, :, sl], shift_im[i, :, sl]) for i in range(3)]
        pwr, pwi = pow_re[:, sl], pow_im[:, sl]
        c_re, c_im = carry_re[:, sl], carry_im[:, sl]
        for n in range(n_tiles):
            rows = slice(n * SUBLANES, (n + 1) * SUBLANES)
            xr, xi = st_re[rows, sl], st_im[rows, sl]
            for i, k in enumerate((1, 2, 4)):
                dr, di = _cmul(coef[i][0], coef[i][1], pltpu.roll(xr, k, 0), pltpu.roll(xi, k, 0))
                xr, xi = xr + dr, xi + di
            dr, di = _cmul(pwr, pwi, c_re, c_im)
            xr, xi = xr + dr, xi + di
            st_re[rows, sl] = xr
            st_im[rows, sl] = xi
            c_re, c_im = xr[SUBLANES - 1:SUBLANES, :], xi[SUBLANES - 1:SUBLANES, :]
        carry_re[:, sl] = c_re
        carry_im[:, sl] = c_im
        ys.append(_dot(st_re[:, sl].astype(bf16), cmat_re[j]) + _dot(st_im[:, sl].astype(bf16), cmat_im[j]))
    y = jnp.concatenate(ys, axis=1) + d_ref[...] * u
    g = jax.nn.gelu(y).astype(bf16)
    o_ref[0] = (_dot(g, glu_a_ref[...]) * jax.nn.sigmoid(_dot(g, glu_b_ref[...]))).astype(bf16)


def _s5(su, layer, lam_re, lam_im, logdt, b_re, b_im, c_re, c_im, d_skip, glu_a, glu_b):
    bsz, seq, _ = su.shape
    tt = min(S5_TIME_TILE, seq)
    n_slab = S5_LANES // S5_SLAB
    tile = pl.BlockSpec((1, tt, S5_WIDTH), lambda b, t: (b, t, 0))
    return pl.pallas_call(
        _s5_body,
        grid=(bsz, seq // tt),
        in_specs=[tile, _resident((1, S5_LANES)), _resident((1, S5_LANES)), _resident((1, S5_LANES)),
                  _resident((LANES, S5_LANES)), _resident((LANES, S5_LANES)),
                  _resident((S5_LANES, LANES)), _resident((S5_LANES, LANES)), _resident((1, S5_WIDTH)),
                  _resident((S5_WIDTH, S5_WIDTH), layer), _resident((S5_WIDTH, S5_WIDTH), layer)],
        out_specs=tile,
        out_shape=jax.ShapeDtypeStruct((bsz, seq, S5_WIDTH), bf16),
        scratch_shapes=[pltpu.VMEM((n_slab, LANES, S5_SLAB), bf16), pltpu.VMEM((n_slab, LANES, S5_SLAB), bf16),
                        pltpu.VMEM((n_slab, S5_SLAB, LANES), bf16), pltpu.VMEM((n_slab, S5_SLAB, LANES), bf16),
                        pltpu.VMEM((3, SUBLANES, S5_LANES), f32), pltpu.VMEM((3, SUBLANES, S5_LANES), f32),
                        pltpu.VMEM((SUBLANES, S5_LANES), f32), pltpu.VMEM((SUBLANES, S5_LANES), f32),
                        pltpu.VMEM((1, S5_LANES), f32), pltpu.VMEM((1, S5_LANES), f32),
                        pltpu.VMEM((tt, S5_LANES), f32), pltpu.VMEM((tt, S5_LANES), f32)],
        compiler_params=_params("arbitrary", "arbitrary"),
        name="s5",
    )(su, lam_re, lam_im, logdt, b_re, b_im, c_re, c_im, d_skip, glu_a, glu_b)


def _dilated_body(cur_ref, prev_ref, o_ref, lse_ref):
    n = pl.program_id(2)
    rows = cur_ref.shape[1]
    qi = lax.broadcasted_iota(i32, (DIL_SPAN, 2 * DIL_SPAN), 0)
    kj = lax.broadcasted_iota(i32, (DIL_SPAN, 2 * DIL_SPAN), 1)
    band = (kj >= qi) & (kj <= qi + DIL_SPAN)
    band_first = band & ((kj >= DIL_SPAN) | (n > 0))
    low_half = lax.broadcasted_iota(i32, (DIL_SPAN, LANES), 1) < HEAD_DIM
    zero_b = jnp.zeros((), bf16)
    for blk in range(rows // DIL_SPAN):
        r0 = blk * DIL_SPAN
        q = cur_ref[0, r0:r0 + DIL_SPAN, 0:C_OUT]
        if blk == 0:
            k = jnp.concatenate([prev_ref[0, :, C_OUT:2 * C_OUT], cur_ref[0, 0:DIL_SPAN, C_OUT:2 * C_OUT]], axis=0)
            v = jnp.concatenate([prev_ref[0, :, 2 * C_OUT:], cur_ref[0, 0:DIL_SPAN, 2 * C_OUT:]], axis=0)
        else:
            k = cur_ref[0, r0 - DIL_SPAN:r0 + DIL_SPAN, C_OUT:2 * C_OUT]
            v = cur_ref[0, r0 - DIL_SPAN:r0 + DIL_SPAN, 2 * C_OUT:]
        mask = band_first if blk == 0 else band
        for j in range(C_OUT // LANES):
            ls = slice(j * LANES, (j + 1) * LANES)
            outs, lses = [], []
            for half in (low_half, ~low_half):
                s = _dot_nt(jnp.where(half, q[:, ls], zero_b), k[:, ls]) * (HEAD_DIM ** -0.5)
                s = jnp.where(mask, s, NEG_INF)
                m = jnp.max(s, axis=1, keepdims=True)
                p = jnp.exp(s - m)
                den = jnp.sum(p, axis=1, keepdims=True)
                outs.append(_dot(p.astype(bf16), v[:, ls]) / den)
                lses.append(jnp.broadcast_to(m + jnp.log(den), (DIL_SPAN, LANES)))
            o_ref[0, r0:r0 + DIL_SPAN, ls] = jnp.where(low_half, outs[0], outs[1]).astype(bf16)
            lse_ref[0, r0:r0 + DIL_SPAN, ls] = jnp.where(low_half, lses[0], lses[1])


def _dilated(qkv, dilation):
    bsz, seq, _ = qkv.shape
    sub = seq // dilation
    rows = min(DIL_ROWS, sub)
    blocks_per_step = rows // DIL_SPAN
    view = qkv.reshape(bsz, sub, dilation * C_WIDTH)
    cur = pl.BlockSpec((1, rows, C_WIDTH), lambda b, r, n: (b, n, r))
    prev = pl.BlockSpec((1, DIL_SPAN, C_WIDTH), lambda b, r, n: (b, jnp.maximum(n * blocks_per_step - 1, 0), r))
    out = pl.BlockSpec((1, rows, C_OUT), lambda b, r, n: (b, n, r))
    o, lse = pl.pallas_call(
        _dilated_body,
        grid=(bsz, dilation, sub // rows),
        in_specs=[cur, prev],
        out_specs=[out, out],
        out_shape=[jax.ShapeDtypeStruct((bsz, sub, dilation * C_OUT), bf16),
                   jax.ShapeDtypeStruct((bsz, sub, dilation * C_OUT), f32)],
        compiler_params=_params("parallel", "parallel", "arbitrary"),
        name=f"dilated_{dilation}",
    )(view, view)
    return o.reshape(bsz * seq, C_OUT), lse.reshape(bsz * seq, C_OUT)


def _merge_body(x_ref, ya_ref, yb_ref, o0_ref, o1_ref, o2_ref, l0_ref, l1_ref, l2_ref,
                g_ref, wgate_ref, wa_ref, wb_ref, wc_ref, wout_ref, out_ref):
    x = x_ref[...]
    h = _rmsnorm_rows(x, g_ref[...]).astype(bf16)
    l0, l1, l2 = l0_ref[...], l1_ref[...], l2_ref[...]
    mx = jnp.maximum(jnp.maximum(l0, l1), l2)
    e0, e1, e2 = jnp.exp(l0 - mx), jnp.exp(l1 - mx), jnp.exp(l2 - mx)
    yc = (e0 * o0_ref[...].astype(f32) + e1 * o1_ref[...].astype(f32) + e2 * o2_ref[...].astype(f32))
    yc = (yc / (e0 + e1 + e2)).astype(bf16)

    def gate(i):
        return jax.nn.sigmoid(_dot(h, wgate_ref[:, i * D_MODEL:(i + 1) * D_MODEL]))

    merged = gate(0) * _dot(ya_ref[...], wa_ref[...])
    merged = merged + gate(1) * _dot(yb_ref[...], wb_ref[...])
    merged = merged + gate(2) * _dot(yc, wc_ref[...])
    out_ref[...] = x + _dot(merged.astype(bf16), wout_ref[...])


def _merge(x2, layer, ya, yb, os_, lses, gain, w_gates, w_a, w_b, w_c, w_out):
    n = x2.shape[0]
    tm = min(TOKEN_TILE, n)

    def tile(width):
        return pl.BlockSpec((tm, width), lambda i: (i, 0))

    return pl.pallas_call(
        _merge_body,
        grid=(n // tm,),
        in_specs=[tile(D_MODEL), tile(A_WIDTH), tile(S5_WIDTH)] + [tile(C_OUT)] * 6 +
                 [_resident((1, D_MODEL), layer), _resident((D_MODEL, 3 * D_MODEL), layer),
                  _resident((A_WIDTH, D_MODEL), layer), _resident((S5_WIDTH, D_MODEL), layer),
                  _resident((C_OUT, D_MODEL), layer), _resident((D_MODEL, D_MODEL), layer)],
        out_specs=tile(D_MODEL),
        out_shape=jax.ShapeDtypeStruct((n, D_MODEL), f32),
        compiler_params=_params("parallel"),
        name="merge",
    )(x2, ya, yb, *os_, *lses, gain, w_gates, w_a, w_b, w_c, w_out)


def _pack_w_in(w_in):
    offs = [0]
    for s in IN_SIZES:
        offs.append(offs[-1] + s)
    col = lambda i: w_in[:, :, offs[i]:offs[i + 1]]
    aq, ak, av, iq, ik, iw, su, cq, ck, cv, gates = [col(i) for i in range(len(IN_SIZES))]
    parts = [aq, ak, ak, iq, ik, ik, su]
    for g in range(len(DIL_PAIRS)):
        gs = slice(g * C_OUT, (g + 1) * C_OUT)
        parts += [cq[:, :, gs], ck[:, :, gs], cv[:, :, gs]]
    iw_pad = jnp.pad(iw, ((0, 0), (0, 0), (0, SUBLANES - IDX_HEADS)))
    w_t = jnp.transpose(jnp.concatenate([av, av, iw_pad], axis=2), (0, 2, 1))
    return jnp.concatenate(parts, axis=2).astype(bf16), w_t.astype(bf16), gates.astype(bf16)


def _segment_mean_matrix(width):
    seg = jnp.arange(width) // HEAD_DIM
    return jnp.where(seg[:, None] == seg[None, :], 1.0 / HEAD_DIM, 0.0).astype(bf16)


def _tile_gain(g, reps):
    return jnp.tile(g[:, None, :], (1, 1, reps))


def kernel(x, ffn1_norm, ffn1_gate, ffn1_up, ffn1_down, mix_norm, w_in, a_q_norm, a_k_norm, s5_lam_re, s5_lam_im, s5_log_dt, s5_b_re, s5_b_im, s5_c_re, s5_c_im, s5_d, s5_glu_a, s5_glu_b, c_q_norm, c_k_norm, w_branch_a, w_branch_b, w_branch_c, w_out, ffn2_norm, ffn2_gate, ffn2_up, ffn2_down):
    bsz, seq, _ = x.shape
    depth = w_in.shape[0]
    n = bsz * seq
    x2 = x.reshape(n, D_MODEL)
    assert TOKEN_TILE == KEY_CHUNK and seq % KEY_CHUNK == 0
    seg_mean = _segment_mean_matrix(A_WIDTH)
    grp_per_slab = S5_SLAB // S5_STATE
    cast = lambda w: w.astype(bf16)
    row = lambda g: g[:, None, :]
    ffn1 = (row(ffn1_norm), cast(ffn1_gate), cast(ffn1_up), cast(ffn1_down))
    ffn2 = (row(ffn2_norm), cast(ffn2_gate), cast(ffn2_up), cast(ffn2_down))
    w_packed, w_t, w_gates = _pack_w_in(w_in)
    head_gains = (_tile_gain(a_q_norm, A_HEADS), _tile_gain(a_k_norm, 2),
                  _tile_gain(c_q_norm, C_HEADS_PER_GROUP), _tile_gain(c_k_norm, C_HEADS_PER_GROUP))
    glu = (cast(s5_glu_a), cast(s5_glu_b))
    merge_w = (row(mix_norm), w_gates, cast(w_branch_a), cast(w_branch_b), cast(w_branch_c), cast(w_out))
    r3 = lambda a: a.reshape(bsz, seq, a.shape[-1])
    b_t = lambda b: jnp.tile(jnp.transpose(b, (2, 0, 1)).reshape(S5_GROUP, S5_LANES), (grp_per_slab, 1))
    c_t = lambda c: jnp.tile(jnp.transpose(c, (0, 2, 1)).reshape(S5_LANES, S5_GROUP), (1, grp_per_slab))
    for l in range(depth):
        x2 = _ffn(x2, l, *ffn1)
        aq, akk, vvt, iq, ikk, iwt, su, c0, c1, c2 = _in_proj(x2, l, row(mix_norm), w_packed, w_t, seg_mean,
                                                               *head_gains)
        vvt = vvt.reshape(bsz, seq // KEY_CHUNK, LANES, KEY_CHUNK)
        ya = _dsa(r3(aq), r3(iq), iwt, r3(akk), vvt, r3(ikk)).reshape(n, A_WIDTH)
        yb = _s5(r3(su), l, s5_lam_re[l].reshape(1, -1), s5_lam_im[l].reshape(1, -1),
                 jnp.repeat(s5_log_dt[l], S5_STATE).reshape(1, -1),
                 b_t(s5_b_re[l]), b_t(s5_b_im[l]), c_t(s5_c_re[l]), c_t(s5_c_im[l]),
                 s5_d[l].reshape(1, -1), *glu).reshape(n, S5_WIDTH)
        os_, lses = [], []
        for cg, (_, dilation) in zip((c0, c1, c2), DIL_PAIRS):
            o, lse = _dilated(r3(cg), dilation)
            os_.append(o)
            lses.append(lse)
        x2 = _merge(x2, l, ya, yb, os_, lses, *merge_w)
        x2 = _ffn(x2, l, *ffn2)
    return x2.reshape(bsz, seq, D_MODEL)
```

```python
import functools
import math

import jax
import jax.numpy as jnp
from jax import lax
from jax.experimental import pallas as pl
from jax.experimental.pallas import tpu as pltpu

f32 = jnp.float32
bf16 = jnp.bfloat16
i32 = jnp.int32
i16 = jnp.int16

D_MODEL = 1024
D_FF = 2816
HEAD_DIM = 64
RMS_EPS = 1e-6
A_HEADS = 8
A_WIDTH = A_HEADS * HEAD_DIM
IDX_HEADS = 4
IDX_DIM = 64
TOPK_MAX = 256
S5_WIDTH = 512
S5_GROUP = 16
S5_GROUPS = 32
S5_STATE = 64
S5_LANES = S5_GROUPS * S5_STATE
DIL_PAIRS = ((128, 1), (512, 4), (2048, 16))
C_HEADS_PER_GROUP = 4
C_OUT = C_HEADS_PER_GROUP * HEAD_DIM
C_WIDTH = 3 * C_OUT
IN_SIZES = (A_WIDTH, HEAD_DIM, HEAD_DIM, IDX_HEADS * IDX_DIM, IDX_DIM, IDX_HEADS,
            S5_WIDTH, C_WIDTH, C_WIDTH, C_WIDTH, 3 * D_MODEL)

LANES = 128
SUBLANES = 8
VMEM_LIMIT_BYTES = 56 * 1024 * 1024

TOKEN_TILE = 512
FF_CHUNK = 256
Q_TILE = 128
KEY_CHUNK = 512
COUNT_ROWS = 64
RANK_ROWS = 256
LOOP_CHUNKS = 2
S5_TIME_TILE = 256
S5_SLAB = 512
S5_SCAN_LANES = 512
DIL_SPAN = 128
DIL_ROWS = 512

NEG_INF = float("-inf")
LOG2_E = math.log2(math.e)
INT_MIN = -2 ** 31
NEG_INF_KEY = INT_MIN + 0x7FFFFF
HALF_BITS = 16
HALF_BIAS = 1 << (HALF_BITS - 1)

_NT = (((1,), (1,)), ((), ()))


def _params(*sem):
    return pltpu.CompilerParams(dimension_semantics=sem, vmem_limit_bytes=VMEM_LIMIT_BYTES)


def _resident(shape, layer=None):
    nd = len(shape)
    if layer is None:
        return pl.BlockSpec(shape, lambda *_: (0,) * nd, pipeline_mode=pl.Buffered(1))
    return pl.BlockSpec((None,) + tuple(shape), lambda *_: (layer,) + (0,) * nd, pipeline_mode=pl.Buffered(1))


def _dot(a, b):
    return jnp.dot(a, b, preferred_element_type=f32)


def _dot_nt(a, b):
    return lax.dot_general(a, b, _NT, preferred_element_type=f32)


def _rmsnorm_rows(x, gain):
    return x * lax.rsqrt(jnp.mean(x * x, axis=-1, keepdims=True) + RMS_EPS) * gain


def _head_rmsnorm(z, gain_tiled, seg_mean):
    ms = _dot((z * z).astype(bf16), seg_mean)
    return z * lax.rsqrt(ms + RMS_EPS) * gain_tiled


def _ffn_body(x_ref, g_ref, wg_ref, wu_ref, wd_ref, o_ref, acc_ref):
    x = x_ref[...]
    h = _rmsnorm_rows(x, g_ref[...]).astype(bf16)
    for c in range(D_FF // FF_CHUNK):
        sl = slice(c * FF_CHUNK, (c + 1) * FF_CHUNK)
        gate = _dot(h, wg_ref[:, sl])
        up = _dot(h, wu_ref[:, sl])
        act = (gate * jax.nn.sigmoid(gate) * up).astype(bf16)
        contrib = _dot(act, wd_ref[sl, :])
        if c == 0:
            acc_ref[...] = contrib
        else:
            acc_ref[...] += contrib
    o_ref[...] = x + 0.5 * acc_ref[...]


def _ffn(x2, layer, gain, w_gate, w_up, w_down):
    n = x2.shape[0]
    tm = min(TOKEN_TILE, n)
    tile = pl.BlockSpec((tm, D_MODEL), lambda i: (i, 0))
    return pl.pallas_call(
        _ffn_body,
        grid=(n // tm,),
        in_specs=[tile, _resident((1, D_MODEL), layer), _resident((D_MODEL, D_FF), layer),
                  _resident((D_MODEL, D_FF), layer), _resident((D_FF, D_MODEL), layer)],
        out_specs=tile,
        out_shape=jax.ShapeDtypeStruct((n, D_MODEL), f32),
        scratch_shapes=[pltpu.VMEM((tm, D_MODEL), f32)],
        compiler_params=_params("parallel"),
        name="ffn",
    )(x2, gain, w_gate, w_up, w_down)


_P_AQ = 0
_P_AKK = _P_AQ + A_WIDTH
_P_IQ = _P_AKK + LANES
_P_IKK = _P_IQ + IDX_HEADS * IDX_DIM
_P_SU = _P_IKK + LANES
_P_C = _P_SU + S5_WIDTH
_P_TOTAL = _P_C + 3 * C_WIDTH
_T_ROWS = LANES + SUBLANES


def _in_proj_body(x_ref, g_ref, w_ref, wt_ref, seg_ref, aqg_ref, akg_ref, cqg_ref, ckg_ref,
                  aq_ref, akk_ref, vvt_ref, iq_ref, ikk_ref, iwt_ref, su_ref, c0_ref, c1_ref, c2_ref):
    h = _rmsnorm_rows(x_ref[...], g_ref[...]).astype(bf16)
    full = _dot(h, w_ref[...])

    def proj(start, width):
        return full[:, start:start + width]

    seg = seg_ref[...]
    aq = _head_rmsnorm(proj(_P_AQ, A_WIDTH), aqg_ref[...], seg) * (HEAD_DIM ** -0.5)
    aq_ref[...] = aq.astype(bf16)
    akk_ref[...] = _head_rmsnorm(proj(_P_AKK, LANES), akg_ref[...], seg[:LANES, :LANES]).astype(bf16)
    iq_ref[...] = proj(_P_IQ, IDX_HEADS * IDX_DIM).astype(bf16)
    ikk_ref[...] = proj(_P_IKK, LANES).astype(bf16)
    su_ref[...] = proj(_P_SU, S5_WIDTH)
    vvt_ref[0] = _dot_nt(wt_ref[:LANES, :], h).astype(bf16)
    iwt_ref[0] = _dot_nt(wt_ref[LANES:, :], h)
    for g, c_ref in enumerate((c0_ref, c1_ref, c2_ref)):
        base = _P_C + g * C_WIDTH
        cq = _head_rmsnorm(proj(base, C_OUT), cqg_ref[...], seg[:C_OUT, :C_OUT])
        ck = _head_rmsnorm(proj(base + C_OUT, C_OUT), ckg_ref[...], seg[:C_OUT, :C_OUT])
        c_ref[:, 0:C_OUT] = cq.astype(bf16)
        c_ref[:, C_OUT:2 * C_OUT] = ck.astype(bf16)
        c_ref[:, 2 * C_OUT:] = proj(base + 2 * C_OUT, C_OUT).astype(bf16)


def _in_proj(x2, layer, gain, w_packed, w_t, seg_mean, aq_gain, ak_gain, cq_gain, ck_gain):
    n = x2.shape[0]
    tm = min(TOKEN_TILE, n)
    rows = lambda w: pl.BlockSpec((tm, w), lambda i: (i, 0))
    cols = lambda r: pl.BlockSpec((1, r, tm), lambda i: (i, 0, 0))
    outs = [(rows(A_WIDTH), (n, A_WIDTH), bf16), (rows(LANES), (n, LANES), bf16),
            (cols(LANES), (n // tm, LANES, tm), bf16), (rows(IDX_HEADS * IDX_DIM), (n, IDX_HEADS * IDX_DIM), bf16),
            (rows(LANES), (n, LANES), bf16), (cols(SUBLANES), (n // tm, SUBLANES, tm), f32),
            (rows(S5_WIDTH), (n, S5_WIDTH), f32)] + [(rows(C_WIDTH), (n, C_WIDTH), bf16)] * 3
    return pl.pallas_call(
        _in_proj_body,
        grid=(n // tm,),
        in_specs=[rows(D_MODEL), _resident((1, D_MODEL), layer),
                  _resident((D_MODEL, _P_TOTAL), layer), _resident((_T_ROWS, D_MODEL), layer),
                  _resident((A_WIDTH, A_WIDTH)),
                  _resident((1, A_WIDTH), layer), _resident((1, LANES), layer),
                  _resident((1, C_OUT), layer), _resident((1, C_OUT), layer)],
        out_specs=[o[0] for o in outs],
        out_shape=[jax.ShapeDtypeStruct(o[1], o[2]) for o in outs],
        compiler_params=_params("parallel"),
        name="in_proj",
    )(x2, gain, w_packed, w_t, seg_mean, aq_gain, ak_gain, cq_gain, ck_gain)


def _float_order_key(x):
    bits = pltpu.bitcast(x, i32)
    return jnp.where(bits < 0, bits ^ 0x7FFFFFFF, bits)


def _fold_rows(x, op, keep=SUBLANES):
    return op(x.reshape(x.shape[0] // keep, keep, x.shape[1]), axis=0)


def _dsa_body(topk, aq_ref, iq_ref, iwt_ref, kk_ref, vvt_ref, ikk_ref, rank_ref, o_ref,
              key_ref, half_ref, s_ref, m_ref, l_ref, acc_ref):
    qb = pl.program_id(1)
    n_chunks = lax.shift_right_logical(qb * Q_TILE, int(math.log2(KEY_CHUNK))) + 1
    q_pos = qb * Q_TILE + lax.broadcasted_iota(i32, (1, Q_TILE), 1)
    key_iota = lax.broadcasted_iota(i32, (KEY_CHUNK, Q_TILE), 0)
    low_half = lax.broadcasted_iota(i32, (Q_TILE, LANES), 1) < HEAD_DIM
    zero_b = jnp.zeros((), bf16)

    def stack_heads(x, n_heads):
        blocks = []
        for h in range(n_heads):
            pair = x[:, (h // 2) * LANES:(h // 2 + 1) * LANES]
            blocks.append(jnp.where(low_half if h % 2 == 0 else ~low_half, pair, zero_b))
        return jnp.concatenate(blocks, axis=0)

    iq_all = stack_heads(iq_ref[0], IDX_HEADS)
    w_t = iwt_ref[0] * ((IDX_DIM ** -0.5) * (IDX_HEADS ** -0.5))

    def key_rows(c, part, rows):
        return pl.ds(pl.multiple_of(c * KEY_CHUNK + part * rows, rows), rows)

    def chunk_loop(body, carry, end=n_chunks):
        n_full = lax.shift_right_logical(end, int(math.log2(LOOP_CHUNKS)))

        def multi(i, cr):
            for j in range(LOOP_CHUNKS):
                cr = body(i * LOOP_CHUNKS + j, cr)
            return cr

        carry = lax.fori_loop(0, n_full, multi, carry)
        return lax.fori_loop(n_full * LOOP_CHUNKS, end, body, carry)

    def score_chunk(c, _, diagonal=False):
        logits = _dot_nt(ikk_ref[0, key_rows(c, 0, KEY_CHUNK), :], iq_all)
        score = None
        for h in range(IDX_HEADS):
            term = jnp.maximum(logits[:, h * Q_TILE:(h + 1) * Q_TILE], 0.0) * w_t[h:h + 1, :]
            score = term if score is None else score + term
        score = jnp.where(score == 0.0, 0.0, score)
        if diagonal:
            score = jnp.where(c * KEY_CHUNK + key_iota <= q_pos, score, NEG_INF)
        key = _float_order_key(score)
        key_ref[c] = key
        half_ref[c] = lax.shift_right_arithmetic(key, HALF_BITS).astype(i16)
        return 0

    chunk_loop(score_chunk, 0, end=n_chunks - 1)
    score_chunk(n_chunks - 1, 0, diagonal=True)

    def count(pred):
        def body(c, acc):
            return acc + _fold_rows(jnp.where(pred(key_ref[c]), 1.0, 0.0), jnp.sum, COUNT_ROWS)
        return jnp.sum(chunk_loop(body, jnp.zeros((COUNT_ROWS, Q_TILE), f32)), axis=0, keepdims=True)

    def count_half(pred):
        one, zero = jnp.ones((), i16), jnp.zeros((), i16)

        def body(c, acc):
            hit = jnp.where(pred(half_ref[c]), one, zero)
            for r in range(KEY_CHUNK // COUNT_ROWS):
                acc = acc + hit[r * COUNT_ROWS:(r + 1) * COUNT_ROWS]
            return acc

        acc = chunk_loop(body, jnp.zeros((COUNT_ROWS, Q_TILE), i16))
        return jnp.sum(acc.astype(f32), axis=0, keepdims=True)

    def search_half(wanted):
        def bit_step(i, prefix):
            cand_prefix = prefix | lax.shift_left(jnp.int32(1), HALF_BITS - 1 - i)
            cand = (cand_prefix - HALF_BIAS).astype(i16)
            cnt = count_half(lambda d: d >= cand)
            return jnp.where(cnt >= wanted, cand_prefix, prefix)
        return lax.fori_loop(0, HALF_BITS, bit_step, jnp.zeros((1, Q_TILE), i32))

    high = search_half(topk) - HALF_BIAS
    n_above = count_half(lambda d: d > high.astype(i16))

    def low_chunk(c, _):
        k = key_ref[c]
        low = lax.shift_right_arithmetic(lax.shift_left(k ^ HALF_BIAS, HALF_BITS), HALF_BITS)
        same_high = lax.shift_right_arithmetic(k, HALF_BITS) == high
        half_ref[c] = jnp.where(same_high, low, -HALF_BIAS).astype(i16)
        return 0

    chunk_loop(low_chunk, 0)
    thr = lax.shift_left(high, HALF_BITS) + search_half(topk - n_above)
    n_gt = count(lambda k: k > thr)
    need = jnp.where(thr <= NEG_INF_KEY, 0.0, topk - n_gt)
    rank_mat = rank_ref[...]

    q_all = stack_heads(aq_ref[0], A_HEADS)

    m_ref[...] = jnp.full(m_ref.shape, NEG_INF, f32)

    def logits_chunk(c, seen):
        for part in range(KEY_CHUNK // RANK_ROWS):
            rows = slice(part * RANK_ROWS, (part + 1) * RANK_ROWS)
            k = key_ref[c, rows, :]
            eq = k == thr
            tie = jnp.where(eq, 1.0, 0.0)
            rank = _dot(rank_mat, tie.astype(bf16)) + seen
            seen = seen + jnp.sum(_fold_rows(tie, jnp.sum, COUNT_ROWS), axis=0, keepdims=True)
            order = jnp.where(eq, rank, jnp.where(k > thr, -1.0, 2.0 ** 30))
            bias = jnp.where(order < need, 0.0, NEG_INF)
            s = _dot_nt(kk_ref[0, key_rows(c, part, RANK_ROWS), :], q_all)
            for h in range(A_HEADS):
                cols = slice(h * Q_TILE, (h + 1) * Q_TILE)
                sh = (s[:, cols] + bias) * LOG2_E
                s_ref[c, rows, cols] = sh
                m_ref[:, cols] = jnp.maximum(m_ref[:, cols], _fold_rows(sh, jnp.max))
        return seen

    chunk_loop(logits_chunk, jnp.zeros((1, Q_TILE), f32))
    m = jnp.max(m_ref[...], axis=0, keepdims=True)

    l_ref[...] = jnp.zeros(l_ref.shape, f32)
    acc_ref[...] = jnp.zeros(acc_ref.shape, f32)

    def softmax_chunk(c, _):
        for part in range(KEY_CHUNK // RANK_ROWS):
            rows = slice(part * RANK_ROWS, (part + 1) * RANK_ROWS)
            p = jnp.exp2(s_ref[c, rows, :] - m)
            l_ref[...] += _fold_rows(p, jnp.sum)
            acc_ref[...] += _dot(vvt_ref[0, c, :, rows], p.astype(bf16))
        return 0

    chunk_loop(softmax_chunk, 0)
    out_t = acc_ref[...] / jnp.sum(l_ref[...], axis=0, keepdims=True)
    top_rows = lax.broadcasted_iota(i32, (LANES, Q_TILE), 0) < HEAD_DIM
    for j in range(A_HEADS // 2):
        even = out_t[:, (2 * j) * Q_TILE:(2 * j + 1) * Q_TILE]
        odd = out_t[:, (2 * j + 1) * Q_TILE:(2 * j + 2) * Q_TILE]
        o_ref[0, :, j * LANES:(j + 1) * LANES] = jnp.where(top_rows, even, odd).T.astype(bf16)


def _tie_rank_matrix():
    i = jnp.arange(RANK_ROWS)
    return (i[None, :] < i[:, None]).astype(bf16)


def _dsa(aq, iq, iwt, akk, vvt, ikk):
    bsz, seq, _ = aq.shape
    topk = min(TOPK_MAX, seq // 4)
    n_kc = seq // KEY_CHUNK
    q_per_chunk = KEY_CHUNK // Q_TILE

    def q_spec(width):
        return pl.BlockSpec((1, Q_TILE, width), lambda b, q: (b, q, 0))

    kv_spec = pl.BlockSpec((1, seq, LANES), lambda b, q: (b, 0, 0))
    return pl.pallas_call(
        functools.partial(_dsa_body, topk),
        grid=(bsz, seq // Q_TILE),
        in_specs=[q_spec(A_WIDTH), q_spec(IDX_HEADS * IDX_DIM),
                  pl.BlockSpec((1, SUBLANES, Q_TILE), lambda b, q: (b * n_kc + q // q_per_chunk, 0, q % q_per_chunk)),
                  kv_spec, pl.BlockSpec((1, n_kc, LANES, KEY_CHUNK), lambda b, q: (b, 0, 0, 0)), kv_spec,
                  _resident((RANK_ROWS, RANK_ROWS))],
        out_specs=q_spec(A_WIDTH),
        out_shape=jax.ShapeDtypeStruct((bsz, seq, A_WIDTH), bf16),
        scratch_shapes=[pltpu.VMEM((n_kc, KEY_CHUNK, Q_TILE), i32),
                        pltpu.VMEM((n_kc, KEY_CHUNK, Q_TILE), i16),
                        pltpu.VMEM((n_kc, KEY_CHUNK, A_HEADS * Q_TILE), f32),
                        pltpu.VMEM((SUBLANES, A_HEADS * Q_TILE), f32),
                        pltpu.VMEM((SUBLANES, A_HEADS * Q_TILE), f32),
                        pltpu.VMEM((LANES, A_HEADS * Q_TILE), f32)],
        compiler_params=_params("parallel", "arbitrary"),
        name="dsa",
    )(aq, iq, iwt, akk, vvt, ikk, _tie_rank_matrix())


def _cmul(ar, ai, br, bi):
    return ar * br - ai * bi, ar * bi + ai * br


def _s5_prepare(lam_re_ref, lam_im_ref, logdt_ref, bre_ref, bim_ref, cre_ref, cim_ref,
                bmat_re, bmat_im, cmat_re, cmat_im, shift_re, shift_im, pow_re, pow_im):
    lr = jnp.minimum(lam_re_ref[...], -1e-4)
    li = lam_im_ref[...]
    dt = jnp.exp(logdt_ref[...])
    mag = jnp.exp(lr * dt)
    l1r, l1i = mag * jnp.cos(li * dt), mag * jnp.sin(li * dt)
    den = lr * lr + li * li
    cr = ((l1r - 1.0) * lr + l1i * li) / den
    ci = (l1i * lr - (l1r - 1.0) * li) / den

    n_slab = S5_LANES // S5_SLAB
    row_g = lax.broadcasted_iota(i32, (LANES, S5_SLAB), 0) // S5_GROUP
    lane_g = lax.broadcasted_iota(i32, (LANES, S5_SLAB), 1) // S5_STATE
    diag_b = row_g == lane_g
    row_g2 = lax.broadcasted_iota(i32, (S5_SLAB, LANES), 0) // S5_STATE
    lane_g2 = lax.broadcasted_iota(i32, (S5_SLAB, LANES), 1) // S5_GROUP
    diag_c = row_g2 == lane_g2
    for j in range(n_slab):
        sl = slice(j * S5_SLAB, (j + 1) * S5_SLAB)
        bbr, bbi = _cmul(cr[:, sl], ci[:, sl], bre_ref[:, sl], bim_ref[:, sl])
        bmat_re[j] = jnp.where(diag_b, bbr, 0.0).astype(bf16)
        bmat_im[j] = jnp.where(diag_b, bbi, 0.0).astype(bf16)
        cmat_re[j] = jnp.where(diag_c, cre_ref[sl, :], 0.0).astype(bf16)
        cmat_im[j] = jnp.where(diag_c, -cim_ref[sl, :], 0.0).astype(bf16)

    l2r, l2i = _cmul(l1r, l1i, l1r, l1i)
    l3r, l3i = _cmul(l2r, l2i, l1r, l1i)
    l4r, l4i = _cmul(l2r, l2i, l2r, l2i)
    l5r, l5i = _cmul(l4r, l4i, l1r, l1i)
    l6r, l6i = _cmul(l4r, l4i, l2r, l2i)
    l7r, l7i = _cmul(l4r, l4i, l3r, l3i)
    l8r, l8i = _cmul(l4r, l4i, l4r, l4i)
    row = lax.broadcasted_iota(i32, (SUBLANES, S5_LANES), 0)
    for idx, (k, pr, pi) in enumerate(((1, l1r, l1i), (2, l2r, l2i), (4, l4r, l4i))):
        shift_re[idx] = jnp.where(row >= k, pr, 0.0)
        shift_im[idx] = jnp.where(row >= k, pi, 0.0)
    pr_acc = jnp.zeros((SUBLANES, S5_LANES), f32)
    pi_acc = jnp.zeros((SUBLANES, S5_LANES), f32)
    powers = ((l1r, l1i), (l2r, l2i), (l3r, l3i), (l4r, l4i), (l5r, l5i), (l6r, l6i), (l7r, l7i), (l8r, l8i))
    for i, (pr, pi) in enumerate(powers):
        pr_acc = jnp.where(row == i, pr, pr_acc)
        pi_acc = jnp.where(row == i, pi, pi_acc)
    pow_re[...] = pr_acc
    pow_im[...] = pi_acc


def _s5_body(u_ref, lam_re_ref, lam_im_ref, logdt_ref, bre_ref, bim_ref, cre_ref, cim_ref, d_ref,
             glu_a_ref, glu_b_ref, o_ref,
             bmat_re, bmat_im, cmat_re, cmat_im, shift_re, shift_im, pow_re, pow_im,
             carry_re, carry_im, st_re, st_im):
    t_idx = pl.program_id(1)

    @pl.when((pl.program_id(0) == 0) & (t_idx == 0))
    def _():
        _s5_prepare(lam_re_ref, lam_im_ref, logdt_ref, bre_ref, bim_ref, cre_ref, cim_ref,
                    bmat_re, bmat_im, cmat_re, cmat_im, shift_re, shift_im, pow_re, pow_im)

    @pl.when(t_idx == 0)
    def _():
        carry_re[...] = jnp.zeros(carry_re.shape, f32)
        carry_im[...] = jnp.zeros(carry_im.shape, f32)

    u = u_ref[0]
    ub = u.astype(bf16)
    n_slab = S5_LANES // S5_SLAB
    n_tiles = u.shape[0] // SUBLANES
    ys = []
    for j in range(n_slab):
        sl = slice(j * S5_SLAB, (j + 1) * S5_SLAB)
        uj = ub[:, j * LANES:(j + 1) * LANES]
        st_re[:, sl] = _dot(uj, bmat_re[j])
        st_im[:, sl] = _dot(uj, bmat_im[j])

        coef = [(shift_re[i, :, sl], shift_im[i, :, sl]) for i in range(3)]
        pwr, pwi = pow_re[:, sl], pow_im[:, sl]
        c_re, c_im = carry_re[:, sl], carry_im[:, sl]
        for n in range(n_tiles):
            rows = slice(n * SUBLANES, (n + 1) * SUBLANES)
            xr, xi = st_re[rows, sl], st_im[rows, sl]
            for i, k in enumerate((1, 2, 4)):
                dr, di = _cmul(coef[i][0], coef[i][1], pltpu.roll(xr, k, 0), pltpu.roll(xi, k, 0))
                xr, xi = xr + dr, xi + di
            dr, di = _cmul(pwr, pwi, c_re, c_im)
            xr, xi = xr + dr, xi + di
            st_re[rows, sl] = xr
            st_im[rows, sl] = xi
            c_re, c_im = xr[SUBLANES - 1:SUBLANES, :], xi[SUBLANES - 1:SUBLANES, :]
        carry_re[:, sl] = c_re
        carry_im[:, sl] = c_im
        ys.append(_dot(st_re[:, sl].astype(bf16), cmat_re[j]) + _dot(st_im[:, sl].astype(bf16), cmat_im[j]))
    y = jnp.concatenate(ys, axis=1) + d_ref[...] * u
    g = jax.nn.gelu(y).astype(bf16)
    o_ref[0] = (_dot(g, glu_a_ref[...]) * jax.nn.sigmoid(_dot(g, glu_b_ref[...]))).astype(bf16)


def _s5(su, layer, lam_re, lam_im, logdt, b_re, b_im, c_re, c_im, d_skip, glu_a, glu_b):
    bsz, seq, _ = su.shape
    tt = min(S5_TIME_TILE, seq)
    n_slab = S5_LANES // S5_SLAB
    tile = pl.BlockSpec((1, tt, S5_WIDTH), lambda b, t: (b, t, 0))
    return pl.pallas_call(
        _s5_body,
        grid=(bsz, seq // tt),
        in_specs=[tile, _resident((1, S5_LANES)), _resident((1, S5_LANES)), _resident((1, S5_LANES)),
                  _resident((LANES, S5_LANES)), _resident((LANES, S5_LANES)),
                  _resident((S5_LANES, LANES)), _resident((S5_LANES, LANES)), _resident((1, S5_WIDTH)),
                  _resident((S5_WIDTH, S5_WIDTH), layer), _resident((S5_WIDTH, S5_WIDTH), layer)],
        out_specs=tile,
        out_shape=jax.ShapeDtypeStruct((bsz, seq, S5_WIDTH), bf16),
        scratch_shapes=[pltpu.VMEM((n_slab, LANES, S5_SLAB), bf16), pltpu.VMEM((n_slab, LANES, S5_SLAB), bf16),
                        pltpu.VMEM((n_slab, S5_SLAB, LANES), bf16), pltpu.VMEM((n_slab, S5_SLAB, LANES), bf16),
                        pltpu.VMEM((3, SUBLANES, S5_LANES), f32), pltpu.VMEM((3, SUBLANES, S5_LANES), f32),
                        pltpu.VMEM((SUBLANES, S5_LANES), f32), pltpu.VMEM((SUBLANES, S5_LANES), f32),
                        pltpu.VMEM((1, S5_LANES), f32), pltpu.VMEM((1, S5_LANES), f32),
                        pltpu.VMEM((tt, S5_LANES), f32), pltpu.VMEM((tt, S5_LANES), f32)],
        compiler_params=_params("arbitrary", "arbitrary"),
        name="s5",
    )(su, lam_re, lam_im, logdt, b_re, b_im, c_re, c_im, d_skip, glu_a, glu_b)


def _dilated_body(cur_ref, prev_ref, o_ref, lse_ref):
    n = pl.program_id(2)
    rows = cur_ref.shape[1]
    qi = lax.broadcasted_iota(i32, (DIL_SPAN, 2 * DIL_SPAN), 0)
    kj = lax.broadcasted_iota(i32, (DIL_SPAN, 2 * DIL_SPAN), 1)
    band = (kj >= qi) & (kj <= qi + DIL_SPAN)
    band_first = band & ((kj >= DIL_SPAN) | (n > 0))
    low_half = lax.broadcasted_iota(i32, (DIL_SPAN, LANES), 1) < HEAD_DIM
    zero_b = jnp.zeros((), bf16)
    for blk in range(rows // DIL_SPAN):
        r0 = blk * DIL_SPAN
        q = cur_ref[0, r0:r0 + DIL_SPAN, 0:C_OUT]
        if blk == 0:
            k = jnp.concatenate([prev_ref[0, :, C_OUT:2 * C_OUT], cur_ref[0, 0:DIL_SPAN, C_OUT:2 * C_OUT]], axis=0)
            v = jnp.concatenate([prev_ref[0, :, 2 * C_OUT:], cur_ref[0, 0:DIL_SPAN, 2 * C_OUT:]], axis=0)
        else:
            k = cur_ref[0, r0 - DIL_SPAN:r0 + DIL_SPAN, C_OUT:2 * C_OUT]
            v = cur_ref[0, r0 - DIL_SPAN:r0 + DIL_SPAN, 2 * C_OUT:]
        mask = band_first if blk == 0 else band
        for j in range(C_OUT // LANES):
            ls = slice(j * LANES, (j + 1) * LANES)
            outs, lses = [], []
            for half in (low_half, ~low_half):
                s = _dot_nt(jnp.where(half, q[:, ls], zero_b), k[:, ls]) * (HEAD_DIM ** -0.5)
                s = jnp.where(mask, s, NEG_INF)
                m = jnp.max(s, axis=1, keepdims=True)
                p = jnp.exp(s - m)
                den = jnp.sum(p, axis=1, keepdims=True)
                outs.append(_dot(p.astype(bf16), v[:, ls]) / den)
                lses.append(jnp.broadcast_to(m + jnp.log(den), (DIL_SPAN, LANES)))
            o_ref[0, r0:r0 + DIL_SPAN, ls] = jnp.where(low_half, outs[0], outs[1]).astype(bf16)
            lse_ref[0, r0:r0 + DIL_SPAN, ls] = jnp.where(low_half, lses[0], lses[1])


def _dilated(qkv, dilation):
    bsz, seq, _ = qkv.shape
    sub = seq // dilation
    rows = min(DIL_ROWS, sub)
    blocks_per_step = rows // DIL_SPAN
    view = qkv.reshape(bsz, sub, dilation * C_WIDTH)
    cur = pl.BlockSpec((1, rows, C_WIDTH), lambda b, r, n: (b, n, r))
    prev = pl.BlockSpec((1, DIL_SPAN, C_WIDTH), lambda b, r, n: (b, jnp.maximum(n * blocks_per_step - 1, 0), r))
    out = pl.BlockSpec((1, rows, C_OUT), lambda b, r, n: (b, n, r))
    o, lse = pl.pallas_call(
        _dilated_body,
        grid=(bsz, dilation, sub // rows),
        in_specs=[cur, prev],
        out_specs=[out, out],
        out_shape=[jax.ShapeDtypeStruct((bsz, sub, dilation * C_OUT), bf16),
                   jax.ShapeDtypeStruct((bsz, sub, dilation * C_OUT), f32)],
        compiler_params=_params("parallel", "parallel", "arbitrary"),
        name=f"dilated_{dilation}",
    )(view, view)
    return o.reshape(bsz * seq, C_OUT), lse.reshape(bsz * seq, C_OUT)


def _merge_body(x_ref, ya_ref, yb_ref, o0_ref, o1_ref, o2_ref, l0_ref, l1_ref, l2_ref,
                g_ref, wgate_ref, wa_ref, wb_ref, wc_ref, wout_ref, out_ref):
    x = x_ref[...]
    h = _rmsnorm_rows(x, g_ref[...]).astype(bf16)
    l0, l1, l2 = l0_ref[...], l1_ref[...], l2_ref[...]
    mx = jnp.maximum(jnp.maximum(l0, l1), l2)
    e0, e1, e2 = jnp.exp(l0 - mx), jnp.exp(l1 - mx), jnp.exp(l2 - mx)
    yc = (e0 * o0_ref[...].astype(f32) + e1 * o1_ref[...].astype(f32) + e2 * o2_ref[...].astype(f32))
    yc = (yc / (e0 + e1 + e2)).astype(bf16)

    def gate(i):
        return jax.nn.sigmoid(_dot(h, wgate_ref[:, i * D_MODEL:(i + 1) * D_MODEL]))

    merged = gate(0) * _dot(ya_ref[...], wa_ref[...])
    merged = merged + gate(1) * _dot(yb_ref[...], wb_ref[...])
    merged = merged + gate(2) * _dot(yc, wc_ref[...])
    out_ref[...] = x + _dot(merged.astype(bf16), wout_ref[...])


def _merge(x2, layer, ya, yb, os_, lses, gain, w_gates, w_a, w_b, w_c, w_out):
    n = x2.shape[0]
    tm = min(TOKEN_TILE, n)

    def tile(width):
        return pl.BlockSpec((tm, width), lambda i: (i, 0))

    return pl.pallas_call(
        _merge_body,
        grid=(n // tm,),
        in_specs=[tile(D_MODEL), tile(A_WIDTH), tile(S5_WIDTH)] + [tile(C_OUT)] * 6 +
                 [_resident((1, D_MODEL), layer), _resident((D_MODEL, 3 * D_MODEL), layer),
                  _resident((A_WIDTH, D_MODEL), layer), _resident((S5_WIDTH, D_MODEL), layer),
                  _resident((C_OUT, D_MODEL), layer), _resident((D_MODEL, D_MODEL), layer)],
        out_specs=tile(D_MODEL),
        out_shape=jax.ShapeDtypeStruct((n, D_MODEL), f32),
        compiler_params=_params("parallel"),
        name="merge",
    )(x2, ya, yb, *os_, *lses, gain, w_gates, w_a, w_b, w_c, w_out)


def _pack_w_in(w_in):
    offs = [0]
    for s in IN_SIZES:
        offs.append(offs[-1] + s)
    col = lambda i: w_in[:, :, offs[i]:offs[i + 1]]
    aq, ak, av, iq, ik, iw, su, cq, ck, cv, gates = [col(i) for i in range(len(IN_SIZES))]
    parts = [aq, ak, ak, iq, ik, ik, su]
    for g in range(len(DIL_PAIRS)):
        gs = slice(g * C_OUT, (g + 1) * C_OUT)
        parts += [cq[:, :, gs], ck[:, :, gs], cv[:, :, gs]]
    iw_pad = jnp.pad(iw, ((0, 0), (0, 0), (0, SUBLANES - IDX_HEADS)))
    w_t = jnp.transpose(jnp.concatenate([av, av, iw_pad], axis=2), (0, 2, 1))
    return jnp.concatenate(parts, axis=2).astype(bf16), w_t.astype(bf16), gates.astype(bf16)


def _segment_mean_matrix(width):
    seg = jnp.arange(width) // HEAD_DIM
    return jnp.where(seg[:, None] == seg[None, :], 1.0 / HEAD_DIM, 0.0).astype(bf16)


def _tile_gain(g, reps):
    return jnp.tile(g[:, None, :], (1, 1, reps))


def kernel(x, ffn1_norm, ffn1_gate, ffn1_up, ffn1_down, mix_norm, w_in, a_q_norm, a_k_norm, s5_lam_re, s5_lam_im, s5_log_dt, s5_b_re, s5_b_im, s5_c_re, s5_c_im, s5_d, s5_glu_a, s5_glu_b, c_q_norm, c_k_norm, w_branch_a, w_branch_b, w_branch_c, w_out, ffn2_norm, ffn2_gate, ffn2_up, ffn2_down):
    bsz, seq, _ = x.shape
    depth = w_in.shape[0]
    n = bsz * seq
    x2 = x.reshape(n, D_MODEL)
    assert TOKEN_TILE == KEY_CHUNK and seq % KEY_CHUNK == 0
    seg_mean = _segment_mean_matrix(A_WIDTH)
    grp_per_slab = S5_SLAB // S5_STATE
    cast = lambda w: w.astype(bf16)
    row = lambda g: g[:, None, :]
    ffn1 = (row(ffn1_norm), cast(ffn1_gate), cast(ffn1_up), cast(ffn1_down))
    ffn2 = (row(ffn2_norm), cast(ffn2_gate), cast(ffn2_up), cast(ffn2_down))
    w_packed, w_t, w_gates = _pack_w_in(w_in)
    head_gains = (_tile_gain(a_q_norm, A_HEADS), _tile_gain(a_k_norm, 2),
                  _tile_gain(c_q_norm, C_HEADS_PER_GROUP), _tile_gain(c_k_norm, C_HEADS_PER_GROUP))
    glu = (cast(s5_glu_a), cast(s5_glu_b))
    merge_w = (row(mix_norm), w_gates, cast(w_branch_a), cast(w_branch_b), cast(w_branch_c), cast(w_out))
    r3 = lambda a: a.reshape(bsz, seq, a.shape[-1])
    b_t = lambda b: jnp.tile(jnp.transpose(b, (2, 0, 1)).reshape(S5_GROUP, S5_LANES), (grp_per_slab, 1))
    c_t = lambda c: jnp.tile(jnp.transpose(c, (0, 2, 1)).reshape(S5_LANES, S5_GROUP), (1, grp_per_slab))
    for l in range(depth):
        x2 = _ffn(x2, l, *ffn1)
        aq, akk, vvt, iq, ikk, iwt, su, c0, c1, c2 = _in_proj(x2, l, row(mix_norm), w_packed, w_t, seg_mean,
                                                               *head_gains)
        vvt = vvt.reshape(bsz, seq // KEY_CHUNK, LANES, KEY_CHUNK)
        ya = _dsa(r3(aq), r3(iq), iwt, r3(akk), vvt, r3(ikk)).reshape(n, A_WIDTH)
        yb = _s5(r3(su), l, s5_lam_re[l].reshape(1, -1), s5_lam_im[l].reshape(1, -1),
                 jnp.repeat(s5_log_dt[l], S5_STATE).reshape(1, -1),
                 b_t(s5_b_re[l]), b_t(s5_b_im[l]), c_t(s5_c_re[l]), c_t(s5_c_im[l]),
                 s5_d[l].reshape(1, -1), *glu).reshape(n, S5_WIDTH)
        os_, lses = [], []
        for cg, (_, dilation) in zip((c0, c1, c2), DIL_PAIRS):
            o, lse = _dilated(r3(cg), dilation)
            os_.append(o)
            lses.append(lse)
        x2 = _merge(x2, l, ya, yb, os_, lses, *merge_w)
        x2 = _ffn(x2, l, *ffn2)
    return x2.reshape(bsz, seq, D_MODEL)
```

```python
import functools
import math

import jax
import jax.numpy as jnp
from jax import lax
from jax.experimental import pallas as pl
from jax.experimental.pallas import tpu as pltpu

f32 = jnp.float32
bf16 = jnp.bfloat16
i32 = jnp.int32

D_MODEL = 1024
D_FF = 2816
HEAD_DIM = 64
RMS_EPS = 1e-6
A_HEADS = 8
A_WIDTH = A_HEADS * HEAD_DIM
IDX_HEADS = 4
IDX_DIM = 64
TOPK_MAX = 256
S5_WIDTH = 512
S5_GROUP = 16
S5_GROUPS = 32
S5_STATE = 64
S5_LANES = S5_GROUPS * S5_STATE
DIL_PAIRS = ((128, 1), (512, 4), (2048, 16))
C_HEADS_PER_GROUP = 4
C_OUT = C_HEADS_PER_GROUP * HEAD_DIM
C_WIDTH = 3 * C_OUT
IN_SIZES = (A_WIDTH, HEAD_DIM, HEAD_DIM, IDX_HEADS * IDX_DIM, IDX_DIM, IDX_HEADS,
            S5_WIDTH, C_WIDTH, C_WIDTH, C_WIDTH, 3 * D_MODEL)

LANES = 128
SUBLANES = 8
VMEM_LIMIT_BYTES = 56 * 1024 * 1024

TOKEN_TILE = 512
FF_CHUNK = 256
Q_TILE = 128
KEY_CHUNK = 512
COUNT_ROWS = 64
KEY_BITS = 32
PLANE_ROWS = KEY_BITS * 8
RANK_ROWS = 256
LOOP_CHUNKS = 2
S5_TIME_TILE = 256
S5_SLAB = 512
S5_SCAN_LANES = 512
DIL_SPAN = 128
DIL_ROWS = 512

NEG_INF = float("-inf")
LOG2_E = math.log2(math.e)
INT_MIN = -2 ** 31
NEG_INF_KEY = INT_MIN + 0x7FFFFF

_NT = (((1,), (1,)), ((), ()))


def _params(*sem):
    return pltpu.CompilerParams(dimension_semantics=sem, vmem_limit_bytes=VMEM_LIMIT_BYTES)


def _resident(shape, layer=None):
    nd = len(shape)
    if layer is None:
        return pl.BlockSpec(shape, lambda *_: (0,) * nd, pipeline_mode=pl.Buffered(1))
    return pl.BlockSpec((None,) + tuple(shape), lambda *_: (layer,) + (0,) * nd, pipeline_mode=pl.Buffered(1))


def _dot(a, b):
    return jnp.dot(a, b, preferred_element_type=f32)


def _dot_nt(a, b):
    return lax.dot_general(a, b, _NT, preferred_element_type=f32)


def _rmsnorm_rows(x, gain):
    return x * lax.rsqrt(jnp.mean(x * x, axis=-1, keepdims=True) + RMS_EPS) * gain


def _head_rmsnorm(z, gain_tiled, seg_mean):
    ms = _dot((z * z).astype(bf16), seg_mean)
    return z * lax.rsqrt(ms + RMS_EPS) * gain_tiled


def _ffn_body(x_ref, g_ref, wg_ref, wu_ref, wd_ref, o_ref, acc_ref):
    x = x_ref[...]
    h = _rmsnorm_rows(x, g_ref[...]).astype(bf16)
    for c in range(D_FF // FF_CHUNK):
        sl = slice(c * FF_CHUNK, (c + 1) * FF_CHUNK)
        gate = _dot(h, wg_ref[:, sl])
        up = _dot(h, wu_ref[:, sl])
        act = (gate * jax.nn.sigmoid(gate) * up).astype(bf16)
        contrib = _dot(act, wd_ref[sl, :])
        if c == 0:
            acc_ref[...] = contrib
        else:
            acc_ref[...] += contrib
    o_ref[...] = x + 0.5 * acc_ref[...]


def _ffn(x2, layer, gain, w_gate, w_up, w_down):
    n = x2.shape[0]
    tm = min(TOKEN_TILE, n)
    tile = pl.BlockSpec((tm, D_MODEL), lambda i: (i, 0))
    return pl.pallas_call(
        _ffn_body,
        grid=(n // tm,),
        in_specs=[tile, _resident((1, D_MODEL), layer), _resident((D_MODEL, D_FF), layer),
                  _resident((D_MODEL, D_FF), layer), _resident((D_FF, D_MODEL), layer)],
        out_specs=tile,
        out_shape=jax.ShapeDtypeStruct((n, D_MODEL), f32),
        scratch_shapes=[pltpu.VMEM((tm, D_MODEL), f32)],
        compiler_params=_params("parallel"),
        name="ffn",
    )(x2, gain, w_gate, w_up, w_down)


_P_AQ = 0
_P_AKK = _P_AQ + A_WIDTH
_P_IQ = _P_AKK + LANES
_P_IKK = _P_IQ + IDX_HEADS * IDX_DIM
_P_SU = _P_IKK + LANES
_P_C = _P_SU + S5_WIDTH
_P_TOTAL = _P_C + 3 * C_WIDTH
_T_ROWS = LANES + SUBLANES


def _in_proj_body(x_ref, g_ref, w_ref, wt_ref, seg_ref, aqg_ref, akg_ref, cqg_ref, ckg_ref,
                  aq_ref, akk_ref, vvt_ref, iq_ref, ikk_ref, iwt_ref, su_ref, c0_ref, c1_ref, c2_ref):
    h = _rmsnorm_rows(x_ref[...], g_ref[...]).astype(bf16)
    full = _dot(h, w_ref[...])

    def proj(start, width):
        return full[:, start:start + width]

    seg = seg_ref[...]
    aq = _head_rmsnorm(proj(_P_AQ, A_WIDTH), aqg_ref[...], seg) * (HEAD_DIM ** -0.5)
    aq_ref[...] = aq.astype(bf16)
    akk_ref[...] = _head_rmsnorm(proj(_P_AKK, LANES), akg_ref[...], seg[:LANES, :LANES]).astype(bf16)
    iq_ref[...] = proj(_P_IQ, IDX_HEADS * IDX_DIM).astype(bf16)
    ikk_ref[...] = proj(_P_IKK, LANES).astype(bf16)
    su_ref[...] = proj(_P_SU, S5_WIDTH)
    vvt_ref[0] = _dot_nt(wt_ref[:LANES, :], h).astype(bf16)
    iwt_ref[0] = _dot_nt(wt_ref[LANES:, :], h)
    for g, c_ref in enumerate((c0_ref, c1_ref, c2_ref)):
        base = _P_C + g * C_WIDTH
        cq = _head_rmsnorm(proj(base, C_OUT), cqg_ref[...], seg[:C_OUT, :C_OUT])
        ck = _head_rmsnorm(proj(base + C_OUT, C_OUT), ckg_ref[...], seg[:C_OUT, :C_OUT])
        c_ref[:, 0:C_OUT] = cq.astype(bf16)
        c_ref[:, C_OUT:2 * C_OUT] = ck.astype(bf16)
        c_ref[:, 2 * C_OUT:] = proj(base + 2 * C_OUT, C_OUT).astype(bf16)


def _in_proj(x2, layer, gain, w_packed, w_t, seg_mean, aq_gain, ak_gain, cq_gain, ck_gain):
    n = x2.shape[0]
    tm = min(TOKEN_TILE, n)
    rows = lambda w: pl.BlockSpec((tm, w), lambda i: (i, 0))
    cols = lambda r: pl.BlockSpec((1, r, tm), lambda i: (i, 0, 0))
    outs = [(rows(A_WIDTH), (n, A_WIDTH), bf16), (rows(LANES), (n, LANES), bf16),
            (cols(LANES), (n // tm, LANES, tm), bf16), (rows(IDX_HEADS * IDX_DIM), (n, IDX_HEADS * IDX_DIM), bf16),
            (rows(LANES), (n, LANES), bf16), (cols(SUBLANES), (n // tm, SUBLANES, tm), f32),
            (rows(S5_WIDTH), (n, S5_WIDTH), f32)] + [(rows(C_WIDTH), (n, C_WIDTH), bf16)] * 3
    return pl.pallas_call(
        _in_proj_body,
        grid=(n // tm,),
        in_specs=[rows(D_MODEL), _resident((1, D_MODEL), layer),
                  _resident((D_MODEL, _P_TOTAL), layer), _resident((_T_ROWS, D_MODEL), layer),
                  _resident((A_WIDTH, A_WIDTH)),
                  _resident((1, A_WIDTH), layer), _resident((1, LANES), layer),
                  _resident((1, C_OUT), layer), _resident((1, C_OUT), layer)],
        out_specs=[o[0] for o in outs],
        out_shape=[jax.ShapeDtypeStruct(o[1], o[2]) for o in outs],
        compiler_params=_params("parallel"),
        name="in_proj",
    )(x2, gain, w_packed, w_t, seg_mean, aq_gain, ak_gain, cq_gain, ck_gain)


def _float_order_key(x):
    bits = pltpu.bitcast(x, i32)
    return jnp.where(bits < 0, bits ^ 0x7FFFFFFF, bits)


def _fold_rows(x, op, keep=SUBLANES):
    return op(x.reshape(x.shape[0] // keep, keep, x.shape[1]), axis=0)


def _bit_planes(words):
    a = [words[t * SUBLANES:(t + 1) * SUBLANES] for t in range(KEY_BITS)]
    for j, m in ((16, 0x0000FFFF), (8, 0x00FF00FF), (4, 0x0F0F0F0F), (2, 0x33333333), (1, 0x55555555)):
        shift = jnp.full(a[0].shape, j, i32)
        for k in range(KEY_BITS):
            if k & j == 0:
                t = (a[k] ^ lax.shift_right_logical(a[k + j], shift)) & m
                a[k] = a[k] ^ t
                a[k + j] = a[k + j] ^ lax.shift_left(t, shift)
    return a


def _dsa_body(topk, aq_ref, iq_ref, iwt_ref, kk_ref, vvt_ref, ikk_ref, rank_ref, o_ref,
              key_ref, plane_ref, cand_ref, s_ref, m_ref, l_ref, acc_ref):
    qb = pl.program_id(1)
    n_chunks = lax.shift_right_logical(qb * Q_TILE, int(math.log2(KEY_CHUNK))) + 1
    q_pos = qb * Q_TILE + lax.broadcasted_iota(i32, (1, Q_TILE), 1)
    key_iota = lax.broadcasted_iota(i32, (KEY_CHUNK, Q_TILE), 0)
    low_half = lax.broadcasted_iota(i32, (Q_TILE, LANES), 1) < HEAD_DIM
    zero_b = jnp.zeros((), bf16)

    def stack_heads(x, n_heads):
        blocks = []
        for h in range(n_heads):
            pair = x[:, (h // 2) * LANES:(h // 2 + 1) * LANES]
            blocks.append(jnp.where(low_half if h % 2 == 0 else ~low_half, pair, zero_b))
        return jnp.concatenate(blocks, axis=0)

    @pl.when((pl.program_id(0) == 0) & (qb == 0))
    def _():
        plane_ref[...] = jnp.zeros(plane_ref.shape, i32)

    iq_all = stack_heads(iq_ref[0], IDX_HEADS)
    w_t = iwt_ref[0] * ((IDX_DIM ** -0.5) * (IDX_HEADS ** -0.5))

    def key_rows(c, part, rows):
        return pl.ds(pl.multiple_of(c * KEY_CHUNK + part * rows, rows), rows)

    def chunk_loop(body, carry, end=n_chunks):
        n_full = lax.shift_right_logical(end, int(math.log2(LOOP_CHUNKS)))

        def multi(i, cr):
            for j in range(LOOP_CHUNKS):
                cr = body(i * LOOP_CHUNKS + j, cr)
            return cr

        carry = lax.fori_loop(0, n_full, multi, carry)
        return lax.fori_loop(n_full * LOOP_CHUNKS, end, body, carry)

    def score_chunk(c, _, diagonal=False):
        logits = _dot_nt(ikk_ref[0, key_rows(c, 0, KEY_CHUNK), :], iq_all)
        score = None
        for h in range(IDX_HEADS):
            term = jnp.maximum(logits[:, h * Q_TILE:(h + 1) * Q_TILE], 0.0) * w_t[h:h + 1, :]
            score = term if score is None else score + term
        score = jnp.where(score == 0.0, 0.0, score)
        if diagonal:
            score = jnp.where(c * KEY_CHUNK + key_iota <= q_pos, score, NEG_INF)
        key = _float_order_key(score)
        key_ref[c] = key
        for part in range(KEY_CHUNK // PLANE_ROWS):
            planes = _bit_planes(key[part * PLANE_ROWS:(part + 1) * PLANE_ROWS] ^ INT_MIN)
            for i in range(KEY_BITS):
                plane_ref[i, c * (KEY_CHUNK // PLANE_ROWS) + part] = planes[i]
        return 0

    chunk_loop(score_chunk, 0, end=n_chunks - 1)
    score_chunk(n_chunks - 1, 0, diagonal=True)

    n_planes = n_chunks * (KEY_CHUNK // PLANE_ROWS)
    block_id = lax.broadcasted_iota(i32, cand_ref.shape, 0)
    cand_ref[...] = jnp.where(block_id < n_planes, -1, 0)

    def bit_step(i, carry):
        prefix, n_above = carry
        cand = cand_ref[...]
        ones = cand & plane_ref[i]
        cnt = jnp.sum(jnp.sum(lax.population_count(ones), axis=0).astype(f32), axis=0, keepdims=True)
        accept = n_above + cnt >= topk
        cand_ref[...] = jnp.where(accept, ones, cand ^ ones)
        prefix = jnp.where(accept, prefix | lax.shift_left(jnp.int32(1), KEY_BITS - 1 - i), prefix)
        return prefix, jnp.where(accept, n_above, n_above + cnt)

    prefix, n_gt = lax.fori_loop(0, KEY_BITS, bit_step,
                                 (jnp.zeros((1, Q_TILE), i32), jnp.zeros((1, Q_TILE), f32)))
    thr = prefix ^ INT_MIN
    need = jnp.where(thr <= NEG_INF_KEY, 0.0, topk - n_gt)
    rank_mat = rank_ref[...]

    q_all = stack_heads(aq_ref[0], A_HEADS)

    m_ref[...] = jnp.full(m_ref.shape, NEG_INF, f32)

    def logits_chunk(c, seen):
        for part in range(KEY_CHUNK // RANK_ROWS):
            rows = slice(part * RANK_ROWS, (part + 1) * RANK_ROWS)
            k = key_ref[c, rows, :]
            eq = k == thr
            tie = jnp.where(eq, 1.0, 0.0)
            rank = _dot(rank_mat, tie.astype(bf16)) + seen
            seen = seen + jnp.sum(_fold_rows(tie, jnp.sum, COUNT_ROWS), axis=0, keepdims=True)
            order = jnp.where(eq, rank, jnp.where(k > thr, -1.0, 2.0 ** 30))
            bias = jnp.where(order < need, 0.0, NEG_INF)
            s = _dot_nt(kk_ref[0, key_rows(c, part, RANK_ROWS), :], q_all)
            for h in range(A_HEADS):
                cols = slice(h * Q_TILE, (h + 1) * Q_TILE)
                sh = (s[:, cols] + bias) * LOG2_E
                s_ref[c, rows, cols] = sh
                m_ref[:, cols] = jnp.maximum(m_ref[:, cols], _fold_rows(sh, jnp.max))
        return seen

    chunk_loop(logits_chunk, jnp.zeros((1, Q_TILE), f32))
    m = jnp.max(m_ref[...], axis=0, keepdims=True)

    l_ref[...] = jnp.zeros(l_ref.shape, f32)
    acc_ref[...] = jnp.zeros(acc_ref.shape, f32)

    def softmax_chunk(c, _):
        for part in range(KEY_CHUNK // RANK_ROWS):
            rows = slice(part * RANK_ROWS, (part + 1) * RANK_ROWS)
            p = jnp.exp2(s_ref[c, rows, :] - m)
            l_ref[...] += _fold_rows(p, jnp.sum)
            acc_ref[...] += _dot(vvt_ref[0, c, :, rows], p.astype(bf16))
        return 0

    chunk_loop(softmax_chunk, 0)
    out_t = acc_ref[...] / jnp.sum(l_ref[...], axis=0, keepdims=True)
    top_rows = lax.broadcasted_iota(i32, (LANES, Q_TILE), 0) < HEAD_DIM
    for j in range(A_HEADS // 2):
        even = out_t[:, (2 * j) * Q_TILE:(2 * j + 1) * Q_TILE]
        odd = out_t[:, (2 * j + 1) * Q_TILE:(2 * j + 2) * Q_TILE]
        o_ref[0, :, j * LANES:(j + 1) * LANES] = jnp.where(top_rows, even, odd).T.astype(bf16)


def _tie_rank_matrix():
    i = jnp.arange(RANK_ROWS)
    return (i[None, :] < i[:, None]).astype(bf16)


def _dsa(aq, iq, iwt, akk, vvt, ikk):
    bsz, seq, _ = aq.shape
    topk = min(TOPK_MAX, seq // 4)
    n_kc = seq // KEY_CHUNK
    q_per_chunk = KEY_CHUNK // Q_TILE

    def q_spec(width):
        return pl.BlockSpec((1, Q_TILE, width), lambda b, q: (b, q, 0))

    kv_spec = pl.BlockSpec((1, seq, LANES), lambda b, q: (b, 0, 0))
    return pl.pallas_call(
        functools.partial(_dsa_body, topk),
        grid=(bsz, seq // Q_TILE),
        in_specs=[q_spec(A_WIDTH), q_spec(IDX_HEADS * IDX_DIM),
                  pl.BlockSpec((1, SUBLANES, Q_TILE), lambda b, q: (b * n_kc + q // q_per_chunk, 0, q % q_per_chunk)),
                  kv_spec, pl.BlockSpec((1, n_kc, LANES, KEY_CHUNK), lambda b, q: (b, 0, 0, 0)), kv_spec,
                  _resident((RANK_ROWS, RANK_ROWS))],
        out_specs=q_spec(A_WIDTH),
        out_shape=jax.ShapeDtypeStruct((bsz, seq, A_WIDTH), bf16),
        scratch_shapes=[pltpu.VMEM((n_kc, KEY_CHUNK, Q_TILE), i32),
                        pltpu.VMEM((KEY_BITS, seq // PLANE_ROWS, SUBLANES, Q_TILE), i32),
                        pltpu.VMEM((seq // PLANE_ROWS, SUBLANES, Q_TILE), i32),
                        pltpu.VMEM((n_kc, KEY_CHUNK, A_HEADS * Q_TILE), f32),
                        pltpu.VMEM((SUBLANES, A_HEADS * Q_TILE), f32),
                        pltpu.VMEM((SUBLANES, A_HEADS * Q_TILE), f32),
                        pltpu.VMEM((LANES, A_HEADS * Q_TILE), f32)],
        compiler_params=_params("arbitrary", "arbitrary"),
        name="dsa",
    )(aq, iq, iwt, akk, vvt, ikk, _tie_rank_matrix())


def _cmul(ar, ai, br, bi):
    return ar * br - ai * bi, ar * bi + ai * br


def _s5_prepare(lam_re_ref, lam_im_ref, logdt_ref, bre_ref, bim_ref, cre_ref, cim_ref,
                bmat_re, bmat_im, cmat_re, cmat_im, shift_re, shift_im, pow_re, pow_im):
    lr = jnp.minimum(lam_re_ref[...], -1e-4)
    li = lam_im_ref[...]
    dt = jnp.exp(logdt_ref[...])
    mag = jnp.exp(lr * dt)
    l1r, l1i = mag * jnp.cos(li * dt), mag * jnp.sin(li * dt)
    den = lr * lr + li * li
    cr = ((l1r - 1.0) * lr + l1i * li) / den
    ci = (l1i * lr - (l1r - 1.0) * li) / den

    n_slab = S5_LANES // S5_SLAB
    row_g = lax.broadcasted_iota(i32, (LANES, S5_SLAB), 0) // S5_GROUP
    lane_g = lax.broadcasted_iota(i32, (LANES, S5_SLAB), 1) // S5_STATE
    diag_b = row_g == lane_g
    row_g2 = lax.broadcasted_iota(i32, (S5_SLAB, LANES), 0) // S5_STATE
    lane_g2 = lax.broadcasted_iota(i32, (S5_SLAB, LANES), 1) // S5_GROUP
    diag_c = row_g2 == lane_g2
    for j in range(n_slab):
        sl = slice(j * S5_SLAB, (j + 1) * S5_SLAB)
        bbr, bbi = _cmul(cr[:, sl], ci[:, sl], bre_ref[:, sl], bim_ref[:, sl])
        bmat_re[j] = jnp.where(diag_b, bbr, 0.0).astype(bf16)
        bmat_im[j] = jnp.where(diag_b, bbi, 0.0).astype(bf16)
        cmat_re[j] = jnp.where(diag_c, cre_ref[sl, :], 0.0).astype(bf16)
        cmat_im[j] = jnp.where(diag_c, -cim_ref[sl, :], 0.0).astype(bf16)

    l2r, l2i = _cmul(l1r, l1i, l1r, l1i)
    l3r, l3i = _cmul(l2r, l2i, l1r, l1i)
    l4r, l4i = _cmul(l2r, l2i, l2r, l2i)
    l5r, l5i = _cmul(l4r, l4i, l1r, l1i)
    l6r, l6i = _cmul(l4r, l4i, l2r, l2i)
    l7r, l7i = _cmul(l4r, l4i, l3r, l3i)
    l8r, l8i = _cmul(l4r, l4i, l4r, l4i)
    row = lax.broadcasted_iota(i32, (SUBLANES, S5_LANES), 0)
    for idx, (k, pr, pi) in enumerate(((1, l1r, l1i), (2, l2r, l2i), (4, l4r, l4i))):
        shift_re[idx] = jnp.where(row >= k, pr, 0.0)
        shift_im[idx] = jnp.where(row >= k, pi, 0.0)
    pr_acc = jnp.zeros((SUBLANES, S5_LANES), f32)
    pi_acc = jnp.zeros((SUBLANES, S5_LANES), f32)
    powers = ((l1r, l1i), (l2r, l2i), (l3r, l3i), (l4r, l4i), (l5r, l5i), (l6r, l6i), (l7r, l7i), (l8r, l8i))
    for i, (pr, pi) in enumerate(powers):
        pr_acc = jnp.where(row == i, pr, pr_acc)
        pi_acc = jnp.where(row == i, pi, pi_acc)
    pow_re[...] = pr_acc
    pow_im[...] = pi_acc


def _s5_body(u_ref, lam_re_ref, lam_im_ref, logdt_ref, bre_ref, bim_ref, cre_ref, cim_ref, d_ref,
             glu_a_ref, glu_b_ref, o_ref,
             bmat_re, bmat_im, cmat_re, cmat_im, shift_re, shift_im, pow_re, pow_im,
             carry_re, carry_im, st_re, st_im):
    t_idx = pl.program_id(1)

    @pl.when((pl.program_id(0) == 0) & (t_idx == 0))
    def _():
        _s5_prepare(lam_re_ref, lam_im_ref, logdt_ref, bre_ref, bim_ref, cre_ref, cim_ref,
                    bmat_re, bmat_im, cmat_re, cmat_im, shift_re, shift_im, pow_re, pow_im)

    @pl.when(t_idx == 0)
    def _():
        carry_re[...] = jnp.zeros(carry_re.shape, f32)
        carry_im[...] = jnp.zeros(carry_im.shape, f32)

    u = u_ref[0]
    ub = u.astype(bf16)
    n_slab = S5_LANES // S5_SLAB
    n_tiles = u.shape[0] // SUBLANES
    ys = []
    for j in range(n_slab):
        sl = slice(j * S5_SLAB, (j + 1) * S5_SLAB)
        uj = ub[:, j * LANES:(j + 1) * LANES]
        st_re[:, sl] = _dot(uj, bmat_re[j])
        st_im[:, sl] = _dot(uj, bmat_im[j])

        coef = [(shift_re[i, :, sl], shift_im[i, :, sl]) for i in range(3)]
        pwr, pwi = pow_re[:, sl], pow_im[:, sl]
        c_re, c_im = carry_re[:, sl], carry_im[:, sl]
        for n in range(n_tiles):
            rows = slice(n * SUBLANES, (n + 1) * SUBLANES)
            xr, xi = st_re[rows, sl], st_im[rows, sl]
            for i, k in enumerate((1, 2, 4)):
                dr, di = _cmul(coef[i][0], coef[i][1], pltpu.roll(xr, k, 0), pltpu.roll(xi, k, 0))
                xr, xi = xr + dr, xi + di
            dr, di = _cmul(pwr, pwi, c_re, c_im)
            xr, xi = xr + dr, xi + di
            st_re[rows, sl] = xr
            st_im[rows, sl] = xi
            c_re, c_im = xr[SUBLANES - 1:SUBLANES, :], xi[SUBLANES - 1:SUBLANES, :]
        carry_re[:, sl] = c_re
        carry_im[:, sl] = c_im
        ys.append(_dot(st_re[:, sl].astype(bf16), cmat_re[j]) + _dot(st_im[:, sl].astype(bf16), cmat_im[j]))
    y = jnp.concatenate(ys, axis=1) + d_ref[...] * u
    g = jax.nn.gelu(y).astype(bf16)
    o_ref[0] = (_dot(g, glu_a_ref[...]) * jax.nn.sigmoid(_dot(g, glu_b_ref[...]))).astype(bf16)


def _s5(su, layer, lam_re, lam_im, logdt, b_re, b_im, c_re, c_im, d_skip, glu_a, glu_b):
    bsz, seq, _ = su.shape
    tt = min(S5_TIME_TILE, seq)
    n_slab = S5_LANES // S5_SLAB
    tile = pl.BlockSpec((1, tt, S5_WIDTH), lambda b, t: (b, t, 0))
    return pl.pallas_call(
        _s5_body,
        grid=(bsz, seq // tt),
        in_specs=[tile, _resident((1, S5_LANES)), _resident((1, S5_LANES)), _resident((1, S5_LANES)),
                  _resident((LANES, S5_LANES)), _resident((LANES, S5_LANES)),
                  _resident((S5_LANES, LANES)), _resident((S5_LANES, LANES)), _resident((1, S5_WIDTH)),
                  _resident((S5_WIDTH, S5_WIDTH), layer), _resident((S5_WIDTH, S5_WIDTH), layer)],
        out_specs=tile,
        out_shape=jax.ShapeDtypeStruct((bsz, seq, S5_WIDTH), bf16),
        scratch_shapes=[pltpu.VMEM((n_slab, LANES, S5_SLAB), bf16), pltpu.VMEM((n_slab, LANES, S5_SLAB), bf16),
                        pltpu.VMEM((n_slab, S5_SLAB, LANES), bf16), pltpu.VMEM((n_slab, S5_SLAB, LANES), bf16),
                        pltpu.VMEM((3, SUBLANES, S5_LANES), f32), pltpu.VMEM((3, SUBLANES, S5_LANES), f32),
                        pltpu.VMEM((SUBLANES, S5_LANES), f32), pltpu.VMEM((SUBLANES, S5_LANES), f32),
                        pltpu.VMEM((1, S5_LANES), f32), pltpu.VMEM((1, S5_LANES), f32),
                        pltpu.VMEM((tt, S5_LANES), f32), pltpu.VMEM((tt, S5_LANES), f32)],
        compiler_params=_params("arbitrary", "arbitrary"),
        name="s5",
    )(su, lam_re, lam_im, logdt, b_re, b_im, c_re, c_im, d_skip, glu_a, glu_b)


def _dilated_body(cur_ref, prev_ref, o_ref, lse_ref):
    n = pl.program_id(2)
    rows = cur_ref.shape[1]
    qi = lax.broadcasted_iota(i32, (DIL_SPAN, 2 * DIL_SPAN), 0)
    kj = lax.broadcasted_iota(i32, (DIL_SPAN, 2 * DIL_SPAN), 1)
    band = (kj >= qi) & (kj <= qi + DIL_SPAN)
    band_first = band & ((kj >= DIL_SPAN) | (n > 0))
    low_half = lax.broadcasted_iota(i32, (DIL_SPAN, LANES), 1) < HEAD_DIM
    zero_b = jnp.zeros((), bf16)
    for blk in range(rows // DIL_SPAN):
        r0 = blk * DIL_SPAN
        q = cur_ref[0, r0:r0 + DIL_SPAN, 0:C_OUT]
        if blk == 0:
            k = jnp.concatenate([prev_ref[0, :, C_OUT:2 * C_OUT], cur_ref[0, 0:DIL_SPAN, C_OUT:2 * C_OUT]], axis=0)
            v = jnp.concatenate([prev_ref[0, :, 2 * C_OUT:], cur_ref[0, 0:DIL_SPAN, 2 * C_OUT:]], axis=0)
        else:
            k = cur_ref[0, r0 - DIL_SPAN:r0 + DIL_SPAN, C_OUT:2 * C_OUT]
            v = cur_ref[0, r0 - DIL_SPAN:r0 + DIL_SPAN, 2 * C_OUT:]
        mask = band_first if blk == 0 else band
        for j in range(C_OUT // LANES):
            ls = slice(j * LANES, (j + 1) * LANES)
            outs, lses = [], []
            for half in (low_half, ~low_half):
                s = _dot_nt(jnp.where(half, q[:, ls], zero_b), k[:, ls]) * (HEAD_DIM ** -0.5)
                s = jnp.where(mask, s, NEG_INF)
                m = jnp.max(s, axis=1, keepdims=True)
                p = jnp.exp(s - m)
                den = jnp.sum(p, axis=1, keepdims=True)
                outs.append(_dot(p.astype(bf16), v[:, ls]) / den)
                lses.append(jnp.broadcast_to(m + jnp.log(den), (DIL_SPAN, LANES)))
            o_ref[0, r0:r0 + DIL_SPAN, ls] = jnp.where(low_half, outs[0], outs[1]).astype(bf16)
            lse_ref[0, r0:r0 + DIL_SPAN, ls] = jnp.where(low_half, lses[0], lses[1])


def _dilated(qkv, dilation):
    bsz, seq, _ = qkv.shape
    sub = seq // dilation
    rows = min(DIL_ROWS, sub)
    blocks_per_step = rows // DIL_SPAN
    view = qkv.reshape(bsz, sub, dilation * C_WIDTH)
    cur = pl.BlockSpec((1, rows, C_WIDTH), lambda b, r, n: (b, n, r))
    prev = pl.BlockSpec((1, DIL_SPAN, C_WIDTH), lambda b, r, n: (b, jnp.maximum(n * blocks_per_step - 1, 0), r))
    out = pl.BlockSpec((1, rows, C_OUT), lambda b, r, n: (b, n, r))
    o, lse = pl.pallas_call(
        _dilated_body,
        grid=(bsz, dilation, sub // rows),
        in_specs=[cur, prev],
        out_specs=[out, out],
        out_shape=[jax.ShapeDtypeStruct((bsz, sub, dilation * C_OUT), bf16),
                   jax.ShapeDtypeStruct((bsz, sub, dilation * C_OUT), f32)],
        compiler_params=_params("parallel", "parallel", "arbitrary"),
        name=f"dilated_{dilation}",
    )(view, view)
    return o.reshape(bsz * seq, C_OUT), lse.reshape(bsz * seq, C_OUT)


def _merge_body(x_ref, ya_ref, yb_ref, o0_ref, o1_ref, o2_ref, l0_ref, l1_ref, l2_ref,
                g_ref, wgate_ref, wa_ref, wb_ref, wc_ref, wout_ref, out_ref):
    x = x_ref[...]
    h = _rmsnorm_rows(x, g_ref[...]).astype(bf16)
    l0, l1, l2 = l0_ref[...], l1_ref[...], l2_ref[...]
    mx = jnp.maximum(jnp.maximum(l0, l1), l2)
    e0, e1, e2 = jnp.exp(l0 - mx), jnp.exp(l1 - mx), jnp.exp(l2 - mx)
    yc = (e0 * o0_ref[...].astype(f32) + e1 * o1_ref[...].astype(f32) + e2 * o2_ref[...].astype(f32))
    yc = (yc / (e0 + e1 + e2)).astype(bf16)

    def gate(i):
        return jax.nn.sigmoid(_dot(h, wgate_ref[:, i * D_MODEL:(i + 1) * D_MODEL]))

    merged = gate(0) * _dot(ya_ref[...], wa_ref[...])
    merged = merged + gate(1) * _dot(yb_ref[...], wb_ref[...])
    merged = merged + gate(2) * _dot(yc, wc_ref[...])
    out_ref[...] = x + _dot(merged.astype(bf16), wout_ref[...])


def _merge(x2, layer, ya, yb, os_, lses, gain, w_gates, w_a, w_b, w_c, w_out):
    n = x2.shape[0]
    tm = min(TOKEN_TILE, n)

    def tile(width):
        return pl.BlockSpec((tm, width), lambda i: (i, 0))

    return pl.pallas_call(
        _merge_body,
        grid=(n // tm,),
        in_specs=[tile(D_MODEL), tile(A_WIDTH), tile(S5_WIDTH)] + [tile(C_OUT)] * 6 +
                 [_resident((1, D_MODEL), layer), _resident((D_MODEL, 3 * D_MODEL), layer),
                  _resident((A_WIDTH, D_MODEL), layer), _resident((S5_WIDTH, D_MODEL), layer),
                  _resident((C_OUT, D_MODEL), layer), _resident((D_MODEL, D_MODEL), layer)],
        out_specs=tile(D_MODEL),
        out_shape=jax.ShapeDtypeStruct((n, D_MODEL), f32),
        compiler_params=_params("parallel"),
        name="merge",
    )(x2, ya, yb, *os_, *lses, gain, w_gates, w_a, w_b, w_c, w_out)


def _pack_w_in(w_in):
    offs = [0]
    for s in IN_SIZES:
        offs.append(offs[-1] + s)
    col = lambda i: w_in[:, :, offs[i]:offs[i + 1]]
    aq, ak, av, iq, ik, iw, su, cq, ck, cv, gates = [col(i) for i in range(len(IN_SIZES))]
    parts = [aq, ak, ak, iq, ik, ik, su]
    for g in range(len(DIL_PAIRS)):
        gs = slice(g * C_OUT, (g + 1) * C_OUT)
        parts += [cq[:, :, gs], ck[:, :, gs], cv[:, :, gs]]
    iw_pad = jnp.pad(iw, ((0, 0), (0, 0), (0, SUBLANES - IDX_HEADS)))
    w_t = jnp.transpose(jnp.concatenate([av, av, iw_pad], axis=2), (0, 2, 1))
    return jnp.concatenate(parts, axis=2).astype(bf16), w_t.astype(bf16), gates.astype(bf16)


def _segment_mean_matrix(width):
    seg = jnp.arange(width) // HEAD_DIM
    return jnp.where(seg[:, None] == seg[None, :], 1.0 / HEAD_DIM, 0.0).astype(bf16)


def _tile_gain(g, reps):
    return jnp.tile(g[:, None, :], (1, 1, reps))


def kernel(x, ffn1_norm, ffn1_gate, ffn1_up, ffn1_down, mix_norm, w_in, a_q_norm, a_k_norm, s5_lam_re, s5_lam_im, s5_log_dt, s5_b_re, s5_b_im, s5_c_re, s5_c_im, s5_d, s5_glu_a, s5_glu_b, c_q_norm, c_k_norm, w_branch_a, w_branch_b, w_branch_c, w_out, ffn2_norm, ffn2_gate, ffn2_up, ffn2_down):
    bsz, seq, _ = x.shape
    depth = w_in.shape[0]
    n = bsz * seq
    x2 = x.reshape(n, D_MODEL)
    assert TOKEN_TILE == KEY_CHUNK and seq % KEY_CHUNK == 0
    seg_mean = _segment_mean_matrix(A_WIDTH)
    grp_per_slab = S5_SLAB // S5_STATE
    cast = lambda w: w.astype(bf16)
    row = lambda g: g[:, None, :]
    ffn1 = (row(ffn1_norm), cast(ffn1_gate), cast(ffn1_up), cast(ffn1_down))
    ffn2 = (row(ffn2_norm), cast(ffn2_gate), cast(ffn2_up), cast(ffn2_down))
    w_packed, w_t, w_gates = _pack_w_in(w_in)
    head_gains = (_tile_gain(a_q_norm, A_HEADS), _tile_gain(a_k_norm, 2),
                  _tile_gain(c_q_norm, C_HEADS_PER_GROUP), _tile_gain(c_k_norm, C_HEADS_PER_GROUP))
    glu = (cast(s5_glu_a), cast(s5_glu_b))
    merge_w = (row(mix_norm), w_gates, cast(w_branch_a), cast(w_branch_b), cast(w_branch_c), cast(w_out))
    r3 = lambda a: a.reshape(bsz, seq, a.shape[-1])
    b_t = lambda b: jnp.tile(jnp.transpose(b, (2, 0, 1)).reshape(S5_GROUP, S5_LANES), (grp_per_slab, 1))
    c_t = lambda c: jnp.tile(jnp.transpose(c, (0, 2, 1)).reshape(S5_LANES, S5_GROUP), (1, grp_per_slab))
    for l in range(depth):
        x2 = _ffn(x2, l, *ffn1)
        aq, akk, vvt, iq, ikk, iwt, su, c0, c1, c2 = _in_proj(x2, l, row(mix_norm), w_packed, w_t, seg_mean,
                                                               *head_gains)
        vvt = vvt.reshape(bsz, seq // KEY_CHUNK, LANES, KEY_CHUNK)
        ya = _dsa(r3(aq), r3(iq), iwt, r3(akk), vvt, r3(ikk)).reshape(n, A_WIDTH)
        yb = _s5(r3(su), l, s5_lam_re[l].reshape(1, -1), s5_lam_im[l].reshape(1, -1),
                 jnp.repeat(s5_log_dt[l], S5_STATE).reshape(1, -1),
                 b_t(s5_b_re[l]), b_t(s5_b_im[l]), c_t(s5_c_re[l]), c_t(s5_c_im[l]),
                 s5_d[l].reshape(1, -1), *glu).reshape(n, S5_WIDTH)
        os_, lses = [], []
        for cg, (_, dilation) in zip((c0, c1, c2), DIL_PAIRS):
            o, lse = _dilated(r3(cg), dilation)
            os_.append(o)
            lses.append(lse)
        x2 = _merge(x2, l, ya, yb, os_, lses, *merge_w)
        x2 = _ffn(x2, l, *ffn2)
    return x2.reshape(bsz, seq, D_MODEL)
```

```python
import functools
import math

import jax
import jax.numpy as jnp
from jax import lax
from jax.experimental import pallas as pl
from jax.experimental.pallas import tpu as pltpu

f32 = jnp.float32
bf16 = jnp.bfloat16
i32 = jnp.int32

D_MODEL = 1024
D_FF = 2816
HEAD_DIM = 64
RMS_EPS = 1e-6
A_HEADS = 8
A_WIDTH = A_HEADS * HEAD_DIM
IDX_HEADS = 4
IDX_DIM = 64
TOPK_MAX = 256
S5_WIDTH = 512
S5_GROUP = 16
S5_GROUPS = 32
S5_STATE = 64
S5_LANES = S5_GROUPS * S5_STATE
DIL_PAIRS = ((128, 1), (512, 4), (2048, 16))
C_HEADS_PER_GROUP = 4
C_OUT = C_HEADS_PER_GROUP * HEAD_DIM
C_WIDTH = 3 * C_OUT
IN_SIZES = (A_WIDTH, HEAD_DIM, HEAD_DIM, IDX_HEADS * IDX_DIM, IDX_DIM, IDX_HEADS,
            S5_WIDTH, C_WIDTH, C_WIDTH, C_WIDTH, 3 * D_MODEL)

LANES = 128
SUBLANES = 8
VMEM_LIMIT_BYTES = 56 * 1024 * 1024

TOKEN_TILE = 512
FF_CHUNK = 256
Q_TILE = 128
KEY_CHUNK = 512
COUNT_ROWS = 64
KEY_BITS = 32
PLANE_ROWS = KEY_BITS * 8
RANK_ROWS = 256
LOOP_UNITS = (4, 2, 1)
S5_TIME_TILE = 256
S5_SLAB = 512
S5_SCAN_LANES = 512
DIL_SPAN = 128
DIL_ROWS = 512

NEG_INF = float("-inf")
LOG2_E = math.log2(math.e)
INT_MIN = -2 ** 31
NEG_INF_KEY = INT_MIN + 0x7FFFFF

_NT = (((1,), (1,)), ((), ()))


def _params(*sem):
    return pltpu.CompilerParams(dimension_semantics=sem, vmem_limit_bytes=VMEM_LIMIT_BYTES)


def _resident(shape, layer=None):
    nd = len(shape)
    if layer is None:
        return pl.BlockSpec(shape, lambda *_: (0,) * nd, pipeline_mode=pl.Buffered(1))
    return pl.BlockSpec((None,) + tuple(shape), lambda *_: (layer,) + (0,) * nd, pipeline_mode=pl.Buffered(1))


def _dot(a, b):
    return jnp.dot(a, b, preferred_element_type=f32)


def _dot_nt(a, b):
    return lax.dot_general(a, b, _NT, preferred_element_type=f32)


def _rmsnorm_rows(x, gain):
    return x * lax.rsqrt(jnp.mean(x * x, axis=-1, keepdims=True) + RMS_EPS) * gain


def _head_rmsnorm(z, gain_tiled, seg_mean):
    ms = _dot((z * z).astype(bf16), seg_mean)
    return z * lax.rsqrt(ms + RMS_EPS) * gain_tiled


def _ffn_body(x_ref, g_ref, wg_ref, wu_ref, wd_ref, o_ref, acc_ref):
    x = x_ref[...]
    h = _rmsnorm_rows(x, g_ref[...]).astype(bf16)
    for c in range(D_FF // FF_CHUNK):
        sl = slice(c * FF_CHUNK, (c + 1) * FF_CHUNK)
        gate = _dot(h, wg_ref[:, sl])
        up = _dot(h, wu_ref[:, sl])
        act = (gate * jax.nn.sigmoid(gate) * up).astype(bf16)
        contrib = _dot(act, wd_ref[sl, :])
        if c == 0:
            acc_ref[...] = contrib
        else:
            acc_ref[...] += contrib
    o_ref[...] = x + 0.5 * acc_ref[...]


def _ffn(x2, layer, gain, w_gate, w_up, w_down):
    n = x2.shape[0]
    tm = min(TOKEN_TILE, n)
    tile = pl.BlockSpec((tm, D_MODEL), lambda i: (i, 0))
    return pl.pallas_call(
        _ffn_body,
        grid=(n // tm,),
        in_specs=[tile, _resident((1, D_MODEL), layer), _resident((D_MODEL, D_FF), layer),
                  _resident((D_MODEL, D_FF), layer), _resident((D_FF, D_MODEL), layer)],
        out_specs=tile,
        out_shape=jax.ShapeDtypeStruct((n, D_MODEL), f32),
        scratch_shapes=[pltpu.VMEM((tm, D_MODEL), f32)],
        compiler_params=_params("parallel"),
        name="ffn",
    )(x2, gain, w_gate, w_up, w_down)


_P_AQ = 0
_P_AKK = _P_AQ + A_WIDTH
_P_IQ = _P_AKK + LANES
_P_IKK = _P_IQ + IDX_HEADS * IDX_DIM
_P_SU = _P_IKK + LANES
_P_C = _P_SU + S5_WIDTH
_P_TOTAL = _P_C + 3 * C_WIDTH
_T_ROWS = LANES + SUBLANES


def _interleave_store(dst_ref, value, dilation, perm_ref):
    rows, width = value.shape
    for j in range(width // LANES):
        perm_ref[j] = value[:, j * LANES:(j + 1) * LANES]
    for r in range(dilation):
        for j in range(width // LANES):
            piece = perm_ref[j, pl.ds(r, rows // dilation, stride=dilation), :]
            dst_ref[:, r * width + j * LANES:r * width + (j + 1) * LANES] = piece.astype(dst_ref.dtype)


def _deinterleave_load(src_ref, dilation, perm_ref):
    width = src_ref.shape[1] // dilation
    rows = src_ref.shape[0] * dilation
    for r in range(dilation):
        for j in range(width // LANES):
            piece = src_ref[:, r * width + j * LANES:r * width + (j + 1) * LANES]
            perm_ref[j, pl.ds(r, rows // dilation, stride=dilation), :] = piece.astype(f32)
    return jnp.concatenate([perm_ref[j] for j in range(width // LANES)], axis=1)


def _in_proj_body(x_ref, g_ref, w_ref, wt_ref, seg_ref, aqg_ref, akg_ref, cqg_ref, ckg_ref,
                  aq_ref, akk_ref, vvt_ref, iq_ref, ikk_ref, iwt_ref, su_ref, c0_ref, c1_ref, c2_ref, perm_ref):
    h = _rmsnorm_rows(x_ref[...], g_ref[...]).astype(bf16)
    full = _dot(h, w_ref[...])

    def proj(start, width):
        return full[:, start:start + width]

    seg = seg_ref[...]
    aq = _head_rmsnorm(proj(_P_AQ, A_WIDTH), aqg_ref[...], seg) * (HEAD_DIM ** -0.5)
    aq_ref[...] = aq.astype(bf16)
    akk_ref[...] = _head_rmsnorm(proj(_P_AKK, LANES), akg_ref[...], seg[:LANES, :LANES]).astype(bf16)
    iq_ref[...] = proj(_P_IQ, IDX_HEADS * IDX_DIM).astype(bf16)
    ikk_ref[...] = proj(_P_IKK, LANES).astype(bf16)
    su_ref[...] = proj(_P_SU, S5_WIDTH)
    vvt_ref[0] = _dot_nt(wt_ref[:LANES, :], h).astype(bf16)
    iwt_ref[0] = _dot_nt(wt_ref[LANES:, :], h)
    for c_ref, (_, dilation), g in zip((c0_ref, c1_ref, c2_ref), DIL_PAIRS, range(len(DIL_PAIRS))):
        base = _P_C + g * C_WIDTH
        cq = _head_rmsnorm(proj(base, C_OUT), cqg_ref[...], seg[:C_OUT, :C_OUT])
        ck = _head_rmsnorm(proj(base + C_OUT, C_OUT), ckg_ref[...], seg[:C_OUT, :C_OUT])
        qkv = (cq, ck, proj(base + 2 * C_OUT, C_OUT))
        if dilation == 1:
            for i, part in enumerate(qkv):
                c_ref[:, i * C_OUT:(i + 1) * C_OUT] = part.astype(bf16)
        else:
            _interleave_store(c_ref, jnp.concatenate(qkv, axis=1), dilation, perm_ref)


def _in_proj(x2, layer, gain, w_packed, w_t, seg_mean, aq_gain, ak_gain, cq_gain, ck_gain):
    n = x2.shape[0]
    tm = min(TOKEN_TILE, n)
    rows = lambda w: pl.BlockSpec((tm, w), lambda i: (i, 0))
    cols = lambda r: pl.BlockSpec((1, r, tm), lambda i: (i, 0, 0))
    outs = [(rows(A_WIDTH), (n, A_WIDTH), bf16), (rows(LANES), (n, LANES), bf16),
            (cols(LANES), (n // tm, LANES, tm), bf16), (rows(IDX_HEADS * IDX_DIM), (n, IDX_HEADS * IDX_DIM), bf16),
            (rows(LANES), (n, LANES), bf16), (cols(SUBLANES), (n // tm, SUBLANES, tm), f32),
            (rows(S5_WIDTH), (n, S5_WIDTH), f32)]
    for _, d in DIL_PAIRS:
        outs.append((pl.BlockSpec((tm // d, d * C_WIDTH), lambda i: (i, 0)), (n // d, d * C_WIDTH), bf16))
    return pl.pallas_call(
        _in_proj_body,
        grid=(n // tm,),
        in_specs=[rows(D_MODEL), _resident((1, D_MODEL), layer),
                  _resident((D_MODEL, _P_TOTAL), layer), _resident((_T_ROWS, D_MODEL), layer),
                  _resident((A_WIDTH, A_WIDTH)),
                  _resident((1, A_WIDTH), layer), _resident((1, LANES), layer),
                  _resident((1, C_OUT), layer), _resident((1, C_OUT), layer)],
        out_specs=[o[0] for o in outs],
        out_shape=[jax.ShapeDtypeStruct(o[1], o[2]) for o in outs],
        scratch_shapes=[pltpu.VMEM((C_WIDTH // LANES, tm, LANES), f32)],
        compiler_params=_params("parallel"),
        name="in_proj",
    )(x2, gain, w_packed, w_t, seg_mean, aq_gain, ak_gain, cq_gain, ck_gain)


def _float_order_key(x):
    bits = pltpu.bitcast(x, i32)
    return jnp.where(bits < 0, bits ^ 0x7FFFFFFF, bits)


def _fold_rows(x, op, keep=SUBLANES):
    return op(x.reshape(x.shape[0] // keep, keep, x.shape[1]), axis=0)


def _bit_planes(words):
    a = [words[t * SUBLANES:(t + 1) * SUBLANES] for t in range(KEY_BITS)]
    for j, m in ((16, 0x0000FFFF), (8, 0x00FF00FF), (4, 0x0F0F0F0F), (2, 0x33333333), (1, 0x55555555)):
        shift = jnp.full(a[0].shape, j, i32)
        for k in range(KEY_BITS):
            if k & j == 0:
                t = (a[k] ^ lax.shift_right_logical(a[k + j], shift)) & m
                a[k] = a[k] ^ t
                a[k + j] = a[k + j] ^ lax.shift_left(t, shift)
    return a


def _dsa_body(topk, aq_ref, iq_ref, iwt_ref, kk_ref, vvt_ref, ikk_ref, rank_ref, o_ref,
              key_ref, plane_ref, cand_ref, s_ref, m_ref, l_ref, acc_ref):
    qb = pl.program_id(1)
    n_chunks = lax.shift_right_logical(qb * Q_TILE, int(math.log2(KEY_CHUNK))) + 1
    q_pos = qb * Q_TILE + lax.broadcasted_iota(i32, (1, Q_TILE), 1)
    key_iota = lax.broadcasted_iota(i32, (KEY_CHUNK, Q_TILE), 0)
    low_half = lax.broadcasted_iota(i32, (Q_TILE, LANES), 1) < HEAD_DIM
    zero_b = jnp.zeros((), bf16)

    def stack_heads(x, n_heads):
        blocks = []
        for h in range(n_heads):
            pair = x[:, (h // 2) * LANES:(h // 2 + 1) * LANES]
            blocks.append(jnp.where(low_half if h % 2 == 0 else ~low_half, pair, zero_b))
        return jnp.concatenate(blocks, axis=0)

    @pl.when((pl.program_id(0) == 0) & (qb == 0))
    def _():
        plane_ref[...] = jnp.zeros(plane_ref.shape, i32)

    iq_all = stack_heads(iq_ref[0], IDX_HEADS)
    w_t = iwt_ref[0] * ((IDX_DIM ** -0.5) * (IDX_HEADS ** -0.5))

    def key_rows(c, part, rows):
        return pl.ds(pl.multiple_of(c * KEY_CHUNK + part * rows, rows), rows)

    def chunk_loop(body, carry, end=n_chunks):
        start = 0
        for unit in LOOP_UNITS:
            n_iter = lax.shift_right_logical(end - start, int(math.log2(unit)))

            def multi(i, cr, unit=unit, start=start):
                for j in range(unit):
                    cr = body(start + i * unit + j, cr)
                return cr

            carry = lax.fori_loop(0, n_iter, multi, carry)
            start = start + n_iter * unit
        return carry

    def score_chunk(c, _, diagonal=False):
        logits = _dot_nt(ikk_ref[0, key_rows(c, 0, KEY_CHUNK), :], iq_all)
        score = None
        for h in range(IDX_HEADS):
            term = jnp.maximum(logits[:, h * Q_TILE:(h + 1) * Q_TILE], 0.0) * w_t[h:h + 1, :]
            score = term if score is None else score + term
        score = jnp.where(score == 0.0, 0.0, score)
        if diagonal:
            score = jnp.where(c * KEY_CHUNK + key_iota <= q_pos, score, NEG_INF)
        key = _float_order_key(score)
        key_ref[c] = key
        for part in range(KEY_CHUNK // PLANE_ROWS):
            planes = _bit_planes(key[part * PLANE_ROWS:(part + 1) * PLANE_ROWS] ^ INT_MIN)
            for i in range(KEY_BITS):
                plane_ref[i, c * (KEY_CHUNK // PLANE_ROWS) + part] = planes[i]
        return 0

    chunk_loop(score_chunk, 0, end=n_chunks - 1)
    score_chunk(n_chunks - 1, 0, diagonal=True)

    n_planes = n_chunks * (KEY_CHUNK // PLANE_ROWS)
    block_id = lax.broadcasted_iota(i32, cand_ref.shape, 0)
    cand_ref[...] = jnp.where(block_id < n_planes, -1, 0)

    def bit_step(i, carry):
        prefix, n_above = carry
        cand = cand_ref[...]
        ones = cand & plane_ref[i]
        cnt = jnp.sum(jnp.sum(lax.population_count(ones), axis=0).astype(f32), axis=0, keepdims=True)
        accept = n_above + cnt >= topk
        cand_ref[...] = jnp.where(accept, ones, cand ^ ones)
        prefix = jnp.where(accept, prefix | lax.shift_left(jnp.int32(1), KEY_BITS - 1 - i), prefix)
        return prefix, jnp.where(accept, n_above, n_above + cnt)

    prefix, n_gt = lax.fori_loop(0, KEY_BITS, bit_step,
                                 (jnp.zeros((1, Q_TILE), i32), jnp.zeros((1, Q_TILE), f32)))
    thr = prefix ^ INT_MIN
    need = jnp.where(thr <= NEG_INF_KEY, 0.0, topk - n_gt)
    rank_mat = rank_ref[...]

    q_all = stack_heads(aq_ref[0], A_HEADS)

    m_ref[...] = jnp.full(m_ref.shape, NEG_INF, f32)

    def logits_chunk(c, seen):
        for part in range(KEY_CHUNK // RANK_ROWS):
            rows = slice(part * RANK_ROWS, (part + 1) * RANK_ROWS)
            k = key_ref[c, rows, :]
            eq = k == thr
            tie = jnp.where(eq, 1.0, 0.0)
            rank = _dot(rank_mat, tie.astype(bf16)) + seen
            seen = seen + jnp.sum(_fold_rows(tie, jnp.sum, COUNT_ROWS), axis=0, keepdims=True)
            order = jnp.where(eq, rank, jnp.where(k > thr, -1.0, 2.0 ** 30))
            bias = jnp.where(order < need, 0.0, NEG_INF)
            s = _dot_nt(kk_ref[0, key_rows(c, part, RANK_ROWS), :], q_all)
            for h in range(A_HEADS):
                cols = slice(h * Q_TILE, (h + 1) * Q_TILE)
                sh = (s[:, cols] + bias) * LOG2_E
                s_ref[c, rows, cols] = sh
                m_ref[:, cols] = jnp.maximum(m_ref[:, cols], _fold_rows(sh, jnp.max))
        return seen

    chunk_loop(logits_chunk, jnp.zeros((1, Q_TILE), f32))
    m = jnp.max(m_ref[...], axis=0, keepdims=True)

    l_ref[...] = jnp.zeros(l_ref.shape, f32)
    acc_ref[...] = jnp.zeros(acc_ref.shape, f32)

    def softmax_chunk(c, _):
        for part in range(KEY_CHUNK // RANK_ROWS):
            rows = slice(part * RANK_ROWS, (part + 1) * RANK_ROWS)
            p = jnp.exp2(s_ref[c, rows, :] - m)
            l_ref[...] += _fold_rows(p, jnp.sum)
            acc_ref[...] += _dot(vvt_ref[0, c, :, rows], p.astype(bf16))
        return 0

    chunk_loop(softmax_chunk, 0)
    out_t = acc_ref[...] / jnp.sum(l_ref[...], axis=0, keepdims=True)
    top_rows = lax.broadcasted_iota(i32, (LANES, Q_TILE), 0) < HEAD_DIM
    for j in range(A_HEADS // 2):
        even = out_t[:, (2 * j) * Q_TILE:(2 * j + 1) * Q_TILE]
        odd = out_t[:, (2 * j + 1) * Q_TILE:(2 * j + 2) * Q_TILE]
        o_ref[0, :, j * LANES:(j + 1) * LANES] = jnp.where(top_rows, even, odd).T.astype(bf16)


def _tie_rank_matrix():
    i = jnp.arange(RANK_ROWS)
    return (i[None, :] < i[:, None]).astype(bf16)


def _dsa(aq, iq, iwt, akk, vvt, ikk):
    bsz, seq, _ = aq.shape
    topk = min(TOPK_MAX, seq // 4)
    n_kc = seq // KEY_CHUNK
    q_per_chunk = KEY_CHUNK // Q_TILE

    def q_spec(width):
        return pl.BlockSpec((1, Q_TILE, width), lambda b, q: (b, q, 0))

    kv_spec = pl.BlockSpec((1, seq, LANES), lambda b, q: (b, 0, 0))
    return pl.pallas_call(
        functools.partial(_dsa_body, topk),
        grid=(bsz, seq // Q_TILE),
        in_specs=[q_spec(A_WIDTH), q_spec(IDX_HEADS * IDX_DIM),
                  pl.BlockSpec((1, SUBLANES, Q_TILE), lambda b, q: (b * n_kc + q // q_per_chunk, 0, q % q_per_chunk)),
                  kv_spec, pl.BlockSpec((1, n_kc, LANES, KEY_CHUNK), lambda b, q: (b, 0, 0, 0)), kv_spec,
                  _resident((RANK_ROWS, RANK_ROWS))],
        out_specs=q_spec(A_WIDTH),
        out_shape=jax.ShapeDtypeStruct((bsz, seq, A_WIDTH), bf16),
        scratch_shapes=[pltpu.VMEM((n_kc, KEY_CHUNK, Q_TILE), i32),
                        pltpu.VMEM((KEY_BITS, seq // PLANE_ROWS, SUBLANES, Q_TILE), i32),
                        pltpu.VMEM((seq // PLANE_ROWS, SUBLANES, Q_TILE), i32),
                        pltpu.VMEM((n_kc, KEY_CHUNK, A_HEADS * Q_TILE), f32),
                        pltpu.VMEM((SUBLANES, A_HEADS * Q_TILE), f32),
                        pltpu.VMEM((SUBLANES, A_HEADS * Q_TILE), f32),
                        pltpu.VMEM((LANES, A_HEADS * Q_TILE), f32)],
        compiler_params=_params("arbitrary", "arbitrary"),
        name="dsa",
    )(aq, iq, iwt, akk, vvt, ikk, _tie_rank_matrix())


def _cmul(ar, ai, br, bi):
    return ar * br - ai * bi, ar * bi + ai * br


def _s5_prepare(lam_re_ref, lam_im_ref, logdt_ref, bre_ref, bim_ref, cre_ref, cim_ref,
                bmat_re, bmat_im, cmat_re, cmat_im, shift_re, shift_im, pow_re, pow_im):
    lr = jnp.minimum(lam_re_ref[...], -1e-4)
    li = lam_im_ref[...]
    dt = jnp.exp(logdt_ref[...])
    mag = jnp.exp(lr * dt)
    l1r, l1i = mag * jnp.cos(li * dt), mag * jnp.sin(li * dt)
    den = lr * lr + li * li
    cr = ((l1r - 1.0) * lr + l1i * li) / den
    ci = (l1i * lr - (l1r - 1.0) * li) / den

    n_slab = S5_LANES // S5_SLAB
    row_g = lax.broadcasted_iota(i32, (LANES, S5_SLAB), 0) // S5_GROUP
    lane_g = lax.broadcasted_iota(i32, (LANES, S5_SLAB), 1) // S5_STATE
    diag_b = row_g == lane_g
    row_g2 = lax.broadcasted_iota(i32, (S5_SLAB, LANES), 0) // S5_STATE
    lane_g2 = lax.broadcasted_iota(i32, (S5_SLAB, LANES), 1) // S5_GROUP
    diag_c = row_g2 == lane_g2
    for j in range(n_slab):
        sl = slice(j * S5_SLAB, (j + 1) * S5_SLAB)
        bbr, bbi = _cmul(cr[:, sl], ci[:, sl], bre_ref[:, sl], bim_ref[:, sl])
        bmat_re[j] = jnp.where(diag_b, bbr, 0.0).astype(bf16)
        bmat_im[j] = jnp.where(diag_b, bbi, 0.0).astype(bf16)
        cmat_re[j] = jnp.where(diag_c, cre_ref[sl, :], 0.0).astype(bf16)
        cmat_im[j] = jnp.where(diag_c, -cim_ref[sl, :], 0.0).astype(bf16)

    l2r, l2i = _cmul(l1r, l1i, l1r, l1i)
    l3r, l3i = _cmul(l2r, l2i, l1r, l1i)
    l4r, l4i = _cmul(l2r, l2i, l2r, l2i)
    l5r, l5i = _cmul(l4r, l4i, l1r, l1i)
    l6r, l6i = _cmul(l4r, l4i, l2r, l2i)
    l7r, l7i = _cmul(l4r, l4i, l3r, l3i)
    l8r, l8i = _cmul(l4r, l4i, l4r, l4i)
    row = lax.broadcasted_iota(i32, (SUBLANES, S5_LANES), 0)
    for idx, (k, pr, pi) in enumerate(((1, l1r, l1i), (2, l2r, l2i), (4, l4r, l4i))):
        shift_re[idx] = jnp.where(row >= k, pr, 0.0)
        shift_im[idx] = jnp.where(row >= k, pi, 0.0)
    pr_acc = jnp.zeros((SUBLANES, S5_LANES), f32)
    pi_acc = jnp.zeros((SUBLANES, S5_LANES), f32)
    powers = ((l1r, l1i), (l2r, l2i), (l3r, l3i), (l4r, l4i), (l5r, l5i), (l6r, l6i), (l7r, l7i), (l8r, l8i))
    for i, (pr, pi) in enumerate(powers):
        pr_acc = jnp.where(row == i, pr, pr_acc)
        pi_acc = jnp.where(row == i, pi, pi_acc)
    pow_re[...] = pr_acc
    pow_im[...] = pi_acc


def _s5_body(u_ref, lam_re_ref, lam_im_ref, logdt_ref, bre_ref, bim_ref, cre_ref, cim_ref, d_ref,
             glu_a_ref, glu_b_ref, o_ref,
             bmat_re, bmat_im, cmat_re, cmat_im, shift_re, shift_im, pow_re, pow_im,
             carry_re, carry_im, st_re, st_im):
    t_idx = pl.program_id(1)

    @pl.when((pl.program_id(0) == 0) & (t_idx == 0))
    def _():
        _s5_prepare(lam_re_ref, lam_im_ref, logdt_ref, bre_ref, bim_ref, cre_ref, cim_ref,
                    bmat_re, bmat_im, cmat_re, cmat_im, shift_re, shift_im, pow_re, pow_im)

    @pl.when(t_idx == 0)
    def _():
        carry_re[...] = jnp.zeros(carry_re.shape, f32)
        carry_im[...] = jnp.zeros(carry_im.shape, f32)

    u = u_ref[0]
    ub = u.astype(bf16)
    n_slab = S5_LANES // S5_SLAB
    n_tiles = u.shape[0] // SUBLANES
    ys = []
    for j in range(n_slab):
        sl = slice(j * S5_SLAB, (j + 1) * S5_SLAB)
        uj = ub[:, j * LANES:(j + 1) * LANES]
        st_re[:, sl] = _dot(uj, bmat_re[j])
        st_im[:, sl] = _dot(uj, bmat_im[j])

        coef = [(shift_re[i, :, sl], shift_im[i, :, sl]) for i in range(3)]
        pwr, pwi = pow_re[:, sl], pow_im[:, sl]
        c_re, c_im = carry_re[:, sl], carry_im[:, sl]
        for n in range(n_tiles):
            rows = slice(n * SUBLANES, (n + 1) * SUBLANES)
            xr, xi = st_re[rows, sl], st_im[rows, sl]
            for i, k in enumerate((1, 2, 4)):
                dr, di = _cmul(coef[i][0], coef[i][1], pltpu.roll(xr, k, 0), pltpu.roll(xi, k, 0))
                xr, xi = xr + dr, xi + di
            dr, di = _cmul(pwr, pwi, c_re, c_im)
            xr, xi = xr + dr, xi + di
            st_re[rows, sl] = xr
            st_im[rows, sl] = xi
            c_re, c_im = xr[SUBLANES - 1:SUBLANES, :], xi[SUBLANES - 1:SUBLANES, :]
        carry_re[:, sl] = c_re
        carry_im[:, sl] = c_im
        ys.append(_dot(st_re[:, sl].astype(bf16), cmat_re[j]) + _dot(st_im[:, sl].astype(bf16), cmat_im[j]))
    y = jnp.concatenate(ys, axis=1) + d_ref[...] * u
    g = jax.nn.gelu(y).astype(bf16)
    o_ref[0] = (_dot(g, glu_a_ref[...]) * jax.nn.sigmoid(_dot(g, glu_b_ref[...]))).astype(bf16)


def _s5(su, layer, lam_re, lam_im, logdt, b_re, b_im, c_re, c_im, d_skip, glu_a, glu_b):
    bsz, seq, _ = su.shape
    tt = min(S5_TIME_TILE, seq)
    n_slab = S5_LANES // S5_SLAB
    tile = pl.BlockSpec((1, tt, S5_WIDTH), lambda b, t: (b, t, 0))
    return pl.pallas_call(
        _s5_body,
        grid=(bsz, seq // tt),
        in_specs=[tile, _resident((1, S5_LANES)), _resident((1, S5_LANES)), _resident((1, S5_LANES)),
                  _resident((LANES, S5_LANES)), _resident((LANES, S5_LANES)),
                  _resident((S5_LANES, LANES)), _resident((S5_LANES, LANES)), _resident((1, S5_WIDTH)),
                  _resident((S5_WIDTH, S5_WIDTH), layer), _resident((S5_WIDTH, S5_WIDTH), layer)],
        out_specs=tile,
        out_shape=jax.ShapeDtypeStruct((bsz, seq, S5_WIDTH), bf16),
        scratch_shapes=[pltpu.VMEM((n_slab, LANES, S5_SLAB), bf16), pltpu.VMEM((n_slab, LANES, S5_SLAB), bf16),
                        pltpu.VMEM((n_slab, S5_SLAB, LANES), bf16), pltpu.VMEM((n_slab, S5_SLAB, LANES), bf16),
                        pltpu.VMEM((3, SUBLANES, S5_LANES), f32), pltpu.VMEM((3, SUBLANES, S5_LANES), f32),
                        pltpu.VMEM((SUBLANES, S5_LANES), f32), pltpu.VMEM((SUBLANES, S5_LANES), f32),
                        pltpu.VMEM((1, S5_LANES), f32), pltpu.VMEM((1, S5_LANES), f32),
                        pltpu.VMEM((tt, S5_LANES), f32), pltpu.VMEM((tt, S5_LANES), f32)],
        compiler_params=_params("arbitrary", "arbitrary"),
        name="s5",
    )(su, lam_re, lam_im, logdt, b_re, b_im, c_re, c_im, d_skip, glu_a, glu_b)


def _dilated_body(cur_ref, prev_ref, o_ref, lse_ref):
    n = pl.program_id(2)
    rows = cur_ref.shape[1]
    qi = lax.broadcasted_iota(i32, (DIL_SPAN, 2 * DIL_SPAN), 0)
    kj = lax.broadcasted_iota(i32, (DIL_SPAN, 2 * DIL_SPAN), 1)
    band = (kj >= qi) & (kj <= qi + DIL_SPAN)
    band_first = band & ((kj >= DIL_SPAN) | (n > 0))
    low_half = lax.broadcasted_iota(i32, (DIL_SPAN, LANES), 1) < HEAD_DIM
    zero_b = jnp.zeros((), bf16)
    for blk in range(rows // DIL_SPAN):
        r0 = blk * DIL_SPAN
        q = cur_ref[0, r0:r0 + DIL_SPAN, 0:C_OUT]
        if blk == 0:
            k = jnp.concatenate([prev_ref[0, :, C_OUT:2 * C_OUT], cur_ref[0, 0:DIL_SPAN, C_OUT:2 * C_OUT]], axis=0)
            v = jnp.concatenate([prev_ref[0, :, 2 * C_OUT:], cur_ref[0, 0:DIL_SPAN, 2 * C_OUT:]], axis=0)
        else:
            k = cur_ref[0, r0 - DIL_SPAN:r0 + DIL_SPAN, C_OUT:2 * C_OUT]
            v = cur_ref[0, r0 - DIL_SPAN:r0 + DIL_SPAN, 2 * C_OUT:]
        mask = band_first if blk == 0 else band
        for j in range(C_OUT // LANES):
            ls = slice(j * LANES, (j + 1) * LANES)
            outs, lses = [], []
            for half in (low_half, ~low_half):
                s = _dot_nt(jnp.where(half, q[:, ls], zero_b), k[:, ls]) * (HEAD_DIM ** -0.5)
                s = jnp.where(mask, s, NEG_INF)
                m = jnp.max(s, axis=1, keepdims=True)
                p = jnp.exp(s - m)
                den = jnp.sum(p, axis=1, keepdims=True)
                outs.append(_dot(p.astype(bf16), v[:, ls]) / den)
                lses.append(jnp.broadcast_to(m + jnp.log(den), (DIL_SPAN, LANES)))
            o_ref[0, r0:r0 + DIL_SPAN, ls] = jnp.where(low_half, outs[0], outs[1]).astype(bf16)
            lse_ref[0, r0:r0 + DIL_SPAN, ls] = jnp.where(low_half, lses[0], lses[1])


def _dilated(view, dilation):
    bsz, sub, _ = view.shape
    rows = min(DIL_ROWS, sub)
    blocks_per_step = rows // DIL_SPAN
    cur = pl.BlockSpec((1, rows, C_WIDTH), lambda b, r, n: (b, n, r))
    prev = pl.BlockSpec((1, DIL_SPAN, C_WIDTH), lambda b, r, n: (b, jnp.maximum(n * blocks_per_step - 1, 0), r))
    out = pl.BlockSpec((1, rows, C_OUT), lambda b, r, n: (b, n, r))
    o, lse = pl.pallas_call(
        _dilated_body,
        grid=(bsz, dilation, sub // rows),
        in_specs=[cur, prev],
        out_specs=[out, out],
        out_shape=[jax.ShapeDtypeStruct((bsz, sub, dilation * C_OUT), bf16),
                   jax.ShapeDtypeStruct((bsz, sub, dilation * C_OUT), f32)],
        compiler_params=_params("parallel", "parallel", "arbitrary"),
        name=f"dilated_{dilation}",
    )(view, view)
    return o.reshape(bsz * sub, dilation * C_OUT), lse.reshape(bsz * sub, dilation * C_OUT)


def _merge_body(x_ref, ya_ref, yb_ref, o0_ref, o1_ref, o2_ref, l0_ref, l1_ref, l2_ref,
                g_ref, wgate_ref, wa_ref, wb_ref, wc_ref, wout_ref, out_ref, *perm_refs):
    x = x_ref[...]
    h = _rmsnorm_rows(x, g_ref[...]).astype(bf16)
    (o0, l0), (o1, l1), (o2, l2) = [
        (o_ref[...].astype(f32), l_ref[...]) if d == 1 else
        (_deinterleave_load(o_ref, d, po_ref), _deinterleave_load(l_ref, d, pl_ref))
        for (o_ref, l_ref, po_ref, pl_ref), (_, d) in zip(
            ((o0_ref, l0_ref, None, None), (o1_ref, l1_ref, perm_refs[0], perm_refs[1]),
             (o2_ref, l2_ref, perm_refs[2], perm_refs[3])), DIL_PAIRS)]
    mx = jnp.maximum(jnp.maximum(l0, l1), l2)
    e0, e1, e2 = jnp.exp(l0 - mx), jnp.exp(l1 - mx), jnp.exp(l2 - mx)
    yc = e0 * o0 + e1 * o1 + e2 * o2
    yc = (yc / (e0 + e1 + e2)).astype(bf16)

    def gate(i):
        return jax.nn.sigmoid(_dot(h, wgate_ref[:, i * D_MODEL:(i + 1) * D_MODEL]))

    merged = gate(0) * _dot(ya_ref[...], wa_ref[...])
    merged = merged + gate(1) * _dot(yb_ref[...], wb_ref[...])
    merged = merged + gate(2) * _dot(yc, wc_ref[...])
    out_ref[...] = x + _dot(merged.astype(bf16), wout_ref[...])


def _merge(x2, layer, ya, yb, os_, lses, gain, w_gates, w_a, w_b, w_c, w_out):
    n = x2.shape[0]
    tm = min(TOKEN_TILE, n)

    def tile(width):
        return pl.BlockSpec((tm, width), lambda i: (i, 0))

    return pl.pallas_call(
        _merge_body,
        grid=(n // tm,),
        in_specs=[tile(D_MODEL), tile(A_WIDTH), tile(S5_WIDTH)] + 2 * [
            pl.BlockSpec((tm // d, d * C_OUT), lambda i: (i, 0)) for _, d in DIL_PAIRS] +
                 [_resident((1, D_MODEL), layer), _resident((D_MODEL, 3 * D_MODEL), layer),
                  _resident((A_WIDTH, D_MODEL), layer), _resident((S5_WIDTH, D_MODEL), layer),
                  _resident((C_OUT, D_MODEL), layer), _resident((D_MODEL, D_MODEL), layer)],
        out_specs=tile(D_MODEL),
        out_shape=jax.ShapeDtypeStruct((n, D_MODEL), f32),
        scratch_shapes=[pltpu.VMEM((C_OUT // LANES, tm, LANES), f32)] * 4,
        compiler_params=_params("parallel"),
        name="merge",
    )(x2, ya, yb, *os_, *lses, gain, w_gates, w_a, w_b, w_c, w_out)


def _pack_w_in(w_in):
    offs = [0]
    for s in IN_SIZES:
        offs.append(offs[-1] + s)
    col = lambda i: w_in[:, :, offs[i]:offs[i + 1]]
    aq, ak, av, iq, ik, iw, su, cq, ck, cv, gates = [col(i) for i in range(len(IN_SIZES))]
    parts = [aq, ak, ak, iq, ik, ik, su]
    for g in range(len(DIL_PAIRS)):
        gs = slice(g * C_OUT, (g + 1) * C_OUT)
        parts += [cq[:, :, gs], ck[:, :, gs], cv[:, :, gs]]
    iw_pad = jnp.pad(iw, ((0, 0), (0, 0), (0, SUBLANES - IDX_HEADS)))
    w_t = jnp.transpose(jnp.concatenate([av, av, iw_pad], axis=2), (0, 2, 1))
    return jnp.concatenate(parts, axis=2).astype(bf16), w_t.astype(bf16), gates.astype(bf16)


def _segment_mean_matrix(width):
    seg = jnp.arange(width) // HEAD_DIM
    return jnp.where(seg[:, None] == seg[None, :], 1.0 / HEAD_DIM, 0.0).astype(bf16)


def _tile_gain(g, reps):
    return jnp.tile(g[:, None, :], (1, 1, reps))


def kernel(x, ffn1_norm, ffn1_gate, ffn1_up, ffn1_down, mix_norm, w_in, a_q_norm, a_k_norm, s5_lam_re, s5_lam_im, s5_log_dt, s5_b_re, s5_b_im, s5_c_re, s5_c_im, s5_d, s5_glu_a, s5_glu_b, c_q_norm, c_k_norm, w_branch_a, w_branch_b, w_branch_c, w_out, ffn2_norm, ffn2_gate, ffn2_up, ffn2_down):
    bsz, seq, _ = x.shape
    depth = w_in.shape[0]
    n = bsz * seq
    x2 = x.reshape(n, D_MODEL)
    assert TOKEN_TILE == KEY_CHUNK and seq % KEY_CHUNK == 0
    seg_mean = _segment_mean_matrix(A_WIDTH)
    grp_per_slab = S5_SLAB // S5_STATE
    cast = lambda w: w.astype(bf16)
    row = lambda g: g[:, None, :]
    ffn1 = (row(ffn1_norm), cast(ffn1_gate), cast(ffn1_up), cast(ffn1_down))
    ffn2 = (row(ffn2_norm), cast(ffn2_gate), cast(ffn2_up), cast(ffn2_down))
    w_packed, w_t, w_gates = _pack_w_in(w_in)
    head_gains = (_tile_gain(a_q_norm, A_HEADS), _tile_gain(a_k_norm, 2),
                  _tile_gain(c_q_norm, C_HEADS_PER_GROUP), _tile_gain(c_k_norm, C_HEADS_PER_GROUP))
    glu = (cast(s5_glu_a), cast(s5_glu_b))
    merge_w = (row(mix_norm), w_gates, cast(w_branch_a), cast(w_branch_b), cast(w_branch_c), cast(w_out))
    r3 = lambda a: a.reshape(bsz, seq, a.shape[-1])
    b_t = lambda b: jnp.tile(jnp.transpose(b, (2, 0, 1)).reshape(S5_GROUP, S5_LANES), (grp_per_slab, 1))
    c_t = lambda c: jnp.tile(jnp.transpose(c, (0, 2, 1)).reshape(S5_LANES, S5_GROUP), (1, grp_per_slab))
    for l in range(depth):
        x2 = _ffn(x2, l, *ffn1)
        aq, akk, vvt, iq, ikk, iwt, su, c0, c1, c2 = _in_proj(x2, l, row(mix_norm), w_packed, w_t, seg_mean,
                                                               *head_gains)
        vvt = vvt.reshape(bsz, seq // KEY_CHUNK, LANES, KEY_CHUNK)
        ya = _dsa(r3(aq), r3(iq), iwt, r3(akk), vvt, r3(ikk)).reshape(n, A_WIDTH)
        yb = _s5(r3(su), l, s5_lam_re[l].reshape(1, -1), s5_lam_im[l].reshape(1, -1),
                 jnp.repeat(s5_log_dt[l], S5_STATE).reshape(1, -1),
                 b_t(s5_b_re[l]), b_t(s5_b_im[l]), c_t(s5_c_re[l]), c_t(s5_c_im[l]),
                 s5_d[l].reshape(1, -1), *glu).reshape(n, S5_WIDTH)
        os_, lses = [], []
        for cg, (_, dilation) in zip((c0, c1, c2), DIL_PAIRS):
            o, lse = _dilated(cg.reshape(bsz, seq // dilation, dilation * C_WIDTH), dilation)
            os_.append(o)
            lses.append(lse)
        x2 = _merge(x2, l, ya, yb, os_, lses, *merge_w)
        x2 = _ffn(x2, l, *ffn2)
    return x2.reshape(bsz, seq, D_MODEL)
```

```python
import functools
import math

import jax
import jax.numpy as jnp
from jax import lax
from jax.experimental import pallas as pl
from jax.experimental.pallas import tpu as pltpu

f32 = jnp.float32
bf16 = jnp.bfloat16
i32 = jnp.int32

D_MODEL = 1024
D_FF = 2816
HEAD_DIM = 64
RMS_EPS = 1e-6
A_HEADS = 8
A_WIDTH = A_HEADS * HEAD_DIM
IDX_HEADS = 4
IDX_DIM = 64
TOPK_MAX = 256
S5_WIDTH = 512
S5_GROUP = 16
S5_GROUPS = 32
S5_STATE = 64
S5_LANES = S5_GROUPS * S5_STATE
DIL_PAIRS = ((128, 1), (512, 4), (2048, 16))
C_HEADS_PER_GROUP = 4
C_OUT = C_HEADS_PER_GROUP * HEAD_DIM
C_WIDTH = 3 * C_OUT
IN_SIZES = (A_WIDTH, HEAD_DIM, HEAD_DIM, IDX_HEADS * IDX_DIM, IDX_DIM, IDX_HEADS,
            S5_WIDTH, C_WIDTH, C_WIDTH, C_WIDTH, 3 * D_MODEL)

LANES = 128
SUBLANES = 8
VMEM_LIMIT_BYTES = 56 * 1024 * 1024

TOKEN_TILE = 512
FF_CHUNK = 256
Q_TILE = 128
KEY_CHUNK = 512
COUNT_ROWS = 64
KEY_BITS = 32
PLANE_ROWS = KEY_BITS * 8
RANK_ROWS = 256
LOOP_UNITS = (4, 2, 1)
S5_TIME_TILE = 512
S5_SLAB = 512
S5_SCAN_LANES = 512
DIL_SPAN = 128
DIL_ROWS = 512

NEG_INF = float("-inf")
LOG2_E = math.log2(math.e)
INT_MIN = -2 ** 31
NEG_INF_KEY = INT_MIN + 0x7FFFFF

_NT = (((1,), (1,)), ((), ()))


def _params(*sem):
    return pltpu.CompilerParams(dimension_semantics=sem, vmem_limit_bytes=VMEM_LIMIT_BYTES)


def _resident(shape, layer=None):
    nd = len(shape)
    if layer is None:
        return pl.BlockSpec(shape, lambda *_: (0,) * nd, pipeline_mode=pl.Buffered(1))
    return pl.BlockSpec((None,) + tuple(shape), lambda *_: (layer,) + (0,) * nd, pipeline_mode=pl.Buffered(1))


def _dot(a, b):
    return jnp.dot(a, b, preferred_element_type=f32)


def _dot_nt(a, b):
    return lax.dot_general(a, b, _NT, preferred_element_type=f32)


def _rmsnorm_rows(x, gain):
    return x * lax.rsqrt(jnp.mean(x * x, axis=-1, keepdims=True) + RMS_EPS) * gain


def _head_rmsnorm(z, gain_tiled, seg_mean):
    ms = _dot((z * z).astype(bf16), seg_mean)
    return z * lax.rsqrt(ms + RMS_EPS) * gain_tiled


def _ffn_body(x_ref, g_ref, wg_ref, wu_ref, wd_ref, o_ref, acc_ref):
    x = x_ref[...]
    h = _rmsnorm_rows(x, g_ref[...]).astype(bf16)
    for c in range(D_FF // FF_CHUNK):
        sl = slice(c * FF_CHUNK, (c + 1) * FF_CHUNK)
        gate = _dot(h, wg_ref[:, sl])
        up = _dot(h, wu_ref[:, sl])
        act = (gate * jax.nn.sigmoid(gate) * up).astype(bf16)
        contrib = _dot(act, wd_ref[sl, :])
        if c == 0:
            acc_ref[...] = contrib
        else:
            acc_ref[...] += contrib
    o_ref[...] = x + 0.5 * acc_ref[...]


def _ffn(x2, layer, gain, w_gate, w_up, w_down):
    n = x2.shape[0]
    tm = min(TOKEN_TILE, n)
    tile = pl.BlockSpec((tm, D_MODEL), lambda i: (i, 0))
    return pl.pallas_call(
        _ffn_body,
        grid=(n // tm,),
        in_specs=[tile, _resident((1, D_MODEL), layer), _resident((D_MODEL, D_FF), layer),
                  _resident((D_MODEL, D_FF), layer), _resident((D_FF, D_MODEL), layer)],
        out_specs=tile,
        out_shape=jax.ShapeDtypeStruct((n, D_MODEL), f32),
        scratch_shapes=[pltpu.VMEM((tm, D_MODEL), f32)],
        compiler_params=_params("parallel"),
        name="ffn",
    )(x2, gain, w_gate, w_up, w_down)


_P_AQ = 0
_P_AKK = _P_AQ + A_WIDTH
_P_IQ = _P_AKK + LANES
_P_IKK = _P_IQ + IDX_HEADS * IDX_DIM
_P_SU = _P_IKK + LANES
_P_C = _P_SU + S5_WIDTH
_P_TOTAL = _P_C + 3 * C_WIDTH
_T_ROWS = LANES + SUBLANES


def _interleave_store(dst_ref, value, dilation, perm_ref):
    rows, width = value.shape
    for j in range(width // LANES):
        perm_ref[j] = value[:, j * LANES:(j + 1) * LANES]
    for r in range(dilation):
        for j in range(width // LANES):
            piece = perm_ref[j, pl.ds(r, rows // dilation, stride=dilation), :]
            dst_ref[:, r * width + j * LANES:r * width + (j + 1) * LANES] = piece.astype(dst_ref.dtype)


def _deinterleave_load(src_ref, dilation, perm_ref):
    width = src_ref.shape[1] // dilation
    rows = src_ref.shape[0] * dilation
    for r in range(dilation):
        for j in range(width // LANES):
            piece = src_ref[:, r * width + j * LANES:r * width + (j + 1) * LANES]
            perm_ref[j, pl.ds(r, rows // dilation, stride=dilation), :] = piece.astype(f32)
    return jnp.concatenate([perm_ref[j] for j in range(width // LANES)], axis=1)


def _in_proj_body(x_ref, g_ref, w_ref, wt_ref, seg_ref, aqg_ref, akg_ref, cqg_ref, ckg_ref,
                  aq_ref, akk_ref, vvt_ref, iq_ref, ikk_ref, iwt_ref, su_ref, c0_ref, c1_ref, c2_ref, perm_ref):
    h = _rmsnorm_rows(x_ref[...], g_ref[...]).astype(bf16)
    full_c = _dot(h, w_ref[:, _P_C:])
    full_a = _dot(h, w_ref[:, :_P_C])

    def proj(start, width):
        if start >= _P_C:
            return full_c[:, start - _P_C:start - _P_C + width]
        return full_a[:, start:start + width]

    seg = seg_ref[...]
    for c_ref, (_, dilation), g in zip((c0_ref, c1_ref, c2_ref), DIL_PAIRS, range(len(DIL_PAIRS))):
        base = _P_C + g * C_WIDTH
        cq = _head_rmsnorm(proj(base, C_OUT), cqg_ref[...], seg[:C_OUT, :C_OUT])
        ck = _head_rmsnorm(proj(base + C_OUT, C_OUT), ckg_ref[...], seg[:C_OUT, :C_OUT])
        qkv = (cq, ck, proj(base + 2 * C_OUT, C_OUT))
        if dilation == 1:
            for i, part in enumerate(qkv):
                c_ref[:, i * C_OUT:(i + 1) * C_OUT] = part.astype(bf16)
        else:
            _interleave_store(c_ref, jnp.concatenate(qkv, axis=1), dilation, perm_ref)
    aq = _head_rmsnorm(proj(_P_AQ, A_WIDTH), aqg_ref[...], seg) * (HEAD_DIM ** -0.5)
    aq_ref[...] = aq.astype(bf16)
    akk_ref[...] = _head_rmsnorm(proj(_P_AKK, LANES), akg_ref[...], seg[:LANES, :LANES]).astype(bf16)
    iq_ref[...] = proj(_P_IQ, IDX_HEADS * IDX_DIM).astype(bf16)
    ikk_ref[...] = proj(_P_IKK, LANES).astype(bf16)
    su_ref[...] = proj(_P_SU, S5_WIDTH)
    vvt_ref[0] = _dot_nt(wt_ref[:LANES, :], h).astype(bf16)
    iwt_ref[0] = _dot_nt(wt_ref[LANES:, :], h)


def _in_proj(x2, layer, gain, w_packed, w_t, seg_mean, aq_gain, ak_gain, cq_gain, ck_gain):
    n = x2.shape[0]
    tm = min(TOKEN_TILE, n)
    rows = lambda w: pl.BlockSpec((tm, w), lambda i: (i, 0))
    cols = lambda r: pl.BlockSpec((1, r, tm), lambda i: (i, 0, 0))
    outs = [(rows(A_WIDTH), (n, A_WIDTH), bf16), (rows(LANES), (n, LANES), bf16),
            (cols(LANES), (n // tm, LANES, tm), bf16), (rows(IDX_HEADS * IDX_DIM), (n, IDX_HEADS * IDX_DIM), bf16),
            (rows(LANES), (n, LANES), bf16), (cols(SUBLANES), (n // tm, SUBLANES, tm), f32),
            (rows(S5_WIDTH), (n, S5_WIDTH), f32)]
    for _, d in DIL_PAIRS:
        outs.append((pl.BlockSpec((tm // d, d * C_WIDTH), lambda i: (i, 0)), (n // d, d * C_WIDTH), bf16))
    return pl.pallas_call(
        _in_proj_body,
        grid=(n // tm,),
        in_specs=[rows(D_MODEL), _resident((1, D_MODEL), layer),
                  _resident((D_MODEL, _P_TOTAL), layer), _resident((_T_ROWS, D_MODEL), layer),
                  _resident((A_WIDTH, A_WIDTH)),
                  _resident((1, A_WIDTH), layer), _resident((1, LANES), layer),
                  _resident((1, C_OUT), layer), _resident((1, C_OUT), layer)],
        out_specs=[o[0] for o in outs],
        out_shape=[jax.ShapeDtypeStruct(o[1], o[2]) for o in outs],
        scratch_shapes=[pltpu.VMEM((C_WIDTH // LANES, tm, LANES), f32)],
        compiler_params=_params("parallel"),
        name="in_proj",
    )(x2, gain, w_packed, w_t, seg_mean, aq_gain, ak_gain, cq_gain, ck_gain)


def _float_order_key(x):
    bits = pltpu.bitcast(x, i32)
    return jnp.where(bits < 0, bits ^ 0x7FFFFFFF, bits)


def _fold_rows(x, op, keep=SUBLANES):
    return op(x.reshape(x.shape[0] // keep, keep, x.shape[1]), axis=0)


def _bit_planes(words):
    a = [words[t * SUBLANES:(t + 1) * SUBLANES] for t in range(KEY_BITS)]
    for j, m in ((16, 0x0000FFFF), (8, 0x00FF00FF), (4, 0x0F0F0F0F), (2, 0x33333333), (1, 0x55555555)):
        shift = jnp.full(a[0].shape, j, i32)
        for k in range(KEY_BITS):
            if k & j == 0:
                t = (a[k] ^ lax.shift_right_logical(a[k + j], shift)) & m
                a[k] = a[k] ^ t
                a[k + j] = a[k + j] ^ lax.shift_left(t, shift)
    return a


def _dsa_body(topk, aq_ref, iq_ref, iwt_ref, kk_ref, vvt_ref, ikk_ref, rank_ref, o_ref,
              key_ref, plane_ref, cand_ref, s_ref, m_ref, l_ref, acc_ref):
    qb = pl.program_id(1)
    n_chunks = lax.shift_right_logical(qb * Q_TILE, int(math.log2(KEY_CHUNK))) + 1
    q_pos = qb * Q_TILE + lax.broadcasted_iota(i32, (1, Q_TILE), 1)
    key_iota = lax.broadcasted_iota(i32, (KEY_CHUNK, Q_TILE), 0)
    low_half = lax.broadcasted_iota(i32, (Q_TILE, LANES), 1) < HEAD_DIM
    zero_b = jnp.zeros((), bf16)

    def stack_heads(x, n_heads):
        blocks = []
        for h in range(n_heads):
            pair = x[:, (h // 2) * LANES:(h // 2 + 1) * LANES]
            blocks.append(jnp.where(low_half if h % 2 == 0 else ~low_half, pair, zero_b))
        return jnp.concatenate(blocks, axis=0)

    @pl.when((pl.program_id(0) == 0) & (qb == 0))
    def _():
        plane_ref[...] = jnp.zeros(plane_ref.shape, i32)

    iq_all = stack_heads(iq_ref[0], IDX_HEADS)
    w_t = iwt_ref[0] * ((IDX_DIM ** -0.5) * (IDX_HEADS ** -0.5))

    def key_rows(c, part, rows):
        return pl.ds(pl.multiple_of(c * KEY_CHUNK + part * rows, rows), rows)

    def chunk_loop(body, carry, end=n_chunks):
        start = 0
        for unit in LOOP_UNITS:
            n_iter = lax.shift_right_logical(end - start, int(math.log2(unit)))

            def multi(i, cr, unit=unit, start=start):
                for j in range(unit):
                    cr = body(start + i * unit + j, cr)
                return cr

            carry = lax.fori_loop(0, n_iter, multi, carry)
            start = start + n_iter * unit
        return carry

    def score_chunk(c, _, diagonal=False):
        logits = _dot_nt(ikk_ref[0, key_rows(c, 0, KEY_CHUNK), :], iq_all)
        score = None
        for h in range(IDX_HEADS):
            term = jnp.maximum(logits[:, h * Q_TILE:(h + 1) * Q_TILE], 0.0) * w_t[h:h + 1, :]
            score = term if score is None else score + term
        score = jnp.where(score == 0.0, 0.0, score)
        if diagonal:
            score = jnp.where(c * KEY_CHUNK + key_iota <= q_pos, score, NEG_INF)
        key = _float_order_key(score)
        key_ref[c] = key
        for part in range(KEY_CHUNK // PLANE_ROWS):
            planes = _bit_planes(key[part * PLANE_ROWS:(part + 1) * PLANE_ROWS] ^ INT_MIN)
            for i in range(KEY_BITS):
                plane_ref[i, c * (KEY_CHUNK // PLANE_ROWS) + part] = planes[i]
        return 0

    chunk_loop(score_chunk, 0, end=n_chunks - 1)
    score_chunk(n_chunks - 1, 0, diagonal=True)

    n_planes = n_chunks * (KEY_CHUNK // PLANE_ROWS)
    block_id = lax.broadcasted_iota(i32, cand_ref.shape, 0)
    cand_ref[...] = jnp.where(block_id < n_planes, -1, 0)

    def bit_step(i, carry):
        prefix, n_above = carry
        cand = cand_ref[...]
        ones = cand & plane_ref[i]
        cnt = jnp.sum(jnp.sum(lax.population_count(ones), axis=0).astype(f32), axis=0, keepdims=True)
        accept = n_above + cnt >= topk
        cand_ref[...] = jnp.where(accept, ones, cand ^ ones)
        prefix = jnp.where(accept, prefix | lax.shift_left(jnp.int32(1), KEY_BITS - 1 - i), prefix)
        return prefix, jnp.where(accept, n_above, n_above + cnt)

    prefix, n_gt = lax.fori_loop(0, KEY_BITS, bit_step,
                                 (jnp.zeros((1, Q_TILE), i32), jnp.zeros((1, Q_TILE), f32)))
    thr = prefix ^ INT_MIN
    need = jnp.where(thr <= NEG_INF_KEY, 0.0, topk - n_gt)
    rank_mat = rank_ref[...]

    q_all = stack_heads(aq_ref[0], A_HEADS)

    m_ref[...] = jnp.full(m_ref.shape, NEG_INF, f32)

    def logits_chunk(c, seen):
        for part in range(KEY_CHUNK // RANK_ROWS):
            rows = slice(part * RANK_ROWS, (part + 1) * RANK_ROWS)
            k = key_ref[c, rows, :]
            eq = k == thr
            tie = jnp.where(eq, 1.0, 0.0)
            rank = _dot(rank_mat, tie.astype(bf16)) + seen
            seen = seen + jnp.sum(_fold_rows(tie, jnp.sum, COUNT_ROWS), axis=0, keepdims=True)
            order = jnp.where(eq, rank, jnp.where(k > thr, -1.0, 2.0 ** 30))
            bias = jnp.where(order < need, 0.0, NEG_INF)
            s = _dot_nt(kk_ref[0, key_rows(c, part, RANK_ROWS), :], q_all)
            for h in range(A_HEADS):
                cols = slice(h * Q_TILE, (h + 1) * Q_TILE)
                sh = (s[:, cols] + bias) * LOG2_E
                s_ref[c, rows, cols] = sh
                m_ref[:, cols] = jnp.maximum(m_ref[:, cols], _fold_rows(sh, jnp.max))
        return seen

    chunk_loop(logits_chunk, jnp.zeros((1, Q_TILE), f32))
    m = jnp.max(m_ref[...], axis=0, keepdims=True)

    l_ref[...] = jnp.zeros(l_ref.shape, f32)
    acc_ref[...] = jnp.zeros(acc_ref.shape, f32)

    def softmax_chunk(c, _):
        for part in range(KEY_CHUNK // RANK_ROWS):
            rows = slice(part * RANK_ROWS, (part + 1) * RANK_ROWS)
            p = jnp.exp2(s_ref[c, rows, :] - m)
            l_ref[...] += _fold_rows(p, jnp.sum)
            acc_ref[...] += _dot(vvt_ref[0, c, :, rows], p.astype(bf16))
        return 0

    chunk_loop(softmax_chunk, 0)
    out_t = acc_ref[...] / jnp.sum(l_ref[...], axis=0, keepdims=True)
    top_rows = lax.broadcasted_iota(i32, (LANES, Q_TILE), 0) < HEAD_DIM
    for j in range(A_HEADS // 2):
        even = out_t[:, (2 * j) * Q_TILE:(2 * j + 1) * Q_TILE]
        odd = out_t[:, (2 * j + 1) * Q_TILE:(2 * j + 2) * Q_TILE]
        o_ref[0, :, j * LANES:(j + 1) * LANES] = jnp.where(top_rows, even, odd).T.astype(bf16)


def _tie_rank_matrix():
    i = jnp.arange(RANK_ROWS)
    return (i[None, :] < i[:, None]).astype(bf16)


def _dsa(aq, iq, iwt, akk, vvt, ikk):
    bsz, seq, _ = aq.shape
    topk = min(TOPK_MAX, seq // 4)
    n_kc = seq // KEY_CHUNK
    q_per_chunk = KEY_CHUNK // Q_TILE

    def q_spec(width):
        return pl.BlockSpec((1, Q_TILE, width), lambda b, q: (b, q, 0))

    kv_spec = pl.BlockSpec((1, seq, LANES), lambda b, q: (b, 0, 0))
    return pl.pallas_call(
        functools.partial(_dsa_body, topk),
        grid=(bsz, seq // Q_TILE),
        in_specs=[q_spec(A_WIDTH), q_spec(IDX_HEADS * IDX_DIM),
                  pl.BlockSpec((1, SUBLANES, Q_TILE), lambda b, q: (b * n_kc + q // q_per_chunk, 0, q % q_per_chunk)),
                  kv_spec, pl.BlockSpec((1, n_kc, LANES, KEY_CHUNK), lambda b, q: (b, 0, 0, 0)), kv_spec,
                  _resident((RANK_ROWS, RANK_ROWS))],
        out_specs=q_spec(A_WIDTH),
        out_shape=jax.ShapeDtypeStruct((bsz, seq, A_WIDTH), bf16),
        scratch_shapes=[pltpu.VMEM((n_kc, KEY_CHUNK, Q_TILE), i32),
                        pltpu.VMEM((KEY_BITS, seq // PLANE_ROWS, SUBLANES, Q_TILE), i32),
                        pltpu.VMEM((seq // PLANE_ROWS, SUBLANES, Q_TILE), i32),
                        pltpu.VMEM((n_kc, KEY_CHUNK, A_HEADS * Q_TILE), f32),
                        pltpu.VMEM((SUBLANES, A_HEADS * Q_TILE), f32),
                        pltpu.VMEM((SUBLANES, A_HEADS * Q_TILE), f32),
                        pltpu.VMEM((LANES, A_HEADS * Q_TILE), f32)],
        compiler_params=_params("arbitrary", "arbitrary"),
        name="dsa",
    )(aq, iq, iwt, akk, vvt, ikk, _tie_rank_matrix())


def _cmul(ar, ai, br, bi):
    return ar * br - ai * bi, ar * bi + ai * br


def _s5_prepare(lam_re_ref, lam_im_ref, logdt_ref, bre_ref, bim_ref, cre_ref, cim_ref,
                bmat_re, bmat_im, cmat_re, cmat_im, shift_re, shift_im, pow_re, pow_im):
    lr = jnp.minimum(lam_re_ref[...], -1e-4)
    li = lam_im_ref[...]
    dt = jnp.exp(logdt_ref[...])
    mag = jnp.exp(lr * dt)
    l1r, l1i = mag * jnp.cos(li * dt), mag * jnp.sin(li * dt)
    den = lr * lr + li * li
    cr = ((l1r - 1.0) * lr + l1i * li) / den
    ci = (l1i * lr - (l1r - 1.0) * li) / den

    n_slab = S5_LANES // S5_SLAB
    row_g = lax.broadcasted_iota(i32, (LANES, S5_SLAB), 0) // S5_GROUP
    lane_g = lax.broadcasted_iota(i32, (LANES, S5_SLAB), 1) // S5_STATE
    diag_b = row_g == lane_g
    row_g2 = lax.broadcasted_iota(i32, (S5_SLAB, LANES), 0) // S5_STATE
    lane_g2 = lax.broadcasted_iota(i32, (S5_SLAB, LANES), 1) // S5_GROUP
    diag_c = row_g2 == lane_g2
    for j in range(n_slab):
        sl = slice(j * S5_SLAB, (j + 1) * S5_SLAB)
        bbr, bbi = _cmul(cr[:, sl], ci[:, sl], bre_ref[:, sl], bim_ref[:, sl])
        bmat_re[j] = jnp.where(diag_b, bbr, 0.0).astype(bf16)
        bmat_im[j] = jnp.where(diag_b, bbi, 0.0).astype(bf16)
        cmat_re[j] = jnp.where(diag_c, cre_ref[sl, :], 0.0).astype(bf16)
        cmat_im[j] = jnp.where(diag_c, -cim_ref[sl, :], 0.0).astype(bf16)

    l2r, l2i = _cmul(l1r, l1i, l1r, l1i)
    l3r, l3i = _cmul(l2r, l2i, l1r, l1i)
    l4r, l4i = _cmul(l2r, l2i, l2r, l2i)
    l5r, l5i = _cmul(l4r, l4i, l1r, l1i)
    l6r, l6i = _cmul(l4r, l4i, l2r, l2i)
    l7r, l7i = _cmul(l4r, l4i, l3r, l3i)
    l8r, l8i = _cmul(l4r, l4i, l4r, l4i)
    row = lax.broadcasted_iota(i32, (SUBLANES, S5_LANES), 0)
    for idx, (k, pr, pi) in enumerate(((1, l1r, l1i), (2, l2r, l2i), (4, l4r, l4i))):
        shift_re[idx] = jnp.where(row >= k, pr, 0.0)
        shift_im[idx] = jnp.where(row >= k, pi, 0.0)
    pr_acc = jnp.zeros((SUBLANES, S5_LANES), f32)
    pi_acc = jnp.zeros((SUBLANES, S5_LANES), f32)
    powers = ((l1r, l1i), (l2r, l2i), (l3r, l3i), (l4r, l4i), (l5r, l5i), (l6r, l6i), (l7r, l7i), (l8r, l8i))
    for i, (pr, pi) in enumerate(powers):
        pr_acc = jnp.where(row == i, pr, pr_acc)
        pi_acc = jnp.where(row == i, pi, pi_acc)
    pow_re[...] = pr_acc
    pow_im[...] = pi_acc


def _s5_body(u_ref, lam_re_ref, lam_im_ref, logdt_ref, bre_ref, bim_ref, cre_ref, cim_ref, d_ref,
             glu_a_ref, glu_b_ref, o_ref,
             bmat_re, bmat_im, cmat_re, cmat_im, shift_re, shift_im, pow_re, pow_im,
             carry_re, carry_im, st_re, st_im):
    t_idx = pl.program_id(1)

    @pl.when((pl.program_id(0) == 0) & (t_idx == 0))
    def _():
        _s5_prepare(lam_re_ref, lam_im_ref, logdt_ref, bre_ref, bim_ref, cre_ref, cim_ref,
                    bmat_re, bmat_im, cmat_re, cmat_im, shift_re, shift_im, pow_re, pow_im)

    @pl.when(t_idx == 0)
    def _():
        carry_re[...] = jnp.zeros(carry_re.shape, f32)
        carry_im[...] = jnp.zeros(carry_im.shape, f32)

    u = u_ref[0]
    ub = u.astype(bf16)
    n_slab = S5_LANES // S5_SLAB
    n_tiles = u.shape[0] // SUBLANES
    ys = []
    for j in range(n_slab):
        sl = slice(j * S5_SLAB, (j + 1) * S5_SLAB)
        uj = ub[:, j * LANES:(j + 1) * LANES]
        st_re[:, sl] = _dot(uj, bmat_re[j])
        st_im[:, sl] = _dot(uj, bmat_im[j])

        coef = [(shift_re[i, :, sl], shift_im[i, :, sl]) for i in range(3)]
        pwr, pwi = pow_re[:, sl], pow_im[:, sl]
        c_re, c_im = carry_re[:, sl], carry_im[:, sl]
        for n in range(n_tiles):
            rows = slice(n * SUBLANES, (n + 1) * SUBLANES)
            xr, xi = st_re[rows, sl], st_im[rows, sl]
            for i, k in enumerate((1, 2, 4)):
                dr, di = _cmul(coef[i][0], coef[i][1], pltpu.roll(xr, k, 0), pltpu.roll(xi, k, 0))
                xr, xi = xr + dr, xi + di
            dr, di = _cmul(pwr, pwi, c_re, c_im)
            xr, xi = xr + dr, xi + di
            st_re[rows, sl] = xr
            st_im[rows, sl] = xi
            c_re, c_im = xr[SUBLANES - 1:SUBLANES, :], xi[SUBLANES - 1:SUBLANES, :]
        carry_re[:, sl] = c_re
        carry_im[:, sl] = c_im
        ys.append(_dot(st_re[:, sl].astype(bf16), cmat_re[j]) + _dot(st_im[:, sl].astype(bf16), cmat_im[j]))
    y = jnp.concatenate(ys, axis=1) + d_ref[...] * u
    g = jax.nn.gelu(y).astype(bf16)
    o_ref[0] = (_dot(g, glu_a_ref[...]) * jax.nn.sigmoid(_dot(g, glu_b_ref[...]))).astype(bf16)


def _s5(su, layer, lam_re, lam_im, logdt, b_re, b_im, c_re, c_im, d_skip, glu_a, glu_b):
    bsz, seq, _ = su.shape
    tt = min(S5_TIME_TILE, seq)
    n_slab = S5_LANES // S5_SLAB
    tile = pl.BlockSpec((1, tt, S5_WIDTH), lambda b, t: (b, t, 0))
    return pl.pallas_call(
        _s5_body,
        grid=(bsz, seq // tt),
        in_specs=[tile, _resident((1, S5_LANES)), _resident((1, S5_LANES)), _resident((1, S5_LANES)),
                  _resident((LANES, S5_LANES)), _resident((LANES, S5_LANES)),
                  _resident((S5_LANES, LANES)), _resident((S5_LANES, LANES)), _resident((1, S5_WIDTH)),
                  _resident((S5_WIDTH, S5_WIDTH), layer), _resident((S5_WIDTH, S5_WIDTH), layer)],
        out_specs=tile,
        out_shape=jax.ShapeDtypeStruct((bsz, seq, S5_WIDTH), bf16),
        scratch_shapes=[pltpu.VMEM((n_slab, LANES, S5_SLAB), bf16), pltpu.VMEM((n_slab, LANES, S5_SLAB), bf16),
                        pltpu.VMEM((n_slab, S5_SLAB, LANES), bf16), pltpu.VMEM((n_slab, S5_SLAB, LANES), bf16),
                        pltpu.VMEM((3, SUBLANES, S5_LANES), f32), pltpu.VMEM((3, SUBLANES, S5_LANES), f32),
                        pltpu.VMEM((SUBLANES, S5_LANES), f32), pltpu.VMEM((SUBLANES, S5_LANES), f32),
                        pltpu.VMEM((1, S5_LANES), f32), pltpu.VMEM((1, S5_LANES), f32),
                        pltpu.VMEM((tt, S5_LANES), f32), pltpu.VMEM((tt, S5_LANES), f32)],
        compiler_params=_params("arbitrary", "arbitrary"),
        name="s5",
    )(su, lam_re, lam_im, logdt, b_re, b_im, c_re, c_im, d_skip, glu_a, glu_b)


def _dilated_body(cur_ref, prev_ref, o_ref, lse_ref):
    n = pl.program_id(2)
    rows = cur_ref.shape[1]
    qi = lax.broadcasted_iota(i32, (DIL_SPAN, 2 * DIL_SPAN), 0)
    kj = lax.broadcasted_iota(i32, (DIL_SPAN, 2 * DIL_SPAN), 1)
    band = (kj >= qi) & (kj <= qi + DIL_SPAN)
    band_first = band & ((kj >= DIL_SPAN) | (n > 0))
    low_half = lax.broadcasted_iota(i32, (DIL_SPAN, LANES), 1) < HEAD_DIM
    zero_b = jnp.zeros((), bf16)
    for res in range(cur_ref.shape[2] // C_WIDTH):
        q0, k0, v0, o0 = res * C_WIDTH, res * C_WIDTH + C_OUT, res * C_WIDTH + 2 * C_OUT, res * C_OUT
        for blk in range(rows // DIL_SPAN):
            r0 = blk * DIL_SPAN
            q = cur_ref[0, r0:r0 + DIL_SPAN, q0:q0 + C_OUT]
            if blk == 0:
                k = jnp.concatenate([prev_ref[0, :, k0:k0 + C_OUT], cur_ref[0, 0:DIL_SPAN, k0:k0 + C_OUT]], axis=0)
                v = jnp.concatenate([prev_ref[0, :, v0:v0 + C_OUT], cur_ref[0, 0:DIL_SPAN, v0:v0 + C_OUT]], axis=0)
            else:
                k = cur_ref[0, r0 - DIL_SPAN:r0 + DIL_SPAN, k0:k0 + C_OUT]
                v = cur_ref[0, r0 - DIL_SPAN:r0 + DIL_SPAN, v0:v0 + C_OUT]
            mask = band_first if blk == 0 else band
            for j in range(C_OUT // LANES):
                ls = slice(j * LANES, (j + 1) * LANES)
                out_ls = slice(o0 + j * LANES, o0 + (j + 1) * LANES)
                outs, lses = [], []
                for half in (low_half, ~low_half):
                    s = _dot_nt(jnp.where(half, q[:, ls], zero_b), k[:, ls]) * (HEAD_DIM ** -0.5)
                    s = jnp.where(mask, s, NEG_INF)
                    m = jnp.max(s, axis=1, keepdims=True)
                    p = jnp.exp(s - m)
                    den = jnp.sum(p, axis=1, keepdims=True)
                    outs.append(_dot(p.astype(bf16), v[:, ls]) / den)
                    lses.append(jnp.broadcast_to(m + jnp.log(den), (DIL_SPAN, LANES)))
                o_ref[0, r0:r0 + DIL_SPAN, out_ls] = jnp.where(low_half, outs[0], outs[1]).astype(bf16)
                lse_ref[0, r0:r0 + DIL_SPAN, out_ls] = jnp.where(low_half, lses[0], lses[1])


def _dilated(view, dilation):
    bsz, sub, _ = view.shape
    rows = min(DIL_ROWS, sub)
    n_res = min(DIL_ROWS // rows, dilation)
    blocks_per_step = rows // DIL_SPAN
    cur = pl.BlockSpec((1, rows, n_res * C_WIDTH), lambda b, r, n: (b, n, r))
    prev = pl.BlockSpec((1, DIL_SPAN, n_res * C_WIDTH),
                        lambda b, r, n: (b, jnp.maximum(n * blocks_per_step - 1, 0), r))
    out = pl.BlockSpec((1, rows, n_res * C_OUT), lambda b, r, n: (b, n, r))
    o, lse = pl.pallas_call(
        _dilated_body,
        grid=(bsz, dilation // n_res, sub // rows),
        in_specs=[cur, prev],
        out_specs=[out, out],
        out_shape=[jax.ShapeDtypeStruct((bsz, sub, dilation * C_OUT), bf16),
                   jax.ShapeDtypeStruct((bsz, sub, dilation * C_OUT), f32)],
        compiler_params=_params("parallel", "parallel", "arbitrary"),
        name=f"dilated_{dilation}",
    )(view, view)
    return o.reshape(bsz * sub, dilation * C_OUT), lse.reshape(bsz * sub, dilation * C_OUT)


def _merge_body(x_ref, ya_ref, yb_ref, o0_ref, o1_ref, o2_ref, l0_ref, l1_ref, l2_ref,
                g_ref, wgate_ref, wa_ref, wb_ref, wc_ref, wout_ref, out_ref, *perm_refs):
    x = x_ref[...]
    h = _rmsnorm_rows(x, g_ref[...]).astype(bf16)
    (o0, l0), (o1, l1), (o2, l2) = [
        (o_ref[...].astype(f32), l_ref[...]) if d == 1 else
        (_deinterleave_load(o_ref, d, po_ref), _deinterleave_load(l_ref, d, pl_ref))
        for (o_ref, l_ref, po_ref, pl_ref), (_, d) in zip(
            ((o0_ref, l0_ref, None, None), (o1_ref, l1_ref, perm_refs[0], perm_refs[1]),
             (o2_ref, l2_ref, perm_refs[2], perm_refs[3])), DIL_PAIRS)]
    mx = jnp.maximum(jnp.maximum(l0, l1), l2)
    e0, e1, e2 = jnp.exp(l0 - mx), jnp.exp(l1 - mx), jnp.exp(l2 - mx)
    yc = e0 * o0 + e1 * o1 + e2 * o2
    yc = (yc / (e0 + e1 + e2)).astype(bf16)

    def gate(i):
        return jax.nn.sigmoid(_dot(h, wgate_ref[:, i * D_MODEL:(i + 1) * D_MODEL]))

    merged = gate(0) * _dot(ya_ref[...], wa_ref[...])
    merged = merged + gate(1) * _dot(yb_ref[...], wb_ref[...])
    merged = merged + gate(2) * _dot(yc, wc_ref[...])
    out_ref[...] = x + _dot(merged.astype(bf16), wout_ref[...])


def _merge(x2, layer, ya, yb, os_, lses, gain, w_gates, w_a, w_b, w_c, w_out):
    n = x2.shape[0]
    tm = min(TOKEN_TILE, n)

    def tile(width):
        return pl.BlockSpec((tm, width), lambda i: (i, 0))

    return pl.pallas_call(
        _merge_body,
        grid=(n // tm,),
        in_specs=[tile(D_MODEL), tile(A_WIDTH), tile(S5_WIDTH)] + 2 * [
            pl.BlockSpec((tm // d, d * C_OUT), lambda i: (i, 0)) for _, d in DIL_PAIRS] +
                 [_resident((1, D_MODEL), layer), _resident((D_MODEL, 3 * D_MODEL), layer),
                  _resident((A_WIDTH, D_MODEL), layer), _resident((S5_WIDTH, D_MODEL), layer),
                  _resident((C_OUT, D_MODEL), layer), _resident((D_MODEL, D_MODEL), layer)],
        out_specs=tile(D_MODEL),
        out_shape=jax.ShapeDtypeStruct((n, D_MODEL), f32),
        scratch_shapes=[pltpu.VMEM((C_OUT // LANES, tm, LANES), f32)] * 4,
        compiler_params=_params("parallel"),
        name="merge",
    )(x2, ya, yb, *os_, *lses, gain, w_gates, w_a, w_b, w_c, w_out)


def _pack_w_in(w_in):
    offs = [0]
    for s in IN_SIZES:
        offs.append(offs[-1] + s)
    col = lambda i: w_in[:, :, offs[i]:offs[i + 1]]
    aq, ak, av, iq, ik, iw, su, cq, ck, cv, gates = [col(i) for i in range(len(IN_SIZES))]
    parts = [aq, ak, ak, iq, ik, ik, su]
    for g in range(len(DIL_PAIRS)):
        gs = slice(g * C_OUT, (g + 1) * C_OUT)
        parts += [cq[:, :, gs], ck[:, :, gs], cv[:, :, gs]]
    iw_pad = jnp.pad(iw, ((0, 0), (0, 0), (0, SUBLANES - IDX_HEADS)))
    w_t = jnp.transpose(jnp.concatenate([av, av, iw_pad], axis=2), (0, 2, 1))
    return jnp.concatenate(parts, axis=2).astype(bf16), w_t.astype(bf16), gates.astype(bf16)


def _segment_mean_matrix(width):
    seg = jnp.arange(width) // HEAD_DIM
    return jnp.where(seg[:, None] == seg[None, :], 1.0 / HEAD_DIM, 0.0).astype(bf16)


def _tile_gain(g, reps):
    return jnp.tile(g[:, None, :], (1, 1, reps))


def kernel(x, ffn1_norm, ffn1_gate, ffn1_up, ffn1_down, mix_norm, w_in, a_q_norm, a_k_norm, s5_lam_re, s5_lam_im, s5_log_dt, s5_b_re, s5_b_im, s5_c_re, s5_c_im, s5_d, s5_glu_a, s5_glu_b, c_q_norm, c_k_norm, w_branch_a, w_branch_b, w_branch_c, w_out, ffn2_norm, ffn2_gate, ffn2_up, ffn2_down):
    bsz, seq, _ = x.shape
    depth = w_in.shape[0]
    n = bsz * seq
    x2 = x.reshape(n, D_MODEL)
    assert TOKEN_TILE == KEY_CHUNK and seq % KEY_CHUNK == 0
    seg_mean = _segment_mean_matrix(A_WIDTH)
    grp_per_slab = S5_SLAB // S5_STATE
    cast = lambda w: w.astype(bf16)
    row = lambda g: g[:, None, :]
    ffn1 = (row(ffn1_norm), cast(ffn1_gate), cast(ffn1_up), cast(ffn1_down))
    ffn2 = (row(ffn2_norm), cast(ffn2_gate), cast(ffn2_up), cast(ffn2_down))
    w_packed, w_t, w_gates = _pack_w_in(w_in)
    head_gains = (_tile_gain(a_q_norm, A_HEADS), _tile_gain(a_k_norm, 2),
                  _tile_gain(c_q_norm, C_HEADS_PER_GROUP), _tile_gain(c_k_norm, C_HEADS_PER_GROUP))
    glu = (cast(s5_glu_a), cast(s5_glu_b))
    merge_w = (row(mix_norm), w_gates, cast(w_branch_a), cast(w_branch_b), cast(w_branch_c), cast(w_out))
    r3 = lambda a: a.reshape(bsz, seq, a.shape[-1])
    b_t = lambda b: jnp.tile(jnp.transpose(b, (2, 0, 1)).reshape(S5_GROUP, S5_LANES), (grp_per_slab, 1))
    c_t = lambda c: jnp.tile(jnp.transpose(c, (0, 2, 1)).reshape(S5_LANES, S5_GROUP), (1, grp_per_slab))
    for l in range(depth):
        x2 = _ffn(x2, l, *ffn1)
        aq, akk, vvt, iq, ikk, iwt, su, c0, c1, c2 = _in_proj(x2, l, row(mix_norm), w_packed, w_t, seg_mean,
                                                               *head_gains)
        vvt = vvt.reshape(bsz, seq // KEY_CHUNK, LANES, KEY_CHUNK)
        ya = _dsa(r3(aq), r3(iq), iwt, r3(akk), vvt, r3(ikk)).reshape(n, A_WIDTH)
        yb = _s5(r3(su), l, s5_lam_re[l].reshape(1, -1), s5_lam_im[l].reshape(1, -1),
                 jnp.repeat(s5_log_dt[l], S5_STATE).reshape(1, -1),
                 b_t(s5_b_re[l]), b_t(s5_b_im[l]), c_t(s5_c_re[l]), c_t(s5_c_im[l]),
                 s5_d[l].reshape(1, -1), *glu).reshape(n, S5_WIDTH)
        os_, lses = [], []
        for cg, (_, dilation) in zip((c0, c1, c2), DIL_PAIRS):
            o, lse = _dilated(cg.reshape(bsz, seq // dilation, dilation * C_WIDTH), dilation)
            os_.append(o)
            lses.append(lse)
        x2 = _merge(x2, l, ya, yb, os_, lses, *merge_w)
        x2 = _ffn(x2, l, *ffn2)
    return x2.reshape(bsz, seq, D_MODEL)
```

```python
import functools
import math

import jax
import jax.numpy as jnp
from jax import lax
from jax.experimental import pallas as pl
from jax.experimental.pallas import tpu as pltpu

f32 = jnp.float32
bf16 = jnp.bfloat16
i32 = jnp.int32

D_MODEL = 1024
D_FF = 2816
HEAD_DIM = 64
RMS_EPS = 1e-6
A_HEADS = 8
A_WIDTH = A_HEADS * HEAD_DIM
IDX_HEADS = 4
IDX_DIM = 64
TOPK_MAX = 256
S5_WIDTH = 512
S5_GROUP = 16
S5_GROUPS = 32
S5_STATE = 64
S5_LANES = S5_GROUPS * S5_STATE
DIL_PAIRS = ((128, 1), (512, 4), (2048, 16))
C_HEADS_PER_GROUP = 4
C_OUT = C_HEADS_PER_GROUP * HEAD_DIM
C_WIDTH = 3 * C_OUT
IN_SIZES = (A_WIDTH, HEAD_DIM, HEAD_DIM, IDX_HEADS * IDX_DIM, IDX_DIM, IDX_HEADS,
            S5_WIDTH, C_WIDTH, C_WIDTH, C_WIDTH, 3 * D_MODEL)

LANES = 128
SUBLANES = 8
VMEM_LIMIT_BYTES = 56 * 1024 * 1024

TOKEN_TILE = 512
FF_CHUNK = 256
Q_TILE = 256
ATT_HEADS = 4
KEY_CHUNK = 512
COUNT_ROWS = 64
KEY_BITS = 32
PLANE_ROWS = KEY_BITS * 8
RANK_ROWS = 256
LOOP_UNITS = (4, 2, 1)
S5_TIME_TILE = 512
S5_SLAB = 512
S5_SCAN_LANES = 512
DIL_SPAN = 128
DIL_ROWS = 512

NEG_INF = float("-inf")
LOG2_E = math.log2(math.e)
INT_MIN = -2 ** 31
NEG_INF_KEY = INT_MIN + 0x7FFFFF

_NT = (((1,), (1,)), ((), ()))


def _params(*sem):
    return pltpu.CompilerParams(dimension_semantics=sem, vmem_limit_bytes=VMEM_LIMIT_BYTES)


def _resident(shape, layer=None):
    nd = len(shape)
    if layer is None:
        return pl.BlockSpec(shape, lambda *_: (0,) * nd, pipeline_mode=pl.Buffered(1))
    return pl.BlockSpec((None,) + tuple(shape), lambda *_: (layer,) + (0,) * nd, pipeline_mode=pl.Buffered(1))


def _dot(a, b):
    return jnp.dot(a, b, preferred_element_type=f32)


def _dot_nt(a, b):
    return lax.dot_general(a, b, _NT, preferred_element_type=f32)


def _rmsnorm_rows(x, gain):
    return x * lax.rsqrt(jnp.mean(x * x, axis=-1, keepdims=True) + RMS_EPS) * gain


def _head_rmsnorm(z, gain_tiled, seg_mean):
    ms = _dot((z * z).astype(bf16), seg_mean)
    return z * lax.rsqrt(ms + RMS_EPS) * gain_tiled


def _ffn_body(x_ref, g_ref, wg_ref, wu_ref, wd_ref, o_ref, acc_ref):
    x = x_ref[...]
    h = _rmsnorm_rows(x, g_ref[...]).astype(bf16)
    for c in range(D_FF // FF_CHUNK):
        sl = slice(c * FF_CHUNK, (c + 1) * FF_CHUNK)
        gate = _dot(h, wg_ref[:, sl])
        up = _dot(h, wu_ref[:, sl])
        act = (gate * jax.nn.sigmoid(gate) * up).astype(bf16)
        contrib = _dot(act, wd_ref[sl, :])
        if c == 0:
            acc_ref[...] = contrib
        else:
            acc_ref[...] += contrib
    o_ref[...] = x + 0.5 * acc_ref[...]


def _ffn(x2, layer, gain, w_gate, w_up, w_down):
    n = x2.shape[0]
    tm = min(TOKEN_TILE, n)
    tile = pl.BlockSpec((tm, D_MODEL), lambda i: (i, 0))
    return pl.pallas_call(
        _ffn_body,
        grid=(n // tm,),
        in_specs=[tile, _resident((1, D_MODEL), layer), _resident((D_MODEL, D_FF), layer),
                  _resident((D_MODEL, D_FF), layer), _resident((D_FF, D_MODEL), layer)],
        out_specs=tile,
        out_shape=jax.ShapeDtypeStruct((n, D_MODEL), f32),
        scratch_shapes=[pltpu.VMEM((tm, D_MODEL), f32)],
        compiler_params=_params("parallel"),
        name="ffn",
    )(x2, gain, w_gate, w_up, w_down)


_P_AQ = 0
_P_AKK = _P_AQ + A_WIDTH
_P_IQ = _P_AKK + LANES
_P_IKK = _P_IQ + IDX_HEADS * IDX_DIM
_P_SU = _P_IKK + LANES
_P_C = _P_SU + S5_WIDTH
_P_TOTAL = _P_C + 3 * C_WIDTH
_T_ROWS = LANES + SUBLANES


def _interleave_store(dst_ref, value, dilation, perm_ref):
    rows, width = value.shape
    for j in range(width // LANES):
        perm_ref[j] = value[:, j * LANES:(j + 1) * LANES]
    for r in range(dilation):
        for j in range(width // LANES):
            piece = perm_ref[j, pl.ds(r, rows // dilation, stride=dilation), :]
            dst_ref[:, r * width + j * LANES:r * width + (j + 1) * LANES] = piece.astype(dst_ref.dtype)


def _deinterleave_load(src_ref, dilation, perm_ref):
    width = src_ref.shape[1] // dilation
    rows = src_ref.shape[0] * dilation
    for r in range(dilation):
        for j in range(width // LANES):
            piece = src_ref[:, r * width + j * LANES:r * width + (j + 1) * LANES]
            perm_ref[j, pl.ds(r, rows // dilation, stride=dilation), :] = piece.astype(f32)
    return jnp.concatenate([perm_ref[j] for j in range(width // LANES)], axis=1)


def _in_proj_body(x_ref, g_ref, w_ref, wt_ref, seg_ref, aqg_ref, akg_ref, cqg_ref, ckg_ref,
                  aq_ref, akk_ref, vvt_ref, iq_ref, ikk_ref, iwt_ref, su_ref, c0_ref, c1_ref, c2_ref, perm_ref):
    h = _rmsnorm_rows(x_ref[...], g_ref[...]).astype(bf16)
    full_c = _dot(h, w_ref[:, _P_C:])
    full_a = _dot(h, w_ref[:, :_P_C])

    def proj(start, width):
        if start >= _P_C:
            return full_c[:, start - _P_C:start - _P_C + width]
        return full_a[:, start:start + width]

    seg = seg_ref[...]
    for c_ref, (_, dilation), g in zip((c0_ref, c1_ref, c2_ref), DIL_PAIRS, range(len(DIL_PAIRS))):
        base = _P_C + g * C_WIDTH
        cq = _head_rmsnorm(proj(base, C_OUT), cqg_ref[...], seg[:C_OUT, :C_OUT])
        ck = _head_rmsnorm(proj(base + C_OUT, C_OUT), ckg_ref[...], seg[:C_OUT, :C_OUT])
        qkv = (cq, ck, proj(base + 2 * C_OUT, C_OUT))
        if dilation == 1:
            for i, part in enumerate(qkv):
                c_ref[:, i * C_OUT:(i + 1) * C_OUT] = part.astype(bf16)
        else:
            _interleave_store(c_ref, jnp.concatenate(qkv, axis=1), dilation, perm_ref)
    aq = _head_rmsnorm(proj(_P_AQ, A_WIDTH), aqg_ref[...], seg) * (HEAD_DIM ** -0.5)
    aq_ref[...] = aq.astype(bf16)
    akk_ref[...] = _head_rmsnorm(proj(_P_AKK, LANES), akg_ref[...], seg[:LANES, :LANES]).astype(bf16)
    iq_ref[...] = proj(_P_IQ, IDX_HEADS * IDX_DIM).astype(bf16)
    ikk_ref[...] = proj(_P_IKK, LANES).astype(bf16)
    su_ref[...] = proj(_P_SU, S5_WIDTH)
    vvt_ref[0] = _dot_nt(wt_ref[:LANES, :], h).astype(bf16)
    iwt_ref[0] = _dot_nt(wt_ref[LANES:, :], h)


def _in_proj(x2, layer, gain, w_packed, w_t, seg_mean, aq_gain, ak_gain, cq_gain, ck_gain):
    n = x2.shape[0]
    tm = min(TOKEN_TILE, n)
    rows = lambda w: pl.BlockSpec((tm, w), lambda i: (i, 0))
    cols = lambda r: pl.BlockSpec((1, r, tm), lambda i: (i, 0, 0))
    outs = [(rows(A_WIDTH), (n, A_WIDTH), bf16), (rows(LANES), (n, LANES), bf16),
            (cols(LANES), (n // tm, LANES, tm), bf16), (rows(IDX_HEADS * IDX_DIM), (n, IDX_HEADS * IDX_DIM), bf16),
            (rows(LANES), (n, LANES), bf16), (cols(SUBLANES), (n // tm, SUBLANES, tm), f32),
            (rows(S5_WIDTH), (n, S5_WIDTH), f32)]
    for _, d in DIL_PAIRS:
        outs.append((pl.BlockSpec((tm // d, d * C_WIDTH), lambda i: (i, 0)), (n // d, d * C_WIDTH), bf16))
    return pl.pallas_call(
        _in_proj_body,
        grid=(n // tm,),
        in_specs=[rows(D_MODEL), _resident((1, D_MODEL), layer),
                  _resident((D_MODEL, _P_TOTAL), layer), _resident((_T_ROWS, D_MODEL), layer),
                  _resident((A_WIDTH, A_WIDTH)),
                  _resident((1, A_WIDTH), layer), _resident((1, LANES), layer),
                  _resident((1, C_OUT), layer), _resident((1, C_OUT), layer)],
        out_specs=[o[0] for o in outs],
        out_shape=[jax.ShapeDtypeStruct(o[1], o[2]) for o in outs],
        scratch_shapes=[pltpu.VMEM((C_WIDTH // LANES, tm, LANES), f32)],
        compiler_params=_params("parallel"),
        name="in_proj",
    )(x2, gain, w_packed, w_t, seg_mean, aq_gain, ak_gain, cq_gain, ck_gain)


def _float_order_key(x):
    bits = pltpu.bitcast(x, i32)
    return jnp.where(bits < 0, bits ^ 0x7FFFFFFF, bits)


def _fold_rows(x, op, keep=SUBLANES):
    return op(x.reshape(x.shape[0] // keep, keep, x.shape[1]), axis=0)


def _bit_planes(words):
    a = [words[t * SUBLANES:(t + 1) * SUBLANES] for t in range(KEY_BITS)]
    for j, m in ((16, 0x0000FFFF), (8, 0x00FF00FF), (4, 0x0F0F0F0F), (2, 0x33333333), (1, 0x55555555)):
        shift = jnp.full(a[0].shape, j, i32)
        for k in range(KEY_BITS):
            if k & j == 0:
                t = (a[k] ^ lax.shift_right_logical(a[k + j], shift)) & m
                a[k] = a[k] ^ t
                a[k + j] = a[k + j] ^ lax.shift_left(t, shift)
    return a


def _dsa_body(topk, aq_ref, iq_ref, iwt_ref, kk_ref, vvt_ref, ikk_ref, rank_ref, o_ref,
              key_ref, plane_ref, cand_ref, bias_ref, s_ref, m_ref, l_ref, acc_ref):
    qb = pl.program_id(1)
    n_chunks = lax.shift_right_logical(qb * Q_TILE, int(math.log2(KEY_CHUNK))) + 1
    q_pos = qb * Q_TILE + lax.broadcasted_iota(i32, (1, Q_TILE), 1)
    key_iota = lax.broadcasted_iota(i32, (KEY_CHUNK, Q_TILE), 0)
    low_half = lax.broadcasted_iota(i32, (Q_TILE, LANES), 1) < HEAD_DIM
    zero_b = jnp.zeros((), bf16)

    def stack_heads(x, n_heads):
        blocks = []
        for h in range(n_heads):
            pair = x[:, (h // 2) * LANES:(h // 2 + 1) * LANES]
            blocks.append(jnp.where(low_half if h % 2 == 0 else ~low_half, pair, zero_b))
        return jnp.concatenate(blocks, axis=0)

    @pl.when((pl.program_id(0) == 0) & (qb == 0))
    def _():
        plane_ref[...] = jnp.zeros(plane_ref.shape, i32)

    iq_all = stack_heads(iq_ref[0], IDX_HEADS)
    w_t = iwt_ref[0] * ((IDX_DIM ** -0.5) * (IDX_HEADS ** -0.5))

    def key_rows(c, part, rows):
        return pl.ds(pl.multiple_of(c * KEY_CHUNK + part * rows, rows), rows)

    def chunk_loop(body, carry, end=n_chunks):
        start = 0
        for unit in LOOP_UNITS:
            n_iter = lax.shift_right_logical(end - start, int(math.log2(unit)))

            def multi(i, cr, unit=unit, start=start):
                for j in range(unit):
                    cr = body(start + i * unit + j, cr)
                return cr

            carry = lax.fori_loop(0, n_iter, multi, carry)
            start = start + n_iter * unit
        return carry

    def score_chunk(c, _, diagonal=False):
        logits = _dot_nt(ikk_ref[0, key_rows(c, 0, KEY_CHUNK), :], iq_all)
        score = None
        for h in range(IDX_HEADS):
            term = jnp.maximum(logits[:, h * Q_TILE:(h + 1) * Q_TILE], 0.0) * w_t[h:h + 1, :]
            score = term if score is None else score + term
        score = jnp.where(score == 0.0, 0.0, score)
        if diagonal:
            score = jnp.where(c * KEY_CHUNK + key_iota <= q_pos, score, NEG_INF)
        key = _float_order_key(score)
        key_ref[c] = key
        for part in range(KEY_CHUNK // PLANE_ROWS):
            planes = _bit_planes(key[part * PLANE_ROWS:(part + 1) * PLANE_ROWS] ^ INT_MIN)
            for i in range(KEY_BITS):
                plane_ref[i, c * (KEY_CHUNK // PLANE_ROWS) + part] = planes[i]
        return 0

    chunk_loop(score_chunk, 0, end=n_chunks - 1)
    score_chunk(n_chunks - 1, 0, diagonal=True)

    n_planes = n_chunks * (KEY_CHUNK // PLANE_ROWS)
    block_id = lax.broadcasted_iota(i32, cand_ref.shape, 0)
    cand_ref[...] = jnp.where(block_id < n_planes, -1, 0)

    def bit_step(i, carry):
        prefix, n_above = carry
        cand = cand_ref[...]
        ones = cand & plane_ref[i]
        cnt = jnp.sum(jnp.sum(lax.population_count(ones), axis=0).astype(f32), axis=0, keepdims=True)
        accept = n_above + cnt >= topk
        cand_ref[...] = jnp.where(accept, ones, cand ^ ones)
        prefix = jnp.where(accept, prefix | lax.shift_left(jnp.int32(1), KEY_BITS - 1 - i), prefix)
        return prefix, jnp.where(accept, n_above, n_above + cnt)

    prefix, n_gt = lax.fori_loop(0, KEY_BITS, bit_step,
                                 (jnp.zeros((1, Q_TILE), i32), jnp.zeros((1, Q_TILE), f32)))
    thr = prefix ^ INT_MIN
    need = jnp.where(thr <= NEG_INF_KEY, 0.0, topk - n_gt)
    rank_mat = rank_ref[...]

    top_rows = lax.broadcasted_iota(i32, (LANES, Q_TILE), 0) < HEAD_DIM
    for group in range(A_HEADS // ATT_HEADS):
        width = ATT_HEADS * HEAD_DIM
        q_all = stack_heads(aq_ref[0, :, group * width:(group + 1) * width], ATT_HEADS)

        m_ref[...] = jnp.full(m_ref.shape, NEG_INF, f32)

        def logits_chunk(c, seen, q_all=q_all, first_group=group == 0):
            for part in range(KEY_CHUNK // RANK_ROWS):
                rows = slice(part * RANK_ROWS, (part + 1) * RANK_ROWS)
                if first_group:
                    k = key_ref[c, rows, :]
                    eq = k == thr
                    tie = jnp.where(eq, 1.0, 0.0)
                    rank = _dot(rank_mat, tie.astype(bf16)) + seen
                    seen = seen + jnp.sum(_fold_rows(tie, jnp.sum, COUNT_ROWS), axis=0, keepdims=True)
                    order = jnp.where(eq, rank, jnp.where(k > thr, -1.0, 2.0 ** 30))
                    bias = jnp.where(order < need, 0.0, NEG_INF)
                    bias_ref[c, rows, :] = bias
                else:
                    bias = bias_ref[c, rows, :]
                s = _dot_nt(kk_ref[0, key_rows(c, part, RANK_ROWS), :], q_all)
                for h in range(ATT_HEADS):
                    cols = slice(h * Q_TILE, (h + 1) * Q_TILE)
                    sh = (s[:, cols] + bias) * LOG2_E
                    s_ref[c, rows, cols] = sh
                    m_ref[:, cols] = jnp.maximum(m_ref[:, cols], _fold_rows(sh, jnp.max))
            return seen

        chunk_loop(logits_chunk, jnp.zeros((1, Q_TILE), f32))
        m = jnp.max(m_ref[...], axis=0, keepdims=True)

        l_ref[...] = jnp.zeros(l_ref.shape, f32)
        acc_ref[...] = jnp.zeros(acc_ref.shape, f32)

        def softmax_chunk(c, _, m=m):
            for part in range(KEY_CHUNK // RANK_ROWS):
                rows = slice(part * RANK_ROWS, (part + 1) * RANK_ROWS)
                p = jnp.exp2(s_ref[c, rows, :] - m)
                l_ref[...] += _fold_rows(p, jnp.sum)
                acc_ref[...] += _dot(vvt_ref[0, c, :, rows], p.astype(bf16))
            return 0

        chunk_loop(softmax_chunk, 0)
        out_t = acc_ref[...] / jnp.sum(l_ref[...], axis=0, keepdims=True)
        for j in range(ATT_HEADS // 2):
            even = out_t[:, (2 * j) * Q_TILE:(2 * j + 1) * Q_TILE]
            odd = out_t[:, (2 * j + 1) * Q_TILE:(2 * j + 2) * Q_TILE]
            pair = group * (ATT_HEADS // 2) + j
            o_ref[0, :, pair * LANES:(pair + 1) * LANES] = jnp.where(top_rows, even, odd).T.astype(bf16)


def _tie_rank_matrix():
    i = jnp.arange(RANK_ROWS)
    return (i[None, :] < i[:, None]).astype(bf16)


def _dsa(aq, iq, iwt, akk, vvt, ikk):
    bsz, seq, _ = aq.shape
    topk = min(TOPK_MAX, seq // 4)
    n_kc = seq // KEY_CHUNK
    q_per_chunk = KEY_CHUNK // Q_TILE

    def q_spec(width):
        return pl.BlockSpec((1, Q_TILE, width), lambda b, q: (b, q, 0))

    kv_spec = pl.BlockSpec((1, seq, LANES), lambda b, q: (b, 0, 0))
    return pl.pallas_call(
        functools.partial(_dsa_body, topk),
        grid=(bsz, seq // Q_TILE),
        in_specs=[q_spec(A_WIDTH), q_spec(IDX_HEADS * IDX_DIM),
                  pl.BlockSpec((1, SUBLANES, Q_TILE), lambda b, q: (b * n_kc + q // q_per_chunk, 0, q % q_per_chunk)),
                  kv_spec, pl.BlockSpec((1, n_kc, LANES, KEY_CHUNK), lambda b, q: (b, 0, 0, 0)), kv_spec,
                  _resident((RANK_ROWS, RANK_ROWS))],
        out_specs=q_spec(A_WIDTH),
        out_shape=jax.ShapeDtypeStruct((bsz, seq, A_WIDTH), bf16),
        scratch_shapes=[pltpu.VMEM((n_kc, KEY_CHUNK, Q_TILE), i32),
                        pltpu.VMEM((KEY_BITS, seq // PLANE_ROWS, SUBLANES, Q_TILE), i32),
                        pltpu.VMEM((seq // PLANE_ROWS, SUBLANES, Q_TILE), i32),
                        pltpu.VMEM((n_kc, KEY_CHUNK, Q_TILE), f32),
                        pltpu.VMEM((n_kc, KEY_CHUNK, ATT_HEADS * Q_TILE), f32),
                        pltpu.VMEM((SUBLANES, ATT_HEADS * Q_TILE), f32),
                        pltpu.VMEM((SUBLANES, ATT_HEADS * Q_TILE), f32),
                        pltpu.VMEM((LANES, ATT_HEADS * Q_TILE), f32)],
        compiler_params=_params("arbitrary", "arbitrary"),
        name="dsa",
    )(aq, iq, iwt, akk, vvt, ikk, _tie_rank_matrix())


def _cmul(ar, ai, br, bi):
    return ar * br - ai * bi, ar * bi + ai * br


def _s5_prepare(lam_re_ref, lam_im_ref, logdt_ref, bre_ref, bim_ref, cre_ref, cim_ref,
                bmat_re, bmat_im, cmat_re, cmat_im, shift_re, shift_im, pow_re, pow_im):
    lr = jnp.minimum(lam_re_ref[...], -1e-4)
    li = lam_im_ref[...]
    dt = jnp.exp(logdt_ref[...])
    mag = jnp.exp(lr * dt)
    l1r, l1i = mag * jnp.cos(li * dt), mag * jnp.sin(li * dt)
    den = lr * lr + li * li
    cr = ((l1r - 1.0) * lr + l1i * li) / den
    ci = (l1i * lr - (l1r - 1.0) * li) / den

    n_slab = S5_LANES // S5_SLAB
    row_g = lax.broadcasted_iota(i32, (LANES, S5_SLAB), 0) // S5_GROUP
    lane_g = lax.broadcasted_iota(i32, (LANES, S5_SLAB), 1) // S5_STATE
    diag_b = row_g == lane_g
    row_g2 = lax.broadcasted_iota(i32, (S5_SLAB, LANES), 0) // S5_STATE
    lane_g2 = lax.broadcasted_iota(i32, (S5_SLAB, LANES), 1) // S5_GROUP
    diag_c = row_g2 == lane_g2
    for j in range(n_slab):
        sl = slice(j * S5_SLAB, (j + 1) * S5_SLAB)
        bbr, bbi = _cmul(cr[:, sl], ci[:, sl], bre_ref[:, sl], bim_ref[:, sl])
        bmat_re[j] = jnp.where(diag_b, bbr, 0.0).astype(bf16)
        bmat_im[j] = jnp.where(diag_b, bbi, 0.0).astype(bf16)
        cmat_re[j] = jnp.where(diag_c, cre_ref[sl, :], 0.0).astype(bf16)
        cmat_im[j] = jnp.where(diag_c, -cim_ref[sl, :], 0.0).astype(bf16)

    l2r, l2i = _cmul(l1r, l1i, l1r, l1i)
    l3r, l3i = _cmul(l2r, l2i, l1r, l1i)
    l4r, l4i = _cmul(l2r, l2i, l2r, l2i)
    l5r, l5i = _cmul(l4r, l4i, l1r, l1i)
    l6r, l6i = _cmul(l4r, l4i, l2r, l2i)
    l7r, l7i = _cmul(l4r, l4i, l3r, l3i)
    l8r, l8i = _cmul(l4r, l4i, l4r, l4i)
    row = lax.broadcasted_iota(i32, (SUBLANES, S5_LANES), 0)
    for idx, (k, pr, pi) in enumerate(((1, l1r, l1i), (2, l2r, l2i), (4, l4r, l4i))):
        shift_re[idx] = jnp.where(row >= k, pr, 0.0)
        shift_im[idx] = jnp.where(row >= k, pi, 0.0)
    pr_acc = jnp.zeros((SUBLANES, S5_LANES), f32)
    pi_acc = jnp.zeros((SUBLANES, S5_LANES), f32)
    powers = ((l1r, l1i), (l2r, l2i), (l3r, l3i), (l4r, l4i), (l5r, l5i), (l6r, l6i), (l7r, l7i), (l8r, l8i))
    for i, (pr, pi) in enumerate(powers):
        pr_acc = jnp.where(row == i, pr, pr_acc)
        pi_acc = jnp.where(row == i, pi, pi_acc)
    pow_re[...] = pr_acc
    pow_im[...] = pi_acc


def _s5_body(u_ref, lam_re_ref, lam_im_ref, logdt_ref, bre_ref, bim_ref, cre_ref, cim_ref, d_ref,
             glu_a_ref, glu_b_ref, o_ref,
             bmat_re, bmat_im, cmat_re, cmat_im, shift_re, shift_im, pow_re, pow_im,
             carry_re, carry_im, st_re, st_im):
    t_idx = pl.program_id(1)

    @pl.when((pl.program_id(0) == 0) & (t_idx == 0))
    def _():
        _s5_prepare(lam_re_ref, lam_im_ref, logdt_ref, bre_ref, bim_ref, cre_ref, cim_ref,
                    bmat_re, bmat_im, cmat_re, cmat_im, shift_re, shift_im, pow_re, pow_im)

    @pl.when(t_idx == 0)
    def _():
        carry_re[...] = jnp.zeros(carry_re.shape, f32)
        carry_im[...] = jnp.zeros(carry_im.shape, f32)

    u = u_ref[0]
    ub = u.astype(bf16)
    n_slab = S5_LANES // S5_SLAB
    n_tiles = u.shape[0] // SUBLANES
    ys = []
    for j in range(n_slab):
        sl = slice(j * S5_SLAB, (j + 1) * S5_SLAB)
        uj = ub[:, j * LANES:(j + 1) * LANES]
        st_re[:, sl] = _dot(uj, bmat_re[j])
        st_im[:, sl] = _dot(uj, bmat_im[j])

        coef = [(shift_re[i, :, sl], shift_im[i, :, sl]) for i in range(3)]
        pwr, pwi = pow_re[:, sl], pow_im[:, sl]
        c_re, c_im = carry_re[:, sl], carry_im[:, sl]
        for n in range(n_tiles):
            rows = slice(n * SUBLANES, (n + 1) * SUBLANES)
            xr, xi = st_re[rows, sl], st_im[rows, sl]
            for i, k in enumerate((1, 2, 4)):
                dr, di = _cmul(coef[i][0], coef[i][1], pltpu.roll(xr, k, 0), pltpu.roll(xi, k, 0))
                xr, xi = xr + dr, xi + di
            dr, di = _cmul(pwr, pwi, c_re, c_im)
            xr, xi = xr + dr, xi + di
            st_re[rows, sl] = xr
            st_im[rows, sl] = xi
            c_re, c_im = xr[SUBLANES - 1:SUBLANES, :], xi[SUBLANES - 1:SUBLANES, :]
        carry_re[:, sl] = c_re
        carry_im[:, sl] = c_im
        ys.append(_dot(st_re[:, sl].astype(bf16), cmat_re[j]) + _dot(st_im[:, sl].astype(bf16), cmat_im[j]))
    y = jnp.concatenate(ys, axis=1) + d_ref[...] * u
    g = jax.nn.gelu(y).astype(bf16)
    o_ref[0] = (_dot(g, glu_a_ref[...]) * jax.nn.sigmoid(_dot(g, glu_b_ref[...]))).astype(bf16)


def _s5(su, layer, lam_re, lam_im, logdt, b_re, b_im, c_re, c_im, d_skip, glu_a, glu_b):
    bsz, seq, _ = su.shape
    tt = min(S5_TIME_TILE, seq)
    n_slab = S5_LANES // S5_SLAB
    tile = pl.BlockSpec((1, tt, S5_WIDTH), lambda b, t: (b, t, 0))
    return pl.pallas_call(
        _s5_body,
        grid=(bsz, seq // tt),
        in_specs=[tile, _resident((1, S5_LANES)), _resident((1, S5_LANES)), _resident((1, S5_LANES)),
                  _resident((LANES, S5_LANES)), _resident((LANES, S5_LANES)),
                  _resident((S5_LANES, LANES)), _resident((S5_LANES, LANES)), _resident((1, S5_WIDTH)),
                  _resident((S5_WIDTH, S5_WIDTH), layer), _resident((S5_WIDTH, S5_WIDTH), layer)],
        out_specs=tile,
        out_shape=jax.ShapeDtypeStruct((bsz, seq, S5_WIDTH), bf16),
        scratch_shapes=[pltpu.VMEM((n_slab, LANES, S5_SLAB), bf16), pltpu.VMEM((n_slab, LANES, S5_SLAB), bf16),
                        pltpu.VMEM((n_slab, S5_SLAB, LANES), bf16), pltpu.VMEM((n_slab, S5_SLAB, LANES), bf16),
                        pltpu.VMEM((3, SUBLANES, S5_LANES), f32), pltpu.VMEM((3, SUBLANES, S5_LANES), f32),
                        pltpu.VMEM((SUBLANES, S5_LANES), f32), pltpu.VMEM((SUBLANES, S5_LANES), f32),
                        pltpu.VMEM((1, S5_LANES), f32), pltpu.VMEM((1, S5_LANES), f32),
                        pltpu.VMEM((tt, S5_LANES), f32), pltpu.VMEM((tt, S5_LANES), f32)],
        compiler_params=_params("arbitrary", "arbitrary"),
        name="s5",
    )(su, lam_re, lam_im, logdt, b_re, b_im, c_re, c_im, d_skip, glu_a, glu_b)


def _dilated_body(cur_ref, prev_ref, o_ref, lse_ref):
    n = pl.program_id(2)
    rows = cur_ref.shape[1]
    qi = lax.broadcasted_iota(i32, (DIL_SPAN, 2 * DIL_SPAN), 0)
    kj = lax.broadcasted_iota(i32, (DIL_SPAN, 2 * DIL_SPAN), 1)
    band = (kj >= qi) & (kj <= qi + DIL_SPAN)
    band_first = band & ((kj >= DIL_SPAN) | (n > 0))
    low_half = lax.broadcasted_iota(i32, (DIL_SPAN, LANES), 1) < HEAD_DIM
    zero_b = jnp.zeros((), bf16)
    for res in range(cur_ref.shape[2] // C_WIDTH):
        q0, k0, v0, o0 = res * C_WIDTH, res * C_WIDTH + C_OUT, res * C_WIDTH + 2 * C_OUT, res * C_OUT
        for blk in range(rows // DIL_SPAN):
            r0 = blk * DIL_SPAN
            q = cur_ref[0, r0:r0 + DIL_SPAN, q0:q0 + C_OUT]
            if blk == 0:
                k = jnp.concatenate([prev_ref[0, :, k0:k0 + C_OUT], cur_ref[0, 0:DIL_SPAN, k0:k0 + C_OUT]], axis=0)
                v = jnp.concatenate([prev_ref[0, :, v0:v0 + C_OUT], cur_ref[0, 0:DIL_SPAN, v0:v0 + C_OUT]], axis=0)
            else:
                k = cur_ref[0, r0 - DIL_SPAN:r0 + DIL_SPAN, k0:k0 + C_OUT]
                v = cur_ref[0, r0 - DIL_SPAN:r0 + DIL_SPAN, v0:v0 + C_OUT]
            mask = band_first if blk == 0 else band
            for j in range(C_OUT // LANES):
                ls = slice(j * LANES, (j + 1) * LANES)
                out_ls = slice(o0 + j * LANES, o0 + (j + 1) * LANES)
                outs, lses = [], []
                for half in (low_half, ~low_half):
                    s = _dot_nt(jnp.where(half, q[:, ls], zero_b), k[:, ls]) * (HEAD_DIM ** -0.5)
                    s = jnp.where(mask, s, NEG_INF)
                    m = jnp.max(s, axis=1, keepdims=True)
                    p = jnp.exp(s - m)
                    den = jnp.sum(p, axis=1, keepdims=True)
                    outs.append(_dot(p.astype(bf16), v[:, ls]) / den)
                    lses.append(jnp.broadcast_to(m + jnp.log(den), (DIL_SPAN, LANES)))
                o_ref[0, r0:r0 + DIL_SPAN, out_ls] = jnp.where(low_half, outs[0], outs[1]).astype(bf16)
                lse_ref[0, r0:r0 + DIL_SPAN, out_ls] = jnp.where(low_half, lses[0], lses[1])


def _dilated(view, dilation):
    bsz, sub, _ = view.shape
    rows = min(DIL_ROWS, sub)
    n_res = min(DIL_ROWS // rows, dilation)
    blocks_per_step = rows // DIL_SPAN
    cur = pl.BlockSpec((1, rows, n_res * C_WIDTH), lambda b, r, n: (b, n, r))
    prev = pl.BlockSpec((1, DIL_SPAN, n_res * C_WIDTH),
                        lambda b, r, n: (b, jnp.maximum(n * blocks_per_step - 1, 0), r))
    out = pl.BlockSpec((1, rows, n_res * C_OUT), lambda b, r, n: (b, n, r))
    o, lse = pl.pallas_call(
        _dilated_body,
        grid=(bsz, dilation // n_res, sub // rows),
        in_specs=[cur, prev],
        out_specs=[out, out],
        out_shape=[jax.ShapeDtypeStruct((bsz, sub, dilation * C_OUT), bf16),
                   jax.ShapeDtypeStruct((bsz, sub, dilation * C_OUT), f32)],
        compiler_params=_params("parallel", "parallel", "arbitrary"),
        name=f"dilated_{dilation}",
    )(view, view)
    return o.reshape(bsz * sub, dilation * C_OUT), lse.reshape(bsz * sub, dilation * C_OUT)


def _merge_body(x_ref, ya_ref, yb_ref, o0_ref, o1_ref, o2_ref, l0_ref, l1_ref, l2_ref,
                g_ref, wgate_ref, wa_ref, wb_ref, wc_ref, wout_ref, out_ref, *perm_refs):
    x = x_ref[...]
    h = _rmsnorm_rows(x, g_ref[...]).astype(bf16)
    (o0, l0), (o1, l1), (o2, l2) = [
        (o_ref[...].astype(f32), l_ref[...]) if d == 1 else
        (_deinterleave_load(o_ref, d, po_ref), _deinterleave_load(l_ref, d, pl_ref))
        for (o_ref, l_ref, po_ref, pl_ref), (_, d) in zip(
            ((o0_ref, l0_ref, None, None), (o1_ref, l1_ref, perm_refs[0], perm_refs[1]),
             (o2_ref, l2_ref, perm_refs[2], perm_refs[3])), DIL_PAIRS)]
    mx = jnp.maximum(jnp.maximum(l0, l1), l2)
    e0, e1, e2 = jnp.exp(l0 - mx), jnp.exp(l1 - mx), jnp.exp(l2 - mx)
    yc = e0 * o0 + e1 * o1 + e2 * o2
    yc = (yc / (e0 + e1 + e2)).astype(bf16)

    def gate(i):
        return jax.nn.sigmoid(_dot(h, wgate_ref[:, i * D_MODEL:(i + 1) * D_MODEL]))

    merged = gate(0) * _dot(ya_ref[...], wa_ref[...])
    merged = merged + gate(1) * _dot(yb_ref[...], wb_ref[...])
    merged = merged + gate(2) * _dot(yc, wc_ref[...])
    out_ref[...] = x + _dot(merged.astype(bf16), wout_ref[...])


def _merge(x2, layer, ya, yb, os_, lses, gain, w_gates, w_a, w_b, w_c, w_out):
    n = x2.shape[0]
    tm = min(TOKEN_TILE, n)

    def tile(width):
        return pl.BlockSpec((tm, width), lambda i: (i, 0))

    return pl.pallas_call(
        _merge_body,
        grid=(n // tm,),
        in_specs=[tile(D_MODEL), tile(A_WIDTH), tile(S5_WIDTH)] + 2 * [
            pl.BlockSpec((tm // d, d * C_OUT), lambda i: (i, 0)) for _, d in DIL_PAIRS] +
                 [_resident((1, D_MODEL), layer), _resident((D_MODEL, 3 * D_MODEL), layer),
                  _resident((A_WIDTH, D_MODEL), layer), _resident((S5_WIDTH, D_MODEL), layer),
                  _resident((C_OUT, D_MODEL), layer), _resident((D_MODEL, D_MODEL), layer)],
        out_specs=tile(D_MODEL),
        out_shape=jax.ShapeDtypeStruct((n, D_MODEL), f32),
        scratch_shapes=[pltpu.VMEM((C_OUT // LANES, tm, LANES), f32)] * 4,
        compiler_params=_params("parallel"),
        name="merge",
    )(x2, ya, yb, *os_, *lses, gain, w_gates, w_a, w_b, w_c, w_out)


def _pack_w_in(w_in):
    offs = [0]
    for s in IN_SIZES:
        offs.append(offs[-1] + s)
    col = lambda i: w_in[:, :, offs[i]:offs[i + 1]]
    aq, ak, av, iq, ik, iw, su, cq, ck, cv, gates = [col(i) for i in range(len(IN_SIZES))]
    parts = [aq, ak, ak, iq, ik, ik, su]
    for g in range(len(DIL_PAIRS)):
        gs = slice(g * C_OUT, (g + 1) * C_OUT)
        parts += [cq[:, :, gs], ck[:, :, gs], cv[:, :, gs]]
    iw_pad = jnp.pad(iw, ((0, 0), (0, 0), (0, SUBLANES - IDX_HEADS)))
    w_t = jnp.transpose(jnp.concatenate([av, av, iw_pad], axis=2), (0, 2, 1))
    return jnp.concatenate(parts, axis=2).astype(bf16), w_t.astype(bf16), gates.astype(bf16)


def _segment_mean_matrix(width):
    seg = jnp.arange(width) // HEAD_DIM
    return jnp.where(seg[:, None] == seg[None, :], 1.0 / HEAD_DIM, 0.0).astype(bf16)


def _tile_gain(g, reps):
    return jnp.tile(g[:, None, :], (1, 1, reps))


def kernel(x, ffn1_norm, ffn1_gate, ffn1_up, ffn1_down, mix_norm, w_in, a_q_norm, a_k_norm, s5_lam_re, s5_lam_im, s5_log_dt, s5_b_re, s5_b_im, s5_c_re, s5_c_im, s5_d, s5_glu_a, s5_glu_b, c_q_norm, c_k_norm, w_branch_a, w_branch_b, w_branch_c, w_out, ffn2_norm, ffn2_gate, ffn2_up, ffn2_down):
    bsz, seq, _ = x.shape
    depth = w_in.shape[0]
    n = bsz * seq
    x2 = x.reshape(n, D_MODEL)
    assert TOKEN_TILE == KEY_CHUNK and seq % KEY_CHUNK == 0
    seg_mean = _segment_mean_matrix(A_WIDTH)
    grp_per_slab = S5_SLAB // S5_STATE
    cast = lambda w: w.astype(bf16)
    row = lambda g: g[:, None, :]
    ffn1 = (row(ffn1_norm), cast(ffn1_gate), cast(ffn1_up), cast(ffn1_down))
    ffn2 = (row(ffn2_norm), cast(ffn2_gate), cast(ffn2_up), cast(ffn2_down))
    w_packed, w_t, w_gates = _pack_w_in(w_in)
    head_gains = (_tile_gain(a_q_norm, A_HEADS), _tile_gain(a_k_norm, 2),
                  _tile_gain(c_q_norm, C_HEADS_PER_GROUP), _tile_gain(c_k_norm, C_HEADS_PER_GROUP))
    glu = (cast(s5_glu_a), cast(s5_glu_b))
    merge_w = (row(mix_norm), w_gates, cast(w_branch_a), cast(w_branch_b), cast(w_branch_c), cast(w_out))
    r3 = lambda a: a.reshape(bsz, seq, a.shape[-1])
    b_t = lambda b: jnp.tile(jnp.transpose(b, (2, 0, 1)).reshape(S5_GROUP, S5_LANES), (grp_per_slab, 1))
    c_t = lambda c: jnp.tile(jnp.transpose(c, (0, 2, 1)).reshape(S5_LANES, S5_GROUP), (1, grp_per_slab))
    for l in range(depth):
        x2 = _ffn(x2, l, *ffn1)
        aq, akk, vvt, iq, ikk, iwt, su, c0, c1, c2 = _in_proj(x2, l, row(mix_norm), w_packed, w_t, seg_mean,
                                                               *head_gains)
        vvt = vvt.reshape(bsz, seq // KEY_CHUNK, LANES, KEY_CHUNK)
        ya = _dsa(r3(aq), r3(iq), iwt, r3(akk), vvt, r3(ikk)).reshape(n, A_WIDTH)
        yb = _s5(r3(su), l, s5_lam_re[l].reshape(1, -1), s5_lam_im[l].reshape(1, -1),
                 jnp.repeat(s5_log_dt[l], S5_STATE).reshape(1, -1),
                 b_t(s5_b_re[l]), b_t(s5_b_im[l]), c_t(s5_c_re[l]), c_t(s5_c_im[l]),
                 s5_d[l].reshape(1, -1), *glu).reshape(n, S5_WIDTH)
        os_, lses = [], []
        for cg, (_, dilation) in zip((c0, c1, c2), DIL_PAIRS):
            o, lse = _dilated(cg.reshape(bsz, seq // dilation, dilation * C_WIDTH), dilation)
            os_.append(o)
            lses.append(lse)
        x2 = _merge(x2, l, ya, yb, os_, lses, *merge_w)
        x2 = _ffn(x2, l, *ffn2)
    return x2.reshape(bsz, seq, D_MODEL)
```

```python
import functools
import math

import jax
import jax.numpy as jnp
from jax import lax
from jax.experimental import pallas as pl
from jax.experimental.pallas import tpu as pltpu

f32 = jnp.float32
bf16 = jnp.bfloat16
i32 = jnp.int32

D_MODEL = 1024
D_FF = 2816
HEAD_DIM = 64
RMS_EPS = 1e-6
A_HEADS = 8
A_WIDTH = A_HEADS * HEAD_DIM
IDX_HEADS = 4
IDX_DIM = 64
TOPK_MAX = 256
S5_WIDTH = 512
S5_GROUP = 16
S5_GROUPS = 32
S5_STATE = 64
S5_LANES = S5_GROUPS * S5_STATE
DIL_PAIRS = ((128, 1), (512, 4), (2048, 16))
C_HEADS_PER_GROUP = 4
C_OUT = C_HEADS_PER_GROUP * HEAD_DIM
C_WIDTH = 3 * C_OUT
IN_SIZES = (A_WIDTH, HEAD_DIM, HEAD_DIM, IDX_HEADS * IDX_DIM, IDX_DIM, IDX_HEADS,
            S5_WIDTH, C_WIDTH, C_WIDTH, C_WIDTH, 3 * D_MODEL)

LANES = 128
SUBLANES = 8
VMEM_LIMIT_BYTES = 56 * 1024 * 1024

TOKEN_TILE = 512
WIDE_TOKEN_TILE = 1024
FF_CHUNK = 256
Q_TILE = 256
ATT_HEADS = 4
KEY_CHUNK = 512
COUNT_ROWS = 64
KEY_BITS = 32
PLANE_ROWS = KEY_BITS * 8
RANK_ROWS = 256
LOOP_UNITS = (4, 2, 1)
S5_TIME_TILE = 1024
S5_SLAB = 512
S5_SCAN_LANES = 512
DIL_SPAN = 128
DIL_ROWS = 512

NEG_INF = float("-inf")
LOG2_E = math.log2(math.e)
INT_MIN = -2 ** 31
NEG_INF_KEY = INT_MIN + 0x7FFFFF

_NT = (((1,), (1,)), ((), ()))


def _params(*sem):
    return pltpu.CompilerParams(dimension_semantics=sem, vmem_limit_bytes=VMEM_LIMIT_BYTES)


def _resident(shape, layer=None):
    nd = len(shape)
    if layer is None:
        return pl.BlockSpec(shape, lambda *_: (0,) * nd, pipeline_mode=pl.Buffered(1))
    return pl.BlockSpec((None,) + tuple(shape), lambda *_: (layer,) + (0,) * nd, pipeline_mode=pl.Buffered(1))


def _dot(a, b):
    return jnp.dot(a, b, preferred_element_type=f32)


def _dot_nt(a, b):
    return lax.dot_general(a, b, _NT, preferred_element_type=f32)


def _rmsnorm_rows(x, gain):
    return x * lax.rsqrt(jnp.mean(x * x, axis=-1, keepdims=True) + RMS_EPS) * gain


def _head_rmsnorm(z, gain_tiled, seg_mean):
    ms = _dot((z * z).astype(bf16), seg_mean)
    return z * lax.rsqrt(ms + RMS_EPS) * gain_tiled


def _ffn_body(x_ref, g_ref, wg_ref, wu_ref, wd_ref, o_ref, acc_ref):
    x = x_ref[...]
    h = _rmsnorm_rows(x, g_ref[...]).astype(bf16)
    for c in range(D_FF // FF_CHUNK):
        sl = slice(c * FF_CHUNK, (c + 1) * FF_CHUNK)
        gate = _dot(h, wg_ref[:, sl])
        up = _dot(h, wu_ref[:, sl])
        act = (gate * jax.nn.sigmoid(gate) * up).astype(bf16)
        contrib = _dot(act, wd_ref[sl, :])
        if c == 0:
            acc_ref[...] = contrib
        else:
            acc_ref[...] += contrib
    o_ref[...] = x + 0.5 * acc_ref[...]


def _ffn(x2, layer, gain, w_gate, w_up, w_down):
    n = x2.shape[0]
    tm = min(WIDE_TOKEN_TILE, n)
    tile = pl.BlockSpec((tm, D_MODEL), lambda i: (i, 0))
    return pl.pallas_call(
        _ffn_body,
        grid=(n // tm,),
        in_specs=[tile, _resident((1, D_MODEL), layer), _resident((D_MODEL, D_FF), layer),
                  _resident((D_MODEL, D_FF), layer), _resident((D_FF, D_MODEL), layer)],
        out_specs=tile,
        out_shape=jax.ShapeDtypeStruct((n, D_MODEL), f32),
        scratch_shapes=[pltpu.VMEM((tm, D_MODEL), f32)],
        compiler_params=_params("parallel"),
        name="ffn",
    )(x2, gain, w_gate, w_up, w_down)


_P_AQ = 0
_P_AKK = _P_AQ + A_WIDTH
_P_IQ = _P_AKK + LANES
_P_IKK = _P_IQ + IDX_HEADS * IDX_DIM
_P_SU = _P_IKK + LANES
_P_C = _P_SU + S5_WIDTH
_P_TOTAL = _P_C + 3 * C_WIDTH
_T_ROWS = LANES + SUBLANES


def _interleave_store(dst_ref, value, dilation, perm_ref):
    rows, width = value.shape
    for j in range(width // LANES):
        perm_ref[j] = value[:, j * LANES:(j + 1) * LANES]
    for r in range(dilation):
        for j in range(width // LANES):
            piece = perm_ref[j, pl.ds(r, rows // dilation, stride=dilation), :]
            dst_ref[:, r * width + j * LANES:r * width + (j + 1) * LANES] = piece.astype(dst_ref.dtype)


def _deinterleave_load(src_ref, dilation, perm_ref):
    width = src_ref.shape[1] // dilation
    rows = src_ref.shape[0] * dilation
    for r in range(dilation):
        for j in range(width // LANES):
            piece = src_ref[:, r * width + j * LANES:r * width + (j + 1) * LANES]
            perm_ref[j, pl.ds(r, rows // dilation, stride=dilation), :] = piece.astype(f32)
    return jnp.concatenate([perm_ref[j] for j in range(width // LANES)], axis=1)


def _in_proj_body(x_ref, g_ref, w_ref, wt_ref, seg_ref, aqg_ref, akg_ref, cqg_ref, ckg_ref,
                  aq_ref, akk_ref, vvt_ref, iq_ref, ikk_ref, iwt_ref, su_ref, c0_ref, c1_ref, c2_ref, perm_ref):
    h = _rmsnorm_rows(x_ref[...], g_ref[...]).astype(bf16)
    full_c = _dot(h, w_ref[:, _P_C:])
    full_a = _dot(h, w_ref[:, :_P_C])

    def proj(start, width):
        if start >= _P_C:
            return full_c[:, start - _P_C:start - _P_C + width]
        return full_a[:, start:start + width]

    seg = seg_ref[...]
    for c_ref, (_, dilation), g in zip((c0_ref, c1_ref, c2_ref), DIL_PAIRS, range(len(DIL_PAIRS))):
        base = _P_C + g * C_WIDTH
        cq = _head_rmsnorm(proj(base, C_OUT), cqg_ref[...], seg[:C_OUT, :C_OUT]) * (HEAD_DIM ** -0.5)
        ck = _head_rmsnorm(proj(base + C_OUT, C_OUT), ckg_ref[...], seg[:C_OUT, :C_OUT])
        qkv = (cq, ck, proj(base + 2 * C_OUT, C_OUT))
        if dilation == 1:
            for i, part in enumerate(qkv):
                c_ref[:, i * C_OUT:(i + 1) * C_OUT] = part.astype(bf16)
        else:
            _interleave_store(c_ref, jnp.concatenate(qkv, axis=1), dilation, perm_ref)
    aq = _head_rmsnorm(proj(_P_AQ, A_WIDTH), aqg_ref[...], seg) * (HEAD_DIM ** -0.5)
    aq_ref[...] = aq.astype(bf16)
    akk_ref[...] = _head_rmsnorm(proj(_P_AKK, LANES), akg_ref[...], seg[:LANES, :LANES]).astype(bf16)
    iq_ref[...] = proj(_P_IQ, IDX_HEADS * IDX_DIM).astype(bf16)
    ikk_ref[...] = proj(_P_IKK, LANES).astype(bf16)
    su_ref[...] = proj(_P_SU, S5_WIDTH)
    vvt_ref[0] = _dot_nt(wt_ref[:LANES, :], h).astype(bf16)
    iwt_ref[0] = _dot_nt(wt_ref[LANES:, :], h)


def _in_proj(x2, layer, gain, w_packed, w_t, seg_mean, aq_gain, ak_gain, cq_gain, ck_gain):
    n = x2.shape[0]
    tm = min(TOKEN_TILE, n)
    rows = lambda w: pl.BlockSpec((tm, w), lambda i: (i, 0))
    cols = lambda r: pl.BlockSpec((1, r, tm), lambda i: (i, 0, 0))
    outs = [(rows(A_WIDTH), (n, A_WIDTH), bf16), (rows(LANES), (n, LANES), bf16),
            (cols(LANES), (n // tm, LANES, tm), bf16), (rows(IDX_HEADS * IDX_DIM), (n, IDX_HEADS * IDX_DIM), bf16),
            (rows(LANES), (n, LANES), bf16), (cols(SUBLANES), (n // tm, SUBLANES, tm), f32),
            (rows(S5_WIDTH), (n, S5_WIDTH), f32)]
    for _, d in DIL_PAIRS:
        outs.append((pl.BlockSpec((tm // d, d * C_WIDTH), lambda i: (i, 0)), (n // d, d * C_WIDTH), bf16))
    return pl.pallas_call(
        _in_proj_body,
        grid=(n // tm,),
        in_specs=[rows(D_MODEL), _resident((1, D_MODEL), layer),
                  _resident((D_MODEL, _P_TOTAL), layer), _resident((_T_ROWS, D_MODEL), layer),
                  _resident((A_WIDTH, A_WIDTH)),
                  _resident((1, A_WIDTH), layer), _resident((1, LANES), layer),
                  _resident((1, C_OUT), layer), _resident((1, C_OUT), layer)],
        out_specs=[o[0] for o in outs],
        out_shape=[jax.ShapeDtypeStruct(o[1], o[2]) for o in outs],
        scratch_shapes=[pltpu.VMEM((C_WIDTH // LANES, tm, LANES), f32)],
        compiler_params=_params("parallel"),
        name="in_proj",
    )(x2, gain, w_packed, w_t, seg_mean, aq_gain, ak_gain, cq_gain, ck_gain)


def _float_order_key(x):
    bits = pltpu.bitcast(x, i32)
    return jnp.where(bits < 0, bits ^ 0x7FFFFFFF, bits)


def _fold_rows(x, op, keep=SUBLANES):
    return op(x.reshape(x.shape[0] // keep, keep, x.shape[1]), axis=0)


def _bit_planes(words):
    a = [words[t * SUBLANES:(t + 1) * SUBLANES] for t in range(KEY_BITS)]
    for j, m in ((16, 0x0000FFFF), (8, 0x00FF00FF), (4, 0x0F0F0F0F), (2, 0x33333333), (1, 0x55555555)):
        shift = jnp.full(a[0].shape, j, i32)
        for k in range(KEY_BITS):
            if k & j == 0:
                t = (a[k] ^ lax.shift_right_logical(a[k + j], shift)) & m
                a[k] = a[k] ^ t
                a[k + j] = a[k + j] ^ lax.shift_left(t, shift)
    return a


def _dsa_body(topk, aq_ref, iq_ref, iwt_ref, kk_ref, vvt_ref, ikk_ref, rank_ref, o_ref,
              key_ref, plane_ref, cand_ref, bias_ref, s_ref, m_ref, l_ref, acc_ref):
    qb = pl.program_id(1)
    n_chunks = lax.shift_right_logical(qb * Q_TILE, int(math.log2(KEY_CHUNK))) + 1
    q_pos = qb * Q_TILE + lax.broadcasted_iota(i32, (1, Q_TILE), 1)
    key_iota = lax.broadcasted_iota(i32, (KEY_CHUNK, Q_TILE), 0)
    low_half = lax.broadcasted_iota(i32, (Q_TILE, LANES), 1) < HEAD_DIM
    zero_b = jnp.zeros((), bf16)

    def stack_heads(x, n_heads):
        blocks = []
        for h in range(n_heads):
            pair = x[:, (h // 2) * LANES:(h // 2 + 1) * LANES]
            blocks.append(jnp.where(low_half if h % 2 == 0 else ~low_half, pair, zero_b))
        return jnp.concatenate(blocks, axis=0)

    @pl.when((pl.program_id(0) == 0) & (qb == 0))
    def _():
        plane_ref[...] = jnp.zeros(plane_ref.shape, i32)

    iq_all = stack_heads(iq_ref[0], IDX_HEADS)
    w_t = iwt_ref[0] * ((IDX_DIM ** -0.5) * (IDX_HEADS ** -0.5))

    def key_rows(c, part, rows):
        return pl.ds(pl.multiple_of(c * KEY_CHUNK + part * rows, rows), rows)

    def chunk_loop(body, carry, end=n_chunks):
        start = 0
        for unit in LOOP_UNITS:
            n_iter = lax.shift_right_logical(end - start, int(math.log2(unit)))

            def multi(i, cr, unit=unit, start=start):
                for j in range(unit):
                    cr = body(start + i * unit + j, cr)
                return cr

            carry = lax.fori_loop(0, n_iter, multi, carry)
            start = start + n_iter * unit
        return carry

    def score_chunk(c, _, diagonal=False):
        logits = _dot_nt(ikk_ref[0, key_rows(c, 0, KEY_CHUNK), :], iq_all)
        score = None
        for h in range(IDX_HEADS):
            term = jnp.maximum(logits[:, h * Q_TILE:(h + 1) * Q_TILE], 0.0) * w_t[h:h + 1, :]
            score = term if score is None else score + term
        score = jnp.where(score == 0.0, 0.0, score)
        if diagonal:
            score = jnp.where(c * KEY_CHUNK + key_iota <= q_pos, score, NEG_INF)
        key = _float_order_key(score)
        key_ref[c] = key
        for part in range(KEY_CHUNK // PLANE_ROWS):
            planes = _bit_planes(key[part * PLANE_ROWS:(part + 1) * PLANE_ROWS] ^ INT_MIN)
            for i in range(KEY_BITS):
                plane_ref[i, c * (KEY_CHUNK // PLANE_ROWS) + part] = planes[i]
        return 0

    chunk_loop(score_chunk, 0, end=n_chunks - 1)
    score_chunk(n_chunks - 1, 0, diagonal=True)

    n_planes = n_chunks * (KEY_CHUNK // PLANE_ROWS)
    block_id = lax.broadcasted_iota(i32, cand_ref.shape, 0)
    cand_ref[...] = jnp.where(block_id < n_planes, -1, 0)

    def bit_step(i, carry):
        prefix, n_above = carry
        cand = cand_ref[...]
        ones = cand & plane_ref[i]
        cnt = jnp.sum(jnp.sum(lax.population_count(ones), axis=0).astype(f32), axis=0, keepdims=True)
        accept = n_above + cnt >= topk
        cand_ref[...] = jnp.where(accept, ones, cand ^ ones)
        prefix = jnp.where(accept, prefix | lax.shift_left(jnp.int32(1), KEY_BITS - 1 - i), prefix)
        return prefix, jnp.where(accept, n_above, n_above + cnt)

    prefix, n_gt = lax.fori_loop(0, KEY_BITS, bit_step,
                                 (jnp.zeros((1, Q_TILE), i32), jnp.zeros((1, Q_TILE), f32)))
    thr = prefix ^ INT_MIN
    need = jnp.where(thr <= NEG_INF_KEY, 0.0, topk - n_gt)
    rank_mat = rank_ref[...]

    top_rows = lax.broadcasted_iota(i32, (LANES, Q_TILE), 0) < HEAD_DIM
    for group in range(A_HEADS // ATT_HEADS):
        width = ATT_HEADS * HEAD_DIM
        q_all = stack_heads(aq_ref[0, :, group * width:(group + 1) * width], ATT_HEADS)

        m_ref[...] = jnp.full(m_ref.shape, NEG_INF, f32)

        def logits_chunk(c, seen, q_all=q_all, first_group=group == 0):
            for part in range(KEY_CHUNK // RANK_ROWS):
                rows = slice(part * RANK_ROWS, (part + 1) * RANK_ROWS)
                if first_group:
                    k = key_ref[c, rows, :]
                    eq = k == thr
                    tie = jnp.where(eq, 1.0, 0.0)
                    rank = _dot(rank_mat, tie.astype(bf16)) + seen
                    seen = seen + jnp.sum(_fold_rows(tie, jnp.sum, COUNT_ROWS), axis=0, keepdims=True)
                    order = jnp.where(eq, rank, jnp.where(k > thr, -1.0, 2.0 ** 30))
                    bias = jnp.where(order < need, 0.0, NEG_INF)
                    bias_ref[c, rows, :] = bias
                else:
                    bias = bias_ref[c, rows, :]
                s = _dot_nt(kk_ref[0, key_rows(c, part, RANK_ROWS), :], q_all)
                for h in range(ATT_HEADS):
                    cols = slice(h * Q_TILE, (h + 1) * Q_TILE)
                    sh = (s[:, cols] + bias) * LOG2_E
                    s_ref[c, rows, cols] = sh
                    m_ref[:, cols] = jnp.maximum(m_ref[:, cols], _fold_rows(sh, jnp.max))
            return seen

        chunk_loop(logits_chunk, jnp.zeros((1, Q_TILE), f32))
        m = jnp.max(m_ref[...], axis=0, keepdims=True)

        l_ref[...] = jnp.zeros(l_ref.shape, f32)
        acc_ref[...] = jnp.zeros(acc_ref.shape, f32)

        def softmax_chunk(c, _, m=m):
            for part in range(KEY_CHUNK // RANK_ROWS):
                rows = slice(part * RANK_ROWS, (part + 1) * RANK_ROWS)
                p = jnp.exp2(s_ref[c, rows, :] - m)
                l_ref[...] += _fold_rows(p, jnp.sum)
                acc_ref[...] += _dot(vvt_ref[0, c, :, rows], p.astype(bf16))
            return 0

        chunk_loop(softmax_chunk, 0)
        out_t = acc_ref[...] / jnp.sum(l_ref[...], axis=0, keepdims=True)
        for j in range(ATT_HEADS // 2):
            even = out_t[:, (2 * j) * Q_TILE:(2 * j + 1) * Q_TILE]
            odd = out_t[:, (2 * j + 1) * Q_TILE:(2 * j + 2) * Q_TILE]
            pair = group * (ATT_HEADS // 2) + j
            o_ref[0, :, pair * LANES:(pair + 1) * LANES] = jnp.where(top_rows, even, odd).T.astype(bf16)


def _tie_rank_matrix():
    i = jnp.arange(RANK_ROWS)
    return (i[None, :] < i[:, None]).astype(bf16)


def _dsa(aq, iq, iwt, akk, vvt, ikk):
    bsz, seq, _ = aq.shape
    topk = min(TOPK_MAX, seq // 4)
    n_kc = seq // KEY_CHUNK
    q_per_chunk = KEY_CHUNK // Q_TILE

    def q_spec(width):
        return pl.BlockSpec((1, Q_TILE, width), lambda b, q: (b, q, 0))

    kv_spec = pl.BlockSpec((1, seq, LANES), lambda b, q: (b, 0, 0))
    return pl.pallas_call(
        functools.partial(_dsa_body, topk),
        grid=(bsz, seq // Q_TILE),
        in_specs=[q_spec(A_WIDTH), q_spec(IDX_HEADS * IDX_DIM),
                  pl.BlockSpec((1, SUBLANES, Q_TILE), lambda b, q: (b * n_kc + q // q_per_chunk, 0, q % q_per_chunk)),
                  kv_spec, pl.BlockSpec((1, n_kc, LANES, KEY_CHUNK), lambda b, q: (b, 0, 0, 0)), kv_spec,
                  _resident((RANK_ROWS, RANK_ROWS))],
        out_specs=q_spec(A_WIDTH),
        out_shape=jax.ShapeDtypeStruct((bsz, seq, A_WIDTH), bf16),
        scratch_shapes=[pltpu.VMEM((n_kc, KEY_CHUNK, Q_TILE), i32),
                        pltpu.VMEM((KEY_BITS, seq // PLANE_ROWS, SUBLANES, Q_TILE), i32),
                        pltpu.VMEM((seq // PLANE_ROWS, SUBLANES, Q_TILE), i32),
                        pltpu.VMEM((n_kc, KEY_CHUNK, Q_TILE), f32),
                        pltpu.VMEM((n_kc, KEY_CHUNK, ATT_HEADS * Q_TILE), f32),
                        pltpu.VMEM((SUBLANES, ATT_HEADS * Q_TILE), f32),
                        pltpu.VMEM((SUBLANES, ATT_HEADS * Q_TILE), f32),
                        pltpu.VMEM((LANES, ATT_HEADS * Q_TILE), f32)],
        compiler_params=_params("arbitrary", "arbitrary"),
        name="dsa",
    )(aq, iq, iwt, akk, vvt, ikk, _tie_rank_matrix())


def _cmul(ar, ai, br, bi):
    return ar * br - ai * bi, ar * bi + ai * br


def _s5_prepare(lam_re_ref, lam_im_ref, logdt_ref, bre_ref, bim_ref, cre_ref, cim_ref,
                bmat_re, bmat_im, cmat_re, cmat_im, shift_re, shift_im, pow_re, pow_im):
    lr = jnp.minimum(lam_re_ref[...], -1e-4)
    li = lam_im_ref[...]
    dt = jnp.exp(logdt_ref[...])
    mag = jnp.exp(lr * dt)
    l1r, l1i = mag * jnp.cos(li * dt), mag * jnp.sin(li * dt)
    den = lr * lr + li * li
    cr = ((l1r - 1.0) * lr + l1i * li) / den
    ci = (l1i * lr - (l1r - 1.0) * li) / den

    n_slab = S5_LANES // S5_SLAB
    row_g = lax.broadcasted_iota(i32, (LANES, S5_SLAB), 0) // S5_GROUP
    lane_g = lax.broadcasted_iota(i32, (LANES, S5_SLAB), 1) // S5_STATE
    diag_b = row_g == lane_g
    row_g2 = lax.broadcasted_iota(i32, (S5_SLAB, LANES), 0) // S5_STATE
    lane_g2 = lax.broadcasted_iota(i32, (S5_SLAB, LANES), 1) // S5_GROUP
    diag_c = row_g2 == lane_g2
    for j in range(n_slab):
        sl = slice(j * S5_SLAB, (j + 1) * S5_SLAB)
        bbr, bbi = _cmul(cr[:, sl], ci[:, sl], bre_ref[:, sl], bim_ref[:, sl])
        bmat_re[j] = jnp.where(diag_b, bbr, 0.0).astype(bf16)
        bmat_im[j] = jnp.where(diag_b, bbi, 0.0).astype(bf16)
        cmat_re[j] = jnp.where(diag_c, cre_ref[sl, :], 0.0).astype(bf16)
        cmat_im[j] = jnp.where(diag_c, -cim_ref[sl, :], 0.0).astype(bf16)

    l2r, l2i = _cmul(l1r, l1i, l1r, l1i)
    l3r, l3i = _cmul(l2r, l2i, l1r, l1i)
    l4r, l4i = _cmul(l2r, l2i, l2r, l2i)
    l5r, l5i = _cmul(l4r, l4i, l1r, l1i)
    l6r, l6i = _cmul(l4r, l4i, l2r, l2i)
    l7r, l7i = _cmul(l4r, l4i, l3r, l3i)
    l8r, l8i = _cmul(l4r, l4i, l4r, l4i)
    row = lax.broadcasted_iota(i32, (SUBLANES, S5_LANES), 0)
    for idx, (k, pr, pi) in enumerate(((1, l1r, l1i), (2, l2r, l2i), (4, l4r, l4i))):
        shift_re[idx] = jnp.where(row >= k, pr, 0.0)
        shift_im[idx] = jnp.where(row >= k, pi, 0.0)
    pr_acc = jnp.zeros((SUBLANES, S5_LANES), f32)
    pi_acc = jnp.zeros((SUBLANES, S5_LANES), f32)
    powers = ((l1r, l1i), (l2r, l2i), (l3r, l3i), (l4r, l4i), (l5r, l5i), (l6r, l6i), (l7r, l7i), (l8r, l8i))
    for i, (pr, pi) in enumerate(powers):
        pr_acc = jnp.where(row == i, pr, pr_acc)
        pi_acc = jnp.where(row == i, pi, pi_acc)
    pow_re[...] = pr_acc
    pow_im[...] = pi_acc


def _s5_body(u_ref, lam_re_ref, lam_im_ref, logdt_ref, bre_ref, bim_ref, cre_ref, cim_ref, d_ref,
             glu_a_ref, glu_b_ref, o_ref,
             bmat_re, bmat_im, cmat_re, cmat_im, shift_re, shift_im, pow_re, pow_im,
             carry_re, carry_im, st_re, st_im):
    t_idx = pl.program_id(1)

    @pl.when((pl.program_id(0) == 0) & (t_idx == 0))
    def _():
        _s5_prepare(lam_re_ref, lam_im_ref, logdt_ref, bre_ref, bim_ref, cre_ref, cim_ref,
                    bmat_re, bmat_im, cmat_re, cmat_im, shift_re, shift_im, pow_re, pow_im)

    @pl.when(t_idx == 0)
    def _():
        carry_re[...] = jnp.zeros(carry_re.shape, f32)
        carry_im[...] = jnp.zeros(carry_im.shape, f32)

    u = u_ref[0]
    ub = u.astype(bf16)
    n_slab = S5_LANES // S5_SLAB
    n_tiles = u.shape[0] // SUBLANES
    ys = []
    for j in range(n_slab):
        sl = slice(j * S5_SLAB, (j + 1) * S5_SLAB)
        uj = ub[:, j * LANES:(j + 1) * LANES]
        st_re[:, sl] = _dot(uj, bmat_re[j])
        st_im[:, sl] = _dot(uj, bmat_im[j])

        coef = [(shift_re[i, :, sl], shift_im[i, :, sl]) for i in range(3)]
        pwr, pwi = pow_re[:, sl], pow_im[:, sl]
        c_re, c_im = carry_re[:, sl], carry_im[:, sl]
        for n in range(n_tiles):
            rows = slice(n * SUBLANES, (n + 1) * SUBLANES)
            xr, xi = st_re[rows, sl], st_im[rows, sl]
            for i, k in enumerate((1, 2, 4)):
                dr, di = _cmul(coef[i][0], coef[i][1], pltpu.roll(xr, k, 0), pltpu.roll(xi, k, 0))
                xr, xi = xr + dr, xi + di
            dr, di = _cmul(pwr, pwi, c_re, c_im)
            xr, xi = xr + dr, xi + di
            st_re[rows, sl] = xr
            st_im[rows, sl] = xi
            c_re, c_im = xr[SUBLANES - 1:SUBLANES, :], xi[SUBLANES - 1:SUBLANES, :]
        carry_re[:, sl] = c_re
        carry_im[:, sl] = c_im
        ys.append(_dot(st_re[:, sl].astype(bf16), cmat_re[j]) + _dot(st_im[:, sl].astype(bf16), cmat_im[j]))
    y = jnp.concatenate(ys, axis=1) + d_ref[...] * u
    g = jax.nn.gelu(y).astype(bf16)
    o_ref[0] = (_dot(g, glu_a_ref[...]) * jax.nn.sigmoid(_dot(g, glu_b_ref[...]))).astype(bf16)


def _s5(su, layer, lam_re, lam_im, logdt, b_re, b_im, c_re, c_im, d_skip, glu_a, glu_b):
    bsz, seq, _ = su.shape
    tt = min(S5_TIME_TILE, seq)
    n_slab = S5_LANES // S5_SLAB
    tile = pl.BlockSpec((1, tt, S5_WIDTH), lambda b, t: (b, t, 0))
    return pl.pallas_call(
        _s5_body,
        grid=(bsz, seq // tt),
        in_specs=[tile, _resident((1, S5_LANES)), _resident((1, S5_LANES)), _resident((1, S5_LANES)),
                  _resident((LANES, S5_LANES)), _resident((LANES, S5_LANES)),
                  _resident((S5_LANES, LANES)), _resident((S5_LANES, LANES)), _resident((1, S5_WIDTH)),
                  _resident((S5_WIDTH, S5_WIDTH), layer), _resident((S5_WIDTH, S5_WIDTH), layer)],
        out_specs=tile,
        out_shape=jax.ShapeDtypeStruct((bsz, seq, S5_WIDTH), bf16),
        scratch_shapes=[pltpu.VMEM((n_slab, LANES, S5_SLAB), bf16), pltpu.VMEM((n_slab, LANES, S5_SLAB), bf16),
                        pltpu.VMEM((n_slab, S5_SLAB, LANES), bf16), pltpu.VMEM((n_slab, S5_SLAB, LANES), bf16),
                        pltpu.VMEM((3, SUBLANES, S5_LANES), f32), pltpu.VMEM((3, SUBLANES, S5_LANES), f32),
                        pltpu.VMEM((SUBLANES, S5_LANES), f32), pltpu.VMEM((SUBLANES, S5_LANES), f32),
                        pltpu.VMEM((1, S5_LANES), f32), pltpu.VMEM((1, S5_LANES), f32),
                        pltpu.VMEM((tt, S5_LANES), f32), pltpu.VMEM((tt, S5_LANES), f32)],
        compiler_params=_params("arbitrary", "arbitrary"),
        name="s5",
    )(su, lam_re, lam_im, logdt, b_re, b_im, c_re, c_im, d_skip, glu_a, glu_b)


def _dilated_body(cur_ref, prev_ref, o_ref, lse_ref):
    n = pl.program_id(2)
    rows = cur_ref.shape[1]
    qi = lax.broadcasted_iota(i32, (DIL_SPAN, 2 * DIL_SPAN), 0)
    kj = lax.broadcasted_iota(i32, (DIL_SPAN, 2 * DIL_SPAN), 1)
    band = (kj >= qi) & (kj <= qi + DIL_SPAN)
    band_first = band & ((kj >= DIL_SPAN) | (n > 0))
    low_half = lax.broadcasted_iota(i32, (DIL_SPAN, LANES), 1) < HEAD_DIM
    zero_b = jnp.zeros((), bf16)
    for res in range(cur_ref.shape[2] // C_WIDTH):
        q0, k0, v0, o0 = res * C_WIDTH, res * C_WIDTH + C_OUT, res * C_WIDTH + 2 * C_OUT, res * C_OUT
        for blk in range(rows // DIL_SPAN):
            r0 = blk * DIL_SPAN
            q = cur_ref[0, r0:r0 + DIL_SPAN, q0:q0 + C_OUT]
            if blk == 0:
                k = jnp.concatenate([prev_ref[0, :, k0:k0 + C_OUT], cur_ref[0, 0:DIL_SPAN, k0:k0 + C_OUT]], axis=0)
                v = jnp.concatenate([prev_ref[0, :, v0:v0 + C_OUT], cur_ref[0, 0:DIL_SPAN, v0:v0 + C_OUT]], axis=0)
            else:
                k = cur_ref[0, r0 - DIL_SPAN:r0 + DIL_SPAN, k0:k0 + C_OUT]
                v = cur_ref[0, r0 - DIL_SPAN:r0 + DIL_SPAN, v0:v0 + C_OUT]
            mask = band_first if blk == 0 else band
            for j in range(C_OUT // LANES):
                ls = slice(j * LANES, (j + 1) * LANES)
                out_ls = slice(o0 + j * LANES, o0 + (j + 1) * LANES)
                outs, lses = [], []
                for half in (low_half, ~low_half):
                    s = _dot_nt(jnp.where(half, q[:, ls], zero_b), k[:, ls])
                    s = jnp.where(mask, s, NEG_INF)
                    m = jnp.max(s, axis=1, keepdims=True)
                    p = jnp.exp(s - m)
                    den = jnp.sum(p, axis=1, keepdims=True)
                    outs.append(_dot(p.astype(bf16), v[:, ls]) / den)
                    lses.append(jnp.broadcast_to(m + jnp.log(den), (DIL_SPAN, LANES)))
                o_ref[0, r0:r0 + DIL_SPAN, out_ls] = jnp.where(low_half, outs[0], outs[1]).astype(bf16)
                lse_ref[0, r0:r0 + DIL_SPAN, out_ls] = jnp.where(low_half, lses[0], lses[1])


def _dilated(view, dilation):
    bsz, sub, _ = view.shape
    rows = min(DIL_ROWS, sub)
    n_res = min(DIL_ROWS // rows, dilation)
    blocks_per_step = rows // DIL_SPAN
    cur = pl.BlockSpec((1, rows, n_res * C_WIDTH), lambda b, r, n: (b, n, r))
    prev = pl.BlockSpec((1, DIL_SPAN, n_res * C_WIDTH),
                        lambda b, r, n: (b, jnp.maximum(n * blocks_per_step - 1, 0), r))
    out = pl.BlockSpec((1, rows, n_res * C_OUT), lambda b, r, n: (b, n, r))
    o, lse = pl.pallas_call(
        _dilated_body,
        grid=(bsz, dilation // n_res, sub // rows),
        in_specs=[cur, prev],
        out_specs=[out, out],
        out_shape=[jax.ShapeDtypeStruct((bsz, sub, dilation * C_OUT), bf16),
                   jax.ShapeDtypeStruct((bsz, sub, dilation * C_OUT), f32)],
        compiler_params=_params("parallel", "parallel", "arbitrary"),
        name=f"dilated_{dilation}",
    )(view, view)
    return o.reshape(bsz * sub, dilation * C_OUT), lse.reshape(bsz * sub, dilation * C_OUT)


def _merge_body(x_ref, ya_ref, yb_ref, o0_ref, o1_ref, o2_ref, l0_ref, l1_ref, l2_ref,
                g_ref, wgate_ref, wa_ref, wb_ref, wc_ref, wout_ref, out_ref, *perm_refs):
    x = x_ref[...]
    h = _rmsnorm_rows(x, g_ref[...]).astype(bf16)
    (o0, l0), (o1, l1), (o2, l2) = [
        (o_ref[...].astype(f32), l_ref[...]) if d == 1 else
        (_deinterleave_load(o_ref, d, po_ref), _deinterleave_load(l_ref, d, pl_ref))
        for (o_ref, l_ref, po_ref, pl_ref), (_, d) in zip(
            ((o0_ref, l0_ref, None, None), (o1_ref, l1_ref, perm_refs[0], perm_refs[1]),
             (o2_ref, l2_ref, perm_refs[2], perm_refs[3])), DIL_PAIRS)]
    mx = jnp.maximum(jnp.maximum(l0, l1), l2)
    e0, e1, e2 = jnp.exp(l0 - mx), jnp.exp(l1 - mx), jnp.exp(l2 - mx)
    yc = e0 * o0 + e1 * o1 + e2 * o2
    yc = (yc / (e0 + e1 + e2)).astype(bf16)

    def gate(i):
        return jax.nn.sigmoid(_dot(h, wgate_ref[:, i * D_MODEL:(i + 1) * D_MODEL]))

    merged = gate(0) * _dot(ya_ref[...], wa_ref[...])
    merged = merged + gate(1) * _dot(yb_ref[...], wb_ref[...])
    merged = merged + gate(2) * _dot(yc, wc_ref[...])
    out_ref[...] = x + _dot(merged.astype(bf16), wout_ref[...])


def _merge(x2, layer, ya, yb, os_, lses, gain, w_gates, w_a, w_b, w_c, w_out):
    n = x2.shape[0]
    tm = min(WIDE_TOKEN_TILE, n)

    def tile(width):
        return pl.BlockSpec((tm, width), lambda i: (i, 0))

    return pl.pallas_call(
        _merge_body,
        grid=(n // tm,),
        in_specs=[tile(D_MODEL), tile(A_WIDTH), tile(S5_WIDTH)] + 2 * [
            pl.BlockSpec((tm // d, d * C_OUT), lambda i: (i, 0)) for _, d in DIL_PAIRS] +
                 [_resident((1, D_MODEL), layer), _resident((D_MODEL, 3 * D_MODEL), layer),
                  _resident((A_WIDTH, D_MODEL), layer), _resident((S5_WIDTH, D_MODEL), layer),
                  _resident((C_OUT, D_MODEL), layer), _resident((D_MODEL, D_MODEL), layer)],
        out_specs=tile(D_MODEL),
        out_shape=jax.ShapeDtypeStruct((n, D_MODEL), f32),
        scratch_shapes=[pltpu.VMEM((C_OUT // LANES, tm, LANES), f32)] * 4,
        compiler_params=_params("parallel"),
        name="merge",
    )(x2, ya, yb, *os_, *lses, gain, w_gates, w_a, w_b, w_c, w_out)


def _pack_w_in(w_in):
    offs = [0]
    for s in IN_SIZES:
        offs.append(offs[-1] + s)
    col = lambda i: w_in[:, :, offs[i]:offs[i + 1]]
    aq, ak, av, iq, ik, iw, su, cq, ck, cv, gates = [col(i) for i in range(len(IN_SIZES))]
    parts = [aq, ak, ak, iq, ik, ik, su]
    for g in range(len(DIL_PAIRS)):
        gs = slice(g * C_OUT, (g + 1) * C_OUT)
        parts += [cq[:, :, gs], ck[:, :, gs], cv[:, :, gs]]
    iw_pad = jnp.pad(iw, ((0, 0), (0, 0), (0, SUBLANES - IDX_HEADS)))
    w_t = jnp.transpose(jnp.concatenate([av, av, iw_pad], axis=2), (0, 2, 1))
    return jnp.concatenate(parts, axis=2).astype(bf16), w_t.astype(bf16), gates.astype(bf16)


def _segment_mean_matrix(width):
    seg = jnp.arange(width) // HEAD_DIM
    return jnp.where(seg[:, None] == seg[None, :], 1.0 / HEAD_DIM, 0.0).astype(bf16)


def _tile_gain(g, reps):
    return jnp.tile(g[:, None, :], (1, 1, reps))


def kernel(x, ffn1_norm, ffn1_gate, ffn1_up, ffn1_down, mix_norm, w_in, a_q_norm, a_k_norm, s5_lam_re, s5_lam_im, s5_log_dt, s5_b_re, s5_b_im, s5_c_re, s5_c_im, s5_d, s5_glu_a, s5_glu_b, c_q_norm, c_k_norm, w_branch_a, w_branch_b, w_branch_c, w_out, ffn2_norm, ffn2_gate, ffn2_up, ffn2_down):
    bsz, seq, _ = x.shape
    depth = w_in.shape[0]
    n = bsz * seq
    x2 = x.reshape(n, D_MODEL)
    assert TOKEN_TILE == KEY_CHUNK and seq % KEY_CHUNK == 0
    seg_mean = _segment_mean_matrix(A_WIDTH)
    grp_per_slab = S5_SLAB // S5_STATE
    cast = lambda w: w.astype(bf16)
    row = lambda g: g[:, None, :]
    ffn1 = (row(ffn1_norm), cast(ffn1_gate), cast(ffn1_up), cast(ffn1_down))
    ffn2 = (row(ffn2_norm), cast(ffn2_gate), cast(ffn2_up), cast(ffn2_down))
    w_packed, w_t, w_gates = _pack_w_in(w_in)
    head_gains = (_tile_gain(a_q_norm, A_HEADS), _tile_gain(a_k_norm, 2),
                  _tile_gain(c_q_norm, C_HEADS_PER_GROUP), _tile_gain(c_k_norm, C_HEADS_PER_GROUP))
    glu = (cast(s5_glu_a), cast(s5_glu_b))
    merge_w = (row(mix_norm), w_gates, cast(w_branch_a), cast(w_branch_b), cast(w_branch_c), cast(w_out))
    r3 = lambda a: a.reshape(bsz, seq, a.shape[-1])
    b_t = lambda b: jnp.tile(jnp.transpose(b, (2, 0, 1)).reshape(S5_GROUP, S5_LANES), (grp_per_slab, 1))
    c_t = lambda c: jnp.tile(jnp.transpose(c, (0, 2, 1)).reshape(S5_LANES, S5_GROUP), (1, grp_per_slab))
    for l in range(depth):
        x2 = _ffn(x2, l, *ffn1)
        aq, akk, vvt, iq, ikk, iwt, su, c0, c1, c2 = _in_proj(x2, l, row(mix_norm), w_packed, w_t, seg_mean,
                                                               *head_gains)
        vvt = vvt.reshape(bsz, seq // KEY_CHUNK, LANES, KEY_CHUNK)
        ya = _dsa(r3(aq), r3(iq), iwt, r3(akk), vvt, r3(ikk)).reshape(n, A_WIDTH)
        yb = _s5(r3(su), l, s5_lam_re[l].reshape(1, -1), s5_lam_im[l].reshape(1, -1),
                 jnp.repeat(s5_log_dt[l], S5_STATE).reshape(1, -1),
                 b_t(s5_b_re[l]), b_t(s5_b_im[l]), c_t(s5_c_re[l]), c_t(s5_c_im[l]),
                 s5_d[l].reshape(1, -1), *glu).reshape(n, S5_WIDTH)
        os_, lses = [], []
        for cg, (_, dilation) in zip((c0, c1, c2), DIL_PAIRS):
            o, lse = _dilated(cg.reshape(bsz, seq // dilation, dilation * C_WIDTH), dilation)
            os_.append(o)
            lses.append(lse)
        x2 = _merge(x2, l, ya, yb, os_, lses, *merge_w)
        x2 = _ffn(x2, l, *ffn2)
    return x2.reshape(bsz, seq, D_MODEL)
```

```python
import functools
import math

import jax
import jax.numpy as jnp
from jax import lax
from jax.experimental import pallas as pl
from jax.experimental.pallas import tpu as pltpu

f32 = jnp.float32
bf16 = jnp.bfloat16
i32 = jnp.int32

D_MODEL = 1024
D_FF = 2816
HEAD_DIM = 64
RMS_EPS = 1e-6
A_HEADS = 8
A_WIDTH = A_HEADS * HEAD_DIM
IDX_HEADS = 4
IDX_DIM = 64
TOPK_MAX = 256
S5_WIDTH = 512
S5_GROUP = 16
S5_GROUPS = 32
S5_STATE = 64
S5_LANES = S5_GROUPS * S5_STATE
DIL_PAIRS = ((128, 1), (512, 4), (2048, 16))
C_HEADS_PER_GROUP = 4
C_OUT = C_HEADS_PER_GROUP * HEAD_DIM
C_WIDTH = 3 * C_OUT
IN_SIZES = (A_WIDTH, HEAD_DIM, HEAD_DIM, IDX_HEADS * IDX_DIM, IDX_DIM, IDX_HEADS,
            S5_WIDTH, C_WIDTH, C_WIDTH, C_WIDTH, 3 * D_MODEL)

LANES = 128
SUBLANES = 8
VMEM_LIMIT_BYTES = 56 * 1024 * 1024

TOKEN_TILE = 512
WIDE_TOKEN_TILE = 1024
FF_CHUNK = 256
Q_TILE = 256
ATT_HEADS = 4
KEY_CHUNK = 512
COUNT_ROWS = 64
KEY_BITS = 32
PLANE_ROWS = KEY_BITS * 8
RANK_ROWS = 256
LOOP_UNITS = (4, 2, 1)
S5_TIME_TILE = 1024
S5_SLAB = 512
S5_SCAN_LANES = 512
DIL_SPAN = 128
DIL_ROWS = 1024

NEG_INF = float("-inf")
LOG2_E = math.log2(math.e)
INT_MIN = -2 ** 31
NEG_INF_KEY = INT_MIN + 0x7FFFFF

_NT = (((1,), (1,)), ((), ()))


def _params(*sem):
    return pltpu.CompilerParams(dimension_semantics=sem, vmem_limit_bytes=VMEM_LIMIT_BYTES)


def _resident(shape, layer=None):
    nd = len(shape)
    if layer is None:
        return pl.BlockSpec(shape, lambda *_: (0,) * nd, pipeline_mode=pl.Buffered(1))
    return pl.BlockSpec((None,) + tuple(shape), lambda *_: (layer,) + (0,) * nd, pipeline_mode=pl.Buffered(1))


def _dot(a, b):
    return jnp.dot(a, b, preferred_element_type=f32)


def _dot_nt(a, b):
    return lax.dot_general(a, b, _NT, preferred_element_type=f32)


def _rmsnorm_rows(x, gain):
    return x * lax.rsqrt(jnp.mean(x * x, axis=-1, keepdims=True) + RMS_EPS) * gain


def _head_rmsnorm(z, gain_tiled, seg_mean):
    ms = _dot((z * z).astype(bf16), seg_mean)
    return z * lax.rsqrt(ms + RMS_EPS) * gain_tiled


def _ffn_body(x_ref, g_ref, wg_ref, wu_ref, wd_ref, o_ref, acc_ref):
    x = x_ref[...]
    h = _rmsnorm_rows(x, g_ref[...]).astype(bf16)
    for c in range(D_FF // FF_CHUNK):
        sl = slice(c * FF_CHUNK, (c + 1) * FF_CHUNK)
        gate = _dot(h, wg_ref[:, sl])
        up = _dot(h, wu_ref[:, sl])
        act = (gate * jax.nn.sigmoid(gate) * up).astype(bf16)
        contrib = _dot(act, wd_ref[sl, :])
        if c == 0:
            acc_ref[...] = contrib
        else:
            acc_ref[...] += contrib
    o_ref[...] = x + 0.5 * acc_ref[...]


def _ffn(x2, layer, gain, w_gate, w_up, w_down):
    n = x2.shape[0]
    tm = min(WIDE_TOKEN_TILE, n)
    tile = pl.BlockSpec((tm, D_MODEL), lambda i: (i, 0))
    return pl.pallas_call(
        _ffn_body,
        grid=(n // tm,),
        in_specs=[tile, _resident((1, D_MODEL), layer), _resident((D_MODEL, D_FF), layer),
                  _resident((D_MODEL, D_FF), layer), _resident((D_FF, D_MODEL), layer)],
        out_specs=tile,
        out_shape=jax.ShapeDtypeStruct((n, D_MODEL), f32),
        scratch_shapes=[pltpu.VMEM((tm, D_MODEL), f32)],
        compiler_params=_params("parallel"),
        name="ffn",
    )(x2, gain, w_gate, w_up, w_down)


_P_AQ = 0
_P_AKK = _P_AQ + A_WIDTH
_P_IQ = _P_AKK + LANES
_P_IKK = _P_IQ + IDX_HEADS * IDX_DIM
_P_SU = _P_IKK + LANES
_P_C = _P_SU + S5_WIDTH
_P_TOTAL = _P_C + 3 * C_WIDTH
_T_ROWS = LANES + SUBLANES


def _interleave_store(dst_ref, value, dilation, perm_ref):
    rows, width = value.shape
    for j in range(width // LANES):
        perm_ref[j] = value[:, j * LANES:(j + 1) * LANES]
    for r in range(dilation):
        for j in range(width // LANES):
            piece = perm_ref[j, pl.ds(r, rows // dilation, stride=dilation), :]
            dst_ref[:, r * width + j * LANES:r * width + (j + 1) * LANES] = piece.astype(dst_ref.dtype)


def _deinterleave_load(src_ref, dilation, perm_ref):
    width = src_ref.shape[1] // dilation
    rows = src_ref.shape[0] * dilation
    for r in range(dilation):
        for j in range(width // LANES):
            piece = src_ref[:, r * width + j * LANES:r * width + (j + 1) * LANES]
            perm_ref[j, pl.ds(r, rows // dilation, stride=dilation), :] = piece.astype(f32)
    return jnp.concatenate([perm_ref[j] for j in range(width // LANES)], axis=1)


def _in_proj_body(x_ref, g_ref, w_ref, wt_ref, seg_ref, aqg_ref, akg_ref, cqg_ref, ckg_ref,
                  aq_ref, akk_ref, vvt_ref, iq_ref, ikk_ref, iwt_ref, su_ref, c0_ref, c1_ref, c2_ref, perm_ref):
    h = _rmsnorm_rows(x_ref[...], g_ref[...]).astype(bf16)
    full_c = _dot(h, w_ref[:, _P_C:])
    full_a = _dot(h, w_ref[:, :_P_C])

    def proj(start, width):
        if start >= _P_C:
            return full_c[:, start - _P_C:start - _P_C + width]
        return full_a[:, start:start + width]

    seg = seg_ref[...]
    for c_ref, (_, dilation), g in zip((c0_ref, c1_ref, c2_ref), DIL_PAIRS, range(len(DIL_PAIRS))):
        base = _P_C + g * C_WIDTH
        cq = _head_rmsnorm(proj(base, C_OUT), cqg_ref[...], seg[:C_OUT, :C_OUT]) * (HEAD_DIM ** -0.5)
        ck = _head_rmsnorm(proj(base + C_OUT, C_OUT), ckg_ref[...], seg[:C_OUT, :C_OUT])
        qkv = (cq, ck, proj(base + 2 * C_OUT, C_OUT))
        if dilation == 1:
            for i, part in enumerate(qkv):
                c_ref[:, i * C_OUT:(i + 1) * C_OUT] = part.astype(bf16)
        else:
            _interleave_store(c_ref, jnp.concatenate(qkv, axis=1), dilation, perm_ref)
    aq = _head_rmsnorm(proj(_P_AQ, A_WIDTH), aqg_ref[...], seg) * (HEAD_DIM ** -0.5)
    aq_ref[...] = aq.astype(bf16)
    akk_ref[...] = _head_rmsnorm(proj(_P_AKK, LANES), akg_ref[...], seg[:LANES, :LANES]).astype(bf16)
    iq_ref[...] = proj(_P_IQ, IDX_HEADS * IDX_DIM).astype(bf16)
    ikk_ref[...] = proj(_P_IKK, LANES).astype(bf16)
    su_ref[...] = proj(_P_SU, S5_WIDTH)
    vvt_ref[0] = _dot_nt(wt_ref[:LANES, :], h).astype(bf16)
    iwt_ref[0] = _dot_nt(wt_ref[LANES:, :], h)


def _in_proj(x2, layer, gain, w_packed, w_t, seg_mean, aq_gain, ak_gain, cq_gain, ck_gain):
    n = x2.shape[0]
    tm = min(TOKEN_TILE, n)
    rows = lambda w: pl.BlockSpec((tm, w), lambda i: (i, 0))
    cols = lambda r: pl.BlockSpec((1, r, tm), lambda i: (i, 0, 0))
    outs = [(rows(A_WIDTH), (n, A_WIDTH), bf16), (rows(LANES), (n, LANES), bf16),
            (cols(LANES), (n // tm, LANES, tm), bf16), (rows(IDX_HEADS * IDX_DIM), (n, IDX_HEADS * IDX_DIM), bf16),
            (rows(LANES), (n, LANES), bf16), (cols(SUBLANES), (n // tm, SUBLANES, tm), f32),
            (rows(S5_WIDTH), (n, S5_WIDTH), f32)]
    for _, d in DIL_PAIRS:
        outs.append((pl.BlockSpec((tm // d, d * C_WIDTH), lambda i: (i, 0)), (n // d, d * C_WIDTH), bf16))
    return pl.pallas_call(
        _in_proj_body,
        grid=(n // tm,),
        in_specs=[rows(D_MODEL), _resident((1, D_MODEL), layer),
                  _resident((D_MODEL, _P_TOTAL), layer), _resident((_T_ROWS, D_MODEL), layer),
                  _resident((A_WIDTH, A_WIDTH)),
                  _resident((1, A_WIDTH), layer), _resident((1, LANES), layer),
                  _resident((1, C_OUT), layer), _resident((1, C_OUT), layer)],
        out_specs=[o[0] for o in outs],
        out_shape=[jax.ShapeDtypeStruct(o[1], o[2]) for o in outs],
        scratch_shapes=[pltpu.VMEM((C_WIDTH // LANES, tm, LANES), f32)],
        compiler_params=_params("parallel"),
        name="in_proj",
    )(x2, gain, w_packed, w_t, seg_mean, aq_gain, ak_gain, cq_gain, ck_gain)


def _float_order_key(x):
    bits = pltpu.bitcast(x, i32)
    return jnp.where(bits < 0, bits ^ 0x7FFFFFFF, bits)


def _fold_rows(x, op, keep=SUBLANES):
    return op(x.reshape(x.shape[0] // keep, keep, x.shape[1]), axis=0)


def _bit_planes(words):
    a = [words[t * SUBLANES:(t + 1) * SUBLANES] for t in range(KEY_BITS)]
    for j, m in ((16, 0x0000FFFF), (8, 0x00FF00FF), (4, 0x0F0F0F0F), (2, 0x33333333), (1, 0x55555555)):
        shift = jnp.full(a[0].shape, j, i32)
        for k in range(KEY_BITS):
            if k & j == 0:
                t = (a[k] ^ lax.shift_right_logical(a[k + j], shift)) & m
                a[k] = a[k] ^ t
                a[k + j] = a[k + j] ^ lax.shift_left(t, shift)
    return a


def _dsa_body(topk, aq_ref, iq_ref, iwt_ref, kk_ref, vvt_ref, ikk_ref, rank_ref, o_ref,
              key_ref, plane_ref, cand_ref, bias_ref, s_ref, m_ref, l_ref, acc_ref):
    qb = pl.program_id(1)
    n_chunks = lax.shift_right_logical(qb * Q_TILE, int(math.log2(KEY_CHUNK))) + 1
    q_pos = qb * Q_TILE + lax.broadcasted_iota(i32, (1, Q_TILE), 1)
    key_iota = lax.broadcasted_iota(i32, (KEY_CHUNK, Q_TILE), 0)
    low_half = lax.broadcasted_iota(i32, (Q_TILE, LANES), 1) < HEAD_DIM
    zero_b = jnp.zeros((), bf16)

    def stack_heads(x, n_heads):
        blocks = []
        for h in range(n_heads):
            pair = x[:, (h // 2) * LANES:(h // 2 + 1) * LANES]
            blocks.append(jnp.where(low_half if h % 2 == 0 else ~low_half, pair, zero_b))
        return jnp.concatenate(blocks, axis=0)

    @pl.when((pl.program_id(0) == 0) & (qb == 0))
    def _():
        plane_ref[...] = jnp.zeros(plane_ref.shape, i32)

    iq_all = stack_heads(iq_ref[0], IDX_HEADS)
    w_t = iwt_ref[0] * ((IDX_DIM ** -0.5) * (IDX_HEADS ** -0.5))

    def key_rows(c, part, rows):
        return pl.ds(pl.multiple_of(c * KEY_CHUNK + part * rows, rows), rows)

    def chunk_loop(body, carry, end=n_chunks):
        start = 0
        for unit in LOOP_UNITS:
            n_iter = lax.shift_right_logical(end - start, int(math.log2(unit)))

            def multi(i, cr, unit=unit, start=start):
                for j in range(unit):
                    cr = body(start + i * unit + j, cr)
                return cr

            carry = lax.fori_loop(0, n_iter, multi, carry)
            start = start + n_iter * unit
        return carry

    def score_chunk(c, _, diagonal=False):
        logits = _dot_nt(ikk_ref[0, key_rows(c, 0, KEY_CHUNK), :], iq_all)
        score = None
        for h in range(IDX_HEADS):
            term = jnp.maximum(logits[:, h * Q_TILE:(h + 1) * Q_TILE], 0.0) * w_t[h:h + 1, :]
            score = term if score is None else score + term
        score = jnp.where(score == 0.0, 0.0, score)
        if diagonal:
            score = jnp.where(c * KEY_CHUNK + key_iota <= q_pos, score, NEG_INF)
        key = _float_order_key(score)
        key_ref[c] = key
        for part in range(KEY_CHUNK // PLANE_ROWS):
            planes = _bit_planes(key[part * PLANE_ROWS:(part + 1) * PLANE_ROWS] ^ INT_MIN)
            for i in range(KEY_BITS):
                plane_ref[i, c * (KEY_CHUNK // PLANE_ROWS) + part] = planes[i]
        return 0

    chunk_loop(score_chunk, 0, end=n_chunks - 1)
    score_chunk(n_chunks - 1, 0, diagonal=True)

    n_planes = n_chunks * (KEY_CHUNK // PLANE_ROWS)
    block_id = lax.broadcasted_iota(i32, cand_ref.shape, 0)
    cand_ref[...] = jnp.where(block_id < n_planes, -1, 0)

    def bit_step(i, carry):
        prefix, n_above = carry
        cand = cand_ref[...]
        ones = cand & plane_ref[i]
        cnt = jnp.sum(jnp.sum(lax.population_count(ones), axis=0).astype(f32), axis=0, keepdims=True)
        accept = n_above + cnt >= topk
        cand_ref[...] = jnp.where(accept, ones, cand ^ ones)
        prefix = jnp.where(accept, prefix | lax.shift_left(jnp.int32(1), KEY_BITS - 1 - i), prefix)
        return prefix, jnp.where(accept, n_above, n_above + cnt)

    prefix, n_gt = lax.fori_loop(0, KEY_BITS, bit_step,
                                 (jnp.zeros((1, Q_TILE), i32), jnp.zeros((1, Q_TILE), f32)))
    thr = prefix ^ INT_MIN
    need = jnp.where(thr <= NEG_INF_KEY, 0.0, topk - n_gt)
    rank_mat = rank_ref[...]

    top_rows = lax.broadcasted_iota(i32, (LANES, Q_TILE), 0) < HEAD_DIM
    for group in range(A_HEADS // ATT_HEADS):
        width = ATT_HEADS * HEAD_DIM
        q_all = stack_heads(aq_ref[0, :, group * width:(group + 1) * width], ATT_HEADS)

        m_ref[...] = jnp.full(m_ref.shape, NEG_INF, f32)

        def logits_chunk(c, seen, q_all=q_all, first_group=group == 0):
            for part in range(KEY_CHUNK // RANK_ROWS):
                rows = slice(part * RANK_ROWS, (part + 1) * RANK_ROWS)
                if first_group:
                    k = key_ref[c, rows, :]
                    eq = k == thr
                    tie = jnp.where(eq, 1.0, 0.0)
                    rank = _dot(rank_mat, tie.astype(bf16)) + seen
                    seen = seen + jnp.sum(_fold_rows(tie, jnp.sum, COUNT_ROWS), axis=0, keepdims=True)
                    order = jnp.where(eq, rank, jnp.where(k > thr, -1.0, 2.0 ** 30))
                    bias = jnp.where(order < need, 0.0, NEG_INF)
                    bias_ref[c, rows, :] = bias
                else:
                    bias = bias_ref[c, rows, :]
                s = _dot_nt(kk_ref[0, key_rows(c, part, RANK_ROWS), :], q_all)
                for h in range(ATT_HEADS):
                    cols = slice(h * Q_TILE, (h + 1) * Q_TILE)
                    sh = (s[:, cols] + bias) * LOG2_E
                    s_ref[c, rows, cols] = sh
                    m_ref[:, cols] = jnp.maximum(m_ref[:, cols], _fold_rows(sh, jnp.max))
            return seen

        chunk_loop(logits_chunk, jnp.zeros((1, Q_TILE), f32))
        m = jnp.max(m_ref[...], axis=0, keepdims=True)

        l_ref[...] = jnp.zeros(l_ref.shape, f32)
        acc_ref[...] = jnp.zeros(acc_ref.shape, f32)

        def softmax_chunk(c, _, m=m):
            for part in range(KEY_CHUNK // RANK_ROWS):
                rows = slice(part * RANK_ROWS, (part + 1) * RANK_ROWS)
                p = jnp.exp2(s_ref[c, rows, :] - m)
                l_ref[...] += _fold_rows(p, jnp.sum)
                acc_ref[...] += _dot(vvt_ref[0, c, :, rows], p.astype(bf16))
            return 0

        chunk_loop(softmax_chunk, 0)
        out_t = acc_ref[...] / jnp.sum(l_ref[...], axis=0, keepdims=True)
        for j in range(ATT_HEADS // 2):
            even = out_t[:, (2 * j) * Q_TILE:(2 * j + 1) * Q_TILE]
            odd = out_t[:, (2 * j + 1) * Q_TILE:(2 * j + 2) * Q_TILE]
            pair = group * (ATT_HEADS // 2) + j
            o_ref[0, :, pair * LANES:(pair + 1) * LANES] = jnp.where(top_rows, even, odd).T.astype(bf16)


def _tie_rank_matrix():
    i = jnp.arange(RANK_ROWS)
    return (i[None, :] < i[:, None]).astype(bf16)


def _dsa(aq, iq, iwt, akk, vvt, ikk):
    bsz, seq, _ = aq.shape
    topk = min(TOPK_MAX, seq // 4)
    n_kc = seq // KEY_CHUNK
    q_per_chunk = KEY_CHUNK // Q_TILE

    def q_spec(width):
        return pl.BlockSpec((1, Q_TILE, width), lambda b, q: (b, q, 0))

    kv_spec = pl.BlockSpec((1, seq, LANES), lambda b, q: (b, 0, 0))
    return pl.pallas_call(
        functools.partial(_dsa_body, topk),
        grid=(bsz, seq // Q_TILE),
        in_specs=[q_spec(A_WIDTH), q_spec(IDX_HEADS * IDX_DIM),
                  pl.BlockSpec((1, SUBLANES, Q_TILE), lambda b, q: (b * n_kc + q // q_per_chunk, 0, q % q_per_chunk)),
                  kv_spec, pl.BlockSpec((1, n_kc, LANES, KEY_CHUNK), lambda b, q: (b, 0, 0, 0)), kv_spec,
                  _resident((RANK_ROWS, RANK_ROWS))],
        out_specs=q_spec(A_WIDTH),
        out_shape=jax.ShapeDtypeStruct((bsz, seq, A_WIDTH), bf16),
        scratch_shapes=[pltpu.VMEM((n_kc, KEY_CHUNK, Q_TILE), i32),
                        pltpu.VMEM((KEY_BITS, seq // PLANE_ROWS, SUBLANES, Q_TILE), i32),
                        pltpu.VMEM((seq // PLANE_ROWS, SUBLANES, Q_TILE), i32),
                        pltpu.VMEM((n_kc, KEY_CHUNK, Q_TILE), f32),
                        pltpu.VMEM((n_kc, KEY_CHUNK, ATT_HEADS * Q_TILE), f32),
                        pltpu.VMEM((SUBLANES, ATT_HEADS * Q_TILE), f32),
                        pltpu.VMEM((SUBLANES, ATT_HEADS * Q_TILE), f32),
                        pltpu.VMEM((LANES, ATT_HEADS * Q_TILE), f32)],
        compiler_params=_params("arbitrary", "arbitrary"),
        name="dsa",
    )(aq, iq, iwt, akk, vvt, ikk, _tie_rank_matrix())


def _cmul(ar, ai, br, bi):
    return ar * br - ai * bi, ar * bi + ai * br


def _s5_prepare(lam_re_ref, lam_im_ref, logdt_ref, bre_ref, bim_ref, cre_ref, cim_ref,
                bmat_re, bmat_im, cmat_re, cmat_im, shift_re, shift_im, pow_re, pow_im):
    lr = jnp.minimum(lam_re_ref[...], -1e-4)
    li = lam_im_ref[...]
    dt = jnp.exp(logdt_ref[...])
    mag = jnp.exp(lr * dt)
    l1r, l1i = mag * jnp.cos(li * dt), mag * jnp.sin(li * dt)
    den = lr * lr + li * li
    cr = ((l1r - 1.0) * lr + l1i * li) / den
    ci = (l1i * lr - (l1r - 1.0) * li) / den

    n_slab = S5_LANES // S5_SLAB
    row_g = lax.broadcasted_iota(i32, (LANES, S5_SLAB), 0) // S5_GROUP
    lane_g = lax.broadcasted_iota(i32, (LANES, S5_SLAB), 1) // S5_STATE
    diag_b = row_g == lane_g
    row_g2 = lax.broadcasted_iota(i32, (S5_SLAB, LANES), 0) // S5_STATE
    lane_g2 = lax.broadcasted_iota(i32, (S5_SLAB, LANES), 1) // S5_GROUP
    diag_c = row_g2 == lane_g2
    for j in range(n_slab):
        sl = slice(j * S5_SLAB, (j + 1) * S5_SLAB)
        bbr, bbi = _cmul(cr[:, sl], ci[:, sl], bre_ref[:, sl], bim_ref[:, sl])
        bmat_re[j] = jnp.where(diag_b, bbr, 0.0).astype(bf16)
        bmat_im[j] = jnp.where(diag_b, bbi, 0.0).astype(bf16)
        cmat_re[j] = jnp.where(diag_c, cre_ref[sl, :], 0.0).astype(bf16)
        cmat_im[j] = jnp.where(diag_c, -cim_ref[sl, :], 0.0).astype(bf16)

    l2r, l2i = _cmul(l1r, l1i, l1r, l1i)
    l3r, l3i = _cmul(l2r, l2i, l1r, l1i)
    l4r, l4i = _cmul(l2r, l2i, l2r, l2i)
    l5r, l5i = _cmul(l4r, l4i, l1r, l1i)
    l6r, l6i = _cmul(l4r, l4i, l2r, l2i)
    l7r, l7i = _cmul(l4r, l4i, l3r, l3i)
    l8r, l8i = _cmul(l4r, l4i, l4r, l4i)
    row = lax.broadcasted_iota(i32, (SUBLANES, S5_LANES), 0)
    for idx, (k, pr, pi) in enumerate(((1, l1r, l1i), (2, l2r, l2i), (4, l4r, l4i))):
        shift_re[idx] = jnp.where(row >= k, pr, 0.0)
        shift_im[idx] = jnp.where(row >= k, pi, 0.0)
    pr_acc = jnp.zeros((SUBLANES, S5_LANES), f32)
    pi_acc = jnp.zeros((SUBLANES, S5_LANES), f32)
    powers = ((l1r, l1i), (l2r, l2i), (l3r, l3i), (l4r, l4i), (l5r, l5i), (l6r, l6i), (l7r, l7i), (l8r, l8i))
    for i, (pr, pi) in enumerate(powers):
        pr_acc = jnp.where(row == i, pr, pr_acc)
        pi_acc = jnp.where(row == i, pi, pi_acc)
    pow_re[...] = pr_acc
    pow_im[...] = pi_acc


def _s5_body(u_ref, lam_re_ref, lam_im_ref, logdt_ref, bre_ref, bim_ref, cre_ref, cim_ref, d_ref,
             glu_a_ref, glu_b_ref, o_ref,
             bmat_re, bmat_im, cmat_re, cmat_im, shift_re, shift_im, pow_re, pow_im,
             carry_re, carry_im, st_re, st_im):
    t_idx = pl.program_id(1)

    @pl.when((pl.program_id(0) == 0) & (t_idx == 0))
    def _():
        _s5_prepare(lam_re_ref, lam_im_ref, logdt_ref, bre_ref, bim_ref, cre_ref, cim_ref,
                    bmat_re, bmat_im, cmat_re, cmat_im, shift_re, shift_im, pow_re, pow_im)

    @pl.when(t_idx == 0)
    def _():
        carry_re[...] = jnp.zeros(carry_re.shape, f32)
        carry_im[...] = jnp.zeros(carry_im.shape, f32)

    u = u_ref[0]
    ub = u.astype(bf16)
    n_slab = S5_LANES // S5_SLAB
    n_tiles = u.shape[0] // SUBLANES
    ys = []
    for j in range(n_slab):
        sl = slice(j * S5_SLAB, (j + 1) * S5_SLAB)
        uj = ub[:, j * LANES:(j + 1) * LANES]
        st_re[:, sl] = _dot(uj, bmat_re[j])
        st_im[:, sl] = _dot(uj, bmat_im[j])

        coef = [(shift_re[i, :, sl], shift_im[i, :, sl]) for i in range(3)]
        pwr, pwi = pow_re[:, sl], pow_im[:, sl]
        c_re, c_im = carry_re[:, sl], carry_im[:, sl]
        for n in range(n_tiles):
            rows = slice(n * SUBLANES, (n + 1) * SUBLANES)
            xr, xi = st_re[rows, sl], st_im[rows, sl]
            for i, k in enumerate((1, 2, 4)):
                dr, di = _cmul(coef[i][0], coef[i][1], pltpu.roll(xr, k, 0), pltpu.roll(xi, k, 0))
                xr, xi = xr + dr, xi + di
            dr, di = _cmul(pwr, pwi, c_re, c_im)
            xr, xi = xr + dr, xi + di
            st_re[rows, sl] = xr
            st_im[rows, sl] = xi
            c_re, c_im = xr[SUBLANES - 1:SUBLANES, :], xi[SUBLANES - 1:SUBLANES, :]
        carry_re[:, sl] = c_re
        carry_im[:, sl] = c_im
        ys.append(_dot(st_re[:, sl].astype(bf16), cmat_re[j]) + _dot(st_im[:, sl].astype(bf16), cmat_im[j]))
    y = jnp.concatenate(ys, axis=1) + d_ref[...] * u
    g = jax.nn.gelu(y).astype(bf16)
    o_ref[0] = (_dot(g, glu_a_ref[...]) * jax.nn.sigmoid(_dot(g, glu_b_ref[...]))).astype(bf16)


def _s5(su, layer, lam_re, lam_im, logdt, b_re, b_im, c_re, c_im, d_skip, glu_a, glu_b):
    bsz, seq, _ = su.shape
    tt = min(S5_TIME_TILE, seq)
    n_slab = S5_LANES // S5_SLAB
    tile = pl.BlockSpec((1, tt, S5_WIDTH), lambda b, t: (b, t, 0))
    return pl.pallas_call(
        _s5_body,
        grid=(bsz, seq // tt),
        in_specs=[tile, _resident((1, S5_LANES)), _resident((1, S5_LANES)), _resident((1, S5_LANES)),
                  _resident((LANES, S5_LANES)), _resident((LANES, S5_LANES)),
                  _resident((S5_LANES, LANES)), _resident((S5_LANES, LANES)), _resident((1, S5_WIDTH)),
                  _resident((S5_WIDTH, S5_WIDTH), layer), _resident((S5_WIDTH, S5_WIDTH), layer)],
        out_specs=tile,
        out_shape=jax.ShapeDtypeStruct((bsz, seq, S5_WIDTH), bf16),
        scratch_shapes=[pltpu.VMEM((n_slab, LANES, S5_SLAB), bf16), pltpu.VMEM((n_slab, LANES, S5_SLAB), bf16),
                        pltpu.VMEM((n_slab, S5_SLAB, LANES), bf16), pltpu.VMEM((n_slab, S5_SLAB, LANES), bf16),
                        pltpu.VMEM((3, SUBLANES, S5_LANES), f32), pltpu.VMEM((3, SUBLANES, S5_LANES), f32),
                        pltpu.VMEM((SUBLANES, S5_LANES), f32), pltpu.VMEM((SUBLANES, S5_LANES), f32),
                        pltpu.VMEM((1, S5_LANES), f32), pltpu.VMEM((1, S5_LANES), f32),
                        pltpu.VMEM((tt, S5_LANES), f32), pltpu.VMEM((tt, S5_LANES), f32)],
        compiler_params=_params("arbitrary", "arbitrary"),
        name="s5",
    )(su, lam_re, lam_im, logdt, b_re, b_im, c_re, c_im, d_skip, glu_a, glu_b)


def _dilated_body(cur_ref, prev_ref, o_ref, lse_ref):
    n = pl.program_id(2)
    rows = cur_ref.shape[1]
    qi = lax.broadcasted_iota(i32, (DIL_SPAN, 2 * DIL_SPAN), 0)
    kj = lax.broadcasted_iota(i32, (DIL_SPAN, 2 * DIL_SPAN), 1)
    band = (kj >= qi) & (kj <= qi + DIL_SPAN)
    band_first = band & ((kj >= DIL_SPAN) | (n > 0))
    low_half = lax.broadcasted_iota(i32, (DIL_SPAN, LANES), 1) < HEAD_DIM
    zero_b = jnp.zeros((), bf16)
    for res in range(cur_ref.shape[2] // C_WIDTH):
        q0, k0, v0, o0 = res * C_WIDTH, res * C_WIDTH + C_OUT, res * C_WIDTH + 2 * C_OUT, res * C_OUT
        for blk in range(rows // DIL_SPAN):
            r0 = blk * DIL_SPAN
            q = cur_ref[0, r0:r0 + DIL_SPAN, q0:q0 + C_OUT]
            if blk == 0:
                k = jnp.concatenate([prev_ref[0, :, k0:k0 + C_OUT], cur_ref[0, 0:DIL_SPAN, k0:k0 + C_OUT]], axis=0)
                v = jnp.concatenate([prev_ref[0, :, v0:v0 + C_OUT], cur_ref[0, 0:DIL_SPAN, v0:v0 + C_OUT]], axis=0)
            else:
                k = cur_ref[0, r0 - DIL_SPAN:r0 + DIL_SPAN, k0:k0 + C_OUT]
                v = cur_ref[0, r0 - DIL_SPAN:r0 + DIL_SPAN, v0:v0 + C_OUT]
            mask = band_first if blk == 0 else band
            for j in range(C_OUT // LANES):
                ls = slice(j * LANES, (j + 1) * LANES)
                out_ls = slice(o0 + j * LANES, o0 + (j + 1) * LANES)
                outs, lses = [], []
                for half in (low_half, ~low_half):
                    s = _dot_nt(jnp.where(half, q[:, ls], zero_b), k[:, ls])
                    s = jnp.where(mask, s, NEG_INF)
                    m = jnp.max(s, axis=1, keepdims=True)
                    p = jnp.exp(s - m)
                    den = jnp.sum(p, axis=1, keepdims=True)
                    outs.append(_dot(p.astype(bf16), v[:, ls]) / den)
                    lses.append(jnp.broadcast_to(m + jnp.log(den), (DIL_SPAN, LANES)))
                o_ref[0, r0:r0 + DIL_SPAN, out_ls] = jnp.where(low_half, outs[0], outs[1]).astype(bf16)
                lse_ref[0, r0:r0 + DIL_SPAN, out_ls] = jnp.where(low_half, lses[0], lses[1])


def _dilated(view, dilation):
    bsz, sub, _ = view.shape
    rows = min(DIL_ROWS, sub)
    n_res = min(DIL_ROWS // rows, dilation)
    blocks_per_step = rows // DIL_SPAN
    cur = pl.BlockSpec((1, rows, n_res * C_WIDTH), lambda b, r, n: (b, n, r))
    prev = pl.BlockSpec((1, DIL_SPAN, n_res * C_WIDTH),
                        lambda b, r, n: (b, jnp.maximum(n * blocks_per_step - 1, 0), r))
    out = pl.BlockSpec((1, rows, n_res * C_OUT), lambda b, r, n: (b, n, r))
    o, lse = pl.pallas_call(
        _dilated_body,
        grid=(bsz, dilation // n_res, sub // rows),
        in_specs=[cur, prev],
        out_specs=[out, out],
        out_shape=[jax.ShapeDtypeStruct((bsz, sub, dilation * C_OUT), bf16),
                   jax.ShapeDtypeStruct((bsz, sub, dilation * C_OUT), f32)],
        compiler_params=_params("parallel", "parallel", "arbitrary"),
        name=f"dilated_{dilation}",
    )(view, view)
    return o.reshape(bsz * sub, dilation * C_OUT), lse.reshape(bsz * sub, dilation * C_OUT)


def _merge_body(x_ref, ya_ref, yb_ref, o0_ref, o1_ref, o2_ref, l0_ref, l1_ref, l2_ref,
                g_ref, wgate_ref, wa_ref, wb_ref, wc_ref, wout_ref, out_ref, *perm_refs):
    x = x_ref[...]
    h = _rmsnorm_rows(x, g_ref[...]).astype(bf16)
    (o0, l0), (o1, l1), (o2, l2) = [
        (o_ref[...].astype(f32), l_ref[...]) if d == 1 else
        (_deinterleave_load(o_ref, d, po_ref), _deinterleave_load(l_ref, d, pl_ref))
        for (o_ref, l_ref, po_ref, pl_ref), (_, d) in zip(
            ((o0_ref, l0_ref, None, None), (o1_ref, l1_ref, perm_refs[0], perm_refs[1]),
             (o2_ref, l2_ref, perm_refs[2], perm_refs[3])), DIL_PAIRS)]
    mx = jnp.maximum(jnp.maximum(l0, l1), l2)
    e0, e1, e2 = jnp.exp(l0 - mx), jnp.exp(l1 - mx), jnp.exp(l2 - mx)
    yc = e0 * o0 + e1 * o1 + e2 * o2
    yc = (yc / (e0 + e1 + e2)).astype(bf16)

    def gate(i):
        return jax.nn.sigmoid(_dot(h, wgate_ref[:, i * D_MODEL:(i + 1) * D_MODEL]))

    merged = gate(0) * _dot(ya_ref[...], wa_ref[...])
    merged = merged + gate(1) * _dot(yb_ref[...], wb_ref[...])
    merged = merged + gate(2) * _dot(yc, wc_ref[...])
    out_ref[...] = x + _dot(merged.astype(bf16), wout_ref[...])


def _merge(x2, layer, ya, yb, os_, lses, gain, w_gates, w_a, w_b, w_c, w_out):
    n = x2.shape[0]
    tm = min(WIDE_TOKEN_TILE, n)

    def tile(width):
        return pl.BlockSpec((tm, width), lambda i: (i, 0))

    return pl.pallas_call(
        _merge_body,
        grid=(n // tm,),
        in_specs=[tile(D_MODEL), tile(A_WIDTH), tile(S5_WIDTH)] + 2 * [
            pl.BlockSpec((tm // d, d * C_OUT), lambda i: (i, 0)) for _, d in DIL_PAIRS] +
                 [_resident((1, D_MODEL), layer), _resident((D_MODEL, 3 * D_MODEL), layer),
                  _resident((A_WIDTH, D_MODEL), layer), _resident((S5_WIDTH, D_MODEL), layer),
                  _resident((C_OUT, D_MODEL), layer), _resident((D_MODEL, D_MODEL), layer)],
        out_specs=tile(D_MODEL),
        out_shape=jax.ShapeDtypeStruct((n, D_MODEL), f32),
        scratch_shapes=[pltpu.VMEM((C_OUT // LANES, tm, LANES), f32)] * 4,
        compiler_params=_params("parallel"),
        name="merge",
    )(x2, ya, yb, *os_, *lses, gain, w_gates, w_a, w_b, w_c, w_out)


def _pack_w_in(w_in):
    offs = [0]
    for s in IN_SIZES:
        offs.append(offs[-1] + s)
    col = lambda i: w_in[:, :, offs[i]:offs[i + 1]]
    aq, ak, av, iq, ik, iw, su, cq, ck, cv, gates = [col(i) for i in range(len(IN_SIZES))]
    parts = [aq, ak, ak, iq, ik, ik, su]
    for g in range(len(DIL_PAIRS)):
        gs = slice(g * C_OUT, (g + 1) * C_OUT)
        parts += [cq[:, :, gs], ck[:, :, gs], cv[:, :, gs]]
    iw_pad = jnp.pad(iw, ((0, 0), (0, 0), (0, SUBLANES - IDX_HEADS)))
    w_t = jnp.transpose(jnp.concatenate([av, av, iw_pad], axis=2), (0, 2, 1))
    return jnp.concatenate(parts, axis=2).astype(bf16), w_t.astype(bf16), gates.astype(bf16)


def _segment_mean_matrix(width):
    seg = jnp.arange(width) // HEAD_DIM
    return jnp.where(seg[:, None] == seg[None, :], 1.0 / HEAD_DIM, 0.0).astype(bf16)


def _tile_gain(g, reps):
    return jnp.tile(g[:, None, :], (1, 1, reps))


def kernel(x, ffn1_norm, ffn1_gate, ffn1_up, ffn1_down, mix_norm, w_in, a_q_norm, a_k_norm, s5_lam_re, s5_lam_im, s5_log_dt, s5_b_re, s5_b_im, s5_c_re, s5_c_im, s5_d, s5_glu_a, s5_glu_b, c_q_norm, c_k_norm, w_branch_a, w_branch_b, w_branch_c, w_out, ffn2_norm, ffn2_gate, ffn2_up, ffn2_down):
    bsz, seq, _ = x.shape
    depth = w_in.shape[0]
    n = bsz * seq
    x2 = x.reshape(n, D_MODEL)
    assert TOKEN_TILE == KEY_CHUNK and seq % KEY_CHUNK == 0
    seg_mean = _segment_mean_matrix(A_WIDTH)
    grp_per_slab = S5_SLAB // S5_STATE
    cast = lambda w: w.astype(bf16)
    row = lambda g: g[:, None, :]
    ffn1 = (row(ffn1_norm), cast(ffn1_gate), cast(ffn1_up), cast(ffn1_down))
    ffn2 = (row(ffn2_norm), cast(ffn2_gate), cast(ffn2_up), cast(ffn2_down))
    w_packed, w_t, w_gates = _pack_w_in(w_in)
    head_gains = (_tile_gain(a_q_norm, A_HEADS), _tile_gain(a_k_norm, 2),
                  _tile_gain(c_q_norm, C_HEADS_PER_GROUP), _tile_gain(c_k_norm, C_HEADS_PER_GROUP))
    glu = (cast(s5_glu_a), cast(s5_glu_b))
    merge_w = (row(mix_norm), w_gates, cast(w_branch_a), cast(w_branch_b), cast(w_branch_c), cast(w_out))
    r3 = lambda a: a.reshape(bsz, seq, a.shape[-1])
    b_t = lambda b: jnp.tile(jnp.transpose(b, (2, 0, 1)).reshape(S5_GROUP, S5_LANES), (grp_per_slab, 1))
    c_t = lambda c: jnp.tile(jnp.transpose(c, (0, 2, 1)).reshape(S5_LANES, S5_GROUP), (1, grp_per_slab))
    for l in range(depth):
        x2 = _ffn(x2, l, *ffn1)
        aq, akk, vvt, iq, ikk, iwt, su, c0, c1, c2 = _in_proj(x2, l, row(mix_norm), w_packed, w_t, seg_mean,
                                                               *head_gains)
        vvt = vvt.reshape(bsz, seq // KEY_CHUNK, LANES, KEY_CHUNK)
        ya = _dsa(r3(aq), r3(iq), iwt, r3(akk), vvt, r3(ikk)).reshape(n, A_WIDTH)
        yb = _s5(r3(su), l, s5_lam_re[l].reshape(1, -1), s5_lam_im[l].reshape(1, -1),
                 jnp.repeat(s5_log_dt[l], S5_STATE).reshape(1, -1),
                 b_t(s5_b_re[l]), b_t(s5_b_im[l]), c_t(s5_c_re[l]), c_t(s5_c_im[l]),
                 s5_d[l].reshape(1, -1), *glu).reshape(n, S5_WIDTH)
        os_, lses = [], []
        for cg, (_, dilation) in zip((c0, c1, c2), DIL_PAIRS):
            o, lse = _dilated(cg.reshape(bsz, seq // dilation, dilation * C_WIDTH), dilation)
            os_.append(o)
            lses.append(lse)
        x2 = _merge(x2, l, ya, yb, os_, lses, *merge_w)
        x2 = _ffn(x2, l, *ffn2)
    return x2.reshape(bsz, seq, D_MODEL)
```

```python
import functools
import math

import jax
import jax.numpy as jnp
from jax import lax
from jax.experimental import pallas as pl
from jax.experimental.pallas import tpu as pltpu

f32 = jnp.float32
bf16 = jnp.bfloat16
i32 = jnp.int32

D_MODEL = 1024
D_FF = 2816
HEAD_DIM = 64
RMS_EPS = 1e-6
A_HEADS = 8
A_WIDTH = A_HEADS * HEAD_DIM
IDX_HEADS = 4
IDX_DIM = 64
TOPK_MAX = 256
S5_WIDTH = 512
S5_GROUP = 16
S5_GROUPS = 32
S5_STATE = 64
S5_LANES = S5_GROUPS * S5_STATE
DIL_PAIRS = ((128, 1), (512, 4), (2048, 16))
C_HEADS_PER_GROUP = 4
C_OUT = C_HEADS_PER_GROUP * HEAD_DIM
C_WIDTH = 3 * C_OUT
IN_SIZES = (A_WIDTH, HEAD_DIM, HEAD_DIM, IDX_HEADS * IDX_DIM, IDX_DIM, IDX_HEADS,
            S5_WIDTH, C_WIDTH, C_WIDTH, C_WIDTH, 3 * D_MODEL)

LANES = 128
SUBLANES = 8
VMEM_LIMIT_BYTES = 56 * 1024 * 1024

TOKEN_TILE = 512
WIDE_TOKEN_TILE = 1024
FF_CHUNK = 256
Q_TILE = 256
ATT_HEADS = 4
KEY_CHUNK = 512
COUNT_ROWS = 64
KEY_BITS = 32
PLANE_ROWS = KEY_BITS * 8
RANK_ROWS = 256
LOOP_UNITS = (4, 2, 1)
S5_TIME_TILE = 1024
S5_SLAB = 512
S5_SCAN_LANES = 512
DIL_SPAN = 128
DIL_ROWS = 2048

NEG_INF = float("-inf")
LOG2_E = math.log2(math.e)
INT_MIN = -2 ** 31
NEG_INF_KEY = INT_MIN + 0x7FFFFF

_NT = (((1,), (1,)), ((), ()))


def _params(*sem):
    return pltpu.CompilerParams(dimension_semantics=sem, vmem_limit_bytes=VMEM_LIMIT_BYTES)


def _resident(shape, layer=None):
    nd = len(shape)
    if layer is None:
        return pl.BlockSpec(shape, lambda *_: (0,) * nd, pipeline_mode=pl.Buffered(1))
    return pl.BlockSpec((None,) + tuple(shape), lambda *_: (layer,) + (0,) * nd, pipeline_mode=pl.Buffered(1))


def _dot(a, b):
    return jnp.dot(a, b, preferred_element_type=f32)


def _dot_nt(a, b):
    return lax.dot_general(a, b, _NT, preferred_element_type=f32)


def _rmsnorm_rows(x, gain):
    return x * lax.rsqrt(jnp.mean(x * x, axis=-1, keepdims=True) + RMS_EPS) * gain


def _head_rmsnorm(z, gain_tiled, seg_mean):
    ms = _dot((z * z).astype(bf16), seg_mean)
    return z * lax.rsqrt(ms + RMS_EPS) * gain_tiled


def _ffn_body(x_ref, g_ref, wg_ref, wu_ref, wd_ref, o_ref, acc_ref):
    x = x_ref[...]
    h = _rmsnorm_rows(x, g_ref[...]).astype(bf16)
    for c in range(D_FF // FF_CHUNK):
        sl = slice(c * FF_CHUNK, (c + 1) * FF_CHUNK)
        gate = _dot(h, wg_ref[:, sl])
        up = _dot(h, wu_ref[:, sl])
        act = (gate * jax.nn.sigmoid(gate) * up).astype(bf16)
        contrib = _dot(act, wd_ref[sl, :])
        if c == 0:
            acc_ref[...] = contrib
        else:
            acc_ref[...] += contrib
    o_ref[...] = x + 0.5 * acc_ref[...]


def _ffn(x2, layer, gain, w_gate, w_up, w_down):
    n = x2.shape[0]
    tm = min(WIDE_TOKEN_TILE, n)
    tile = pl.BlockSpec((tm, D_MODEL), lambda i: (i, 0))
    return pl.pallas_call(
        _ffn_body,
        grid=(n // tm,),
        in_specs=[tile, _resident((1, D_MODEL), layer), _resident((D_MODEL, D_FF), layer),
                  _resident((D_MODEL, D_FF), layer), _resident((D_FF, D_MODEL), layer)],
        out_specs=tile,
        out_shape=jax.ShapeDtypeStruct((n, D_MODEL), f32),
        scratch_shapes=[pltpu.VMEM((tm, D_MODEL), f32)],
        compiler_params=_params("parallel"),
        name="ffn",
    )(x2, gain, w_gate, w_up, w_down)


_P_AQ = 0
_P_AKK = _P_AQ + A_WIDTH
_P_IQ = _P_AKK + LANES
_P_IKK = _P_IQ + IDX_HEADS * IDX_DIM
_P_SU = _P_IKK + LANES
_P_C = _P_SU + S5_WIDTH
_P_TOTAL = _P_C + 3 * C_WIDTH
_T_ROWS = LANES + SUBLANES


def _interleave_store(dst_ref, value, dilation, perm_ref):
    rows, width = value.shape
    for j in range(width // LANES):
        perm_ref[j] = value[:, j * LANES:(j + 1) * LANES]
    for r in range(dilation):
        for j in range(width // LANES):
            piece = perm_ref[j, pl.ds(r, rows // dilation, stride=dilation), :]
            dst_ref[:, r * width + j * LANES:r * width + (j + 1) * LANES] = piece.astype(dst_ref.dtype)


def _deinterleave_load(src_ref, dilation, perm_ref):
    width = src_ref.shape[1] // dilation
    rows = src_ref.shape[0] * dilation
    for r in range(dilation):
        for j in range(width // LANES):
            piece = src_ref[:, r * width + j * LANES:r * width + (j + 1) * LANES]
            perm_ref[j, pl.ds(r, rows // dilation, stride=dilation), :] = piece.astype(f32)
    return jnp.concatenate([perm_ref[j] for j in range(width // LANES)], axis=1)


def _in_proj_body(x_ref, g_ref, w_ref, wt_ref, seg_ref, aqg_ref, akg_ref, cqg_ref, ckg_ref,
                  aq_ref, akk_ref, vvt_ref, iq_ref, ikk_ref, iwt_ref, su_ref, c0_ref, c1_ref, c2_ref, perm_ref):
    h = _rmsnorm_rows(x_ref[...], g_ref[...]).astype(bf16)
    full_c = _dot(h, w_ref[:, _P_C:])
    full_a = _dot(h, w_ref[:, :_P_C])

    def proj(start, width):
        if start >= _P_C:
            return full_c[:, start - _P_C:start - _P_C + width]
        return full_a[:, start:start + width]

    seg = seg_ref[...]
    for c_ref, (_, dilation), g in zip((c0_ref, c1_ref, c2_ref), DIL_PAIRS, range(len(DIL_PAIRS))):
        base = _P_C + g * C_WIDTH
        cq = _head_rmsnorm(proj(base, C_OUT), cqg_ref[...], seg[:C_OUT, :C_OUT]) * (HEAD_DIM ** -0.5)
        ck = _head_rmsnorm(proj(base + C_OUT, C_OUT), ckg_ref[...], seg[:C_OUT, :C_OUT])
        qkv = (cq, ck, proj(base + 2 * C_OUT, C_OUT))
        if dilation == 1:
            for i, part in enumerate(qkv):
                c_ref[:, i * C_OUT:(i + 1) * C_OUT] = part.astype(bf16)
        else:
            _interleave_store(c_ref, jnp.concatenate(qkv, axis=1), dilation, perm_ref)
    aq = _head_rmsnorm(proj(_P_AQ, A_WIDTH), aqg_ref[...], seg) * (HEAD_DIM ** -0.5)
    aq_ref[...] = aq.astype(bf16)
    akk_ref[...] = _head_rmsnorm(proj(_P_AKK, LANES), akg_ref[...], seg[:LANES, :LANES]).astype(bf16)
    iq_ref[...] = proj(_P_IQ, IDX_HEADS * IDX_DIM).astype(bf16)
    ikk_ref[...] = proj(_P_IKK, LANES).astype(bf16)
    su_ref[...] = proj(_P_SU, S5_WIDTH)
    vvt_ref[0] = _dot_nt(wt_ref[:LANES, :], h).astype(bf16)
    iwt_ref[0] = _dot_nt(wt_ref[LANES:, :], h)


def _in_proj(x2, layer, gain, w_packed, w_t, seg_mean, aq_gain, ak_gain, cq_gain, ck_gain):
    n = x2.shape[0]
    tm = min(TOKEN_TILE, n)
    rows = lambda w: pl.BlockSpec((tm, w), lambda i: (i, 0))
    cols = lambda r: pl.BlockSpec((1, r, tm), lambda i: (i, 0, 0))
    outs = [(rows(A_WIDTH), (n, A_WIDTH), bf16), (rows(LANES), (n, LANES), bf16),
            (cols(LANES), (n // tm, LANES, tm), bf16), (rows(IDX_HEADS * IDX_DIM), (n, IDX_HEADS * IDX_DIM), bf16),
            (rows(LANES), (n, LANES), bf16), (cols(SUBLANES), (n // tm, SUBLANES, tm), f32),
            (rows(S5_WIDTH), (n, S5_WIDTH), f32)]
    for _, d in DIL_PAIRS:
        outs.append((pl.BlockSpec((tm // d, d * C_WIDTH), lambda i: (i, 0)), (n // d, d * C_WIDTH), bf16))
    return pl.pallas_call(
        _in_proj_body,
        grid=(n // tm,),
        in_specs=[rows(D_MODEL), _resident((1, D_MODEL), layer),
                  _resident((D_MODEL, _P_TOTAL), layer), _resident((_T_ROWS, D_MODEL), layer),
                  _resident((A_WIDTH, A_WIDTH)),
                  _resident((1, A_WIDTH), layer), _resident((1, LANES), layer),
                  _resident((1, C_OUT), layer), _resident((1, C_OUT), layer)],
        out_specs=[o[0] for o in outs],
        out_shape=[jax.ShapeDtypeStruct(o[1], o[2]) for o in outs],
        scratch_shapes=[pltpu.VMEM((C_WIDTH // LANES, tm, LANES), f32)],
        compiler_params=_params("parallel"),
        name="in_proj",
    )(x2, gain, w_packed, w_t, seg_mean, aq_gain, ak_gain, cq_gain, ck_gain)


def _float_order_key(x):
    bits = pltpu.bitcast(x, i32)
    return jnp.where(bits < 0, bits ^ 0x7FFFFFFF, bits)


def _fold_rows(x, op, keep=SUBLANES):
    return op(x.reshape(x.shape[0] // keep, keep, x.shape[1]), axis=0)


def _bit_planes(words):
    a = [words[t * SUBLANES:(t + 1) * SUBLANES] for t in range(KEY_BITS)]
    for j, m in ((16, 0x0000FFFF), (8, 0x00FF00FF), (4, 0x0F0F0F0F), (2, 0x33333333), (1, 0x55555555)):
        shift = jnp.full(a[0].shape, j, i32)
        for k in range(KEY_BITS):
            if k & j == 0:
                t = (a[k] ^ lax.shift_right_logical(a[k + j], shift)) & m
                a[k] = a[k] ^ t
                a[k + j] = a[k + j] ^ lax.shift_left(t, shift)
    return a


def _dsa_body(topk, aq_ref, iq_ref, iwt_ref, kk_ref, vvt_ref, ikk_ref, rank_ref, o_ref,
              key_ref, plane_ref, cand_ref, bias_ref, s_ref, m_ref, l_ref, acc_ref):
    qb = pl.program_id(1)
    n_chunks = lax.shift_right_logical(qb * Q_TILE, int(math.log2(KEY_CHUNK))) + 1
    q_pos = qb * Q_TILE + lax.broadcasted_iota(i32, (1, Q_TILE), 1)
    key_iota = lax.broadcasted_iota(i32, (KEY_CHUNK, Q_TILE), 0)
    low_half = lax.broadcasted_iota(i32, (Q_TILE, LANES), 1) < HEAD_DIM
    zero_b = jnp.zeros((), bf16)

    def stack_heads(x, n_heads):
        blocks = []
        for h in range(n_heads):
            pair = x[:, (h // 2) * LANES:(h // 2 + 1) * LANES]
            blocks.append(jnp.where(low_half if h % 2 == 0 else ~low_half, pair, zero_b))
        return jnp.concatenate(blocks, axis=0)

    @pl.when((pl.program_id(0) == 0) & (qb == 0))
    def _():
        plane_ref[...] = jnp.zeros(plane_ref.shape, i32)

    iq_all = stack_heads(iq_ref[0], IDX_HEADS)
    w_t = iwt_ref[0] * ((IDX_DIM ** -0.5) * (IDX_HEADS ** -0.5))

    def key_rows(c, part, rows):
        return pl.ds(pl.multiple_of(c * KEY_CHUNK + part * rows, rows), rows)

    def chunk_loop(body, carry, end=n_chunks):
        start = 0
        for unit in LOOP_UNITS:
            n_iter = lax.shift_right_logical(end - start, int(math.log2(unit)))

            def multi(i, cr, unit=unit, start=start):
                for j in range(unit):
                    cr = body(start + i * unit + j, cr)
                return cr

            carry = lax.fori_loop(0, n_iter, multi, carry)
            start = start + n_iter * unit
        return carry

    def score_chunk(c, _, diagonal=False):
        logits = _dot_nt(ikk_ref[0, key_rows(c, 0, KEY_CHUNK), :], iq_all)
        score = None
        for h in range(IDX_HEADS):
            term = jnp.maximum(logits[:, h * Q_TILE:(h + 1) * Q_TILE], 0.0) * w_t[h:h + 1, :]
            score = term if score is None else score + term
        score = jnp.where(score == 0.0, 0.0, score)
        if diagonal:
            score = jnp.where(c * KEY_CHUNK + key_iota <= q_pos, score, NEG_INF)
        key = _float_order_key(score)
        key_ref[c] = key
        for part in range(KEY_CHUNK // PLANE_ROWS):
            planes = _bit_planes(key[part * PLANE_ROWS:(part + 1) * PLANE_ROWS] ^ INT_MIN)
            for i in range(KEY_BITS):
                plane_ref[i, c * (KEY_CHUNK // PLANE_ROWS) + part] = planes[i]
        return 0

    chunk_loop(score_chunk, 0, end=n_chunks - 1)
    score_chunk(n_chunks - 1, 0, diagonal=True)

    n_planes = n_chunks * (KEY_CHUNK // PLANE_ROWS)
    block_id = lax.broadcasted_iota(i32, cand_ref.shape, 0)
    cand_ref[...] = jnp.where(block_id < n_planes, -1, 0)

    def bit_step(i, carry):
        prefix, n_above = carry
        cand = cand_ref[...]
        ones = cand & plane_ref[i]
        cnt = jnp.sum(jnp.sum(lax.population_count(ones), axis=0).astype(f32), axis=0, keepdims=True)
        accept = n_above + cnt >= topk
        cand_ref[...] = jnp.where(accept, ones, cand ^ ones)
        prefix = jnp.where(accept, prefix | lax.shift_left(jnp.int32(1), KEY_BITS - 1 - i), prefix)
        return prefix, jnp.where(accept, n_above, n_above + cnt)

    prefix, n_gt = lax.fori_loop(0, KEY_BITS, bit_step,
                                 (jnp.zeros((1, Q_TILE), i32), jnp.zeros((1, Q_TILE), f32)))
    thr = prefix ^ INT_MIN
    need = jnp.where(thr <= NEG_INF_KEY, 0.0, topk - n_gt)
    rank_mat = rank_ref[...]

    top_rows = lax.broadcasted_iota(i32, (LANES, Q_TILE), 0) < HEAD_DIM
    for group in range(A_HEADS // ATT_HEADS):
        width = ATT_HEADS * HEAD_DIM
        q_all = stack_heads(aq_ref[0, :, group * width:(group + 1) * width], ATT_HEADS)

        m_ref[...] = jnp.full(m_ref.shape, NEG_INF, f32)

        def logits_chunk(c, seen, q_all=q_all, first_group=group == 0):
            for part in range(KEY_CHUNK // RANK_ROWS):
                rows = slice(part * RANK_ROWS, (part + 1) * RANK_ROWS)
                if first_group:
                    k = key_ref[c, rows, :]
                    eq = k == thr
                    tie = jnp.where(eq, 1.0, 0.0)
                    rank = _dot(rank_mat, tie.astype(bf16)) + seen
                    seen = seen + jnp.sum(_fold_rows(tie, jnp.sum, COUNT_ROWS), axis=0, keepdims=True)
                    order = jnp.where(eq, rank, jnp.where(k > thr, -1.0, 2.0 ** 30))
                    bias = jnp.where(order < need, 0.0, NEG_INF)
                    bias_ref[c, rows, :] = bias
                else:
                    bias = bias_ref[c, rows, :]
                s = _dot_nt(kk_ref[0, key_rows(c, part, RANK_ROWS), :], q_all)
                for h in range(ATT_HEADS):
                    cols = slice(h * Q_TILE, (h + 1) * Q_TILE)
                    sh = (s[:, cols] + bias) * LOG2_E
                    s_ref[c, rows, cols] = sh
                    m_ref[:, cols] = jnp.maximum(m_ref[:, cols], _fold_rows(sh, jnp.max))
            return seen

        chunk_loop(logits_chunk, jnp.zeros((1, Q_TILE), f32))
        m = jnp.max(m_ref[...], axis=0, keepdims=True)

        l_ref[...] = jnp.zeros(l_ref.shape, f32)
        acc_ref[...] = jnp.zeros(acc_ref.shape, f32)

        def softmax_chunk(c, _, m=m):
            for part in range(KEY_CHUNK // RANK_ROWS):
                rows = slice(part * RANK_ROWS, (part + 1) * RANK_ROWS)
                p = jnp.exp2(s_ref[c, rows, :] - m)
                l_ref[...] += _fold_rows(p, jnp.sum)
                acc_ref[...] += _dot(vvt_ref[0, c, :, rows], p.astype(bf16))
            return 0

        chunk_loop(softmax_chunk, 0)
        out_t = acc_ref[...] / jnp.sum(l_ref[...], axis=0, keepdims=True)
        for j in range(ATT_HEADS // 2):
            even = out_t[:, (2 * j) * Q_TILE:(2 * j + 1) * Q_TILE]
            odd = out_t[:, (2 * j + 1) * Q_TILE:(2 * j + 2) * Q_TILE]
            pair = group * (ATT_HEADS // 2) + j
            o_ref[0, :, pair * LANES:(pair + 1) * LANES] = jnp.where(top_rows, even, odd).T.astype(bf16)


def _tie_rank_matrix():
    i = jnp.arange(RANK_ROWS)
    return (i[None, :] < i[:, None]).astype(bf16)


def _dsa(aq, iq, iwt, akk, vvt, ikk):
    bsz, seq, _ = aq.shape
    topk = min(TOPK_MAX, seq // 4)
    n_kc = seq // KEY_CHUNK
    q_per_chunk = KEY_CHUNK // Q_TILE

    def q_spec(width):
        return pl.BlockSpec((1, Q_TILE, width), lambda b, q: (b, q, 0))

    kv_spec = pl.BlockSpec((1, seq, LANES), lambda b, q: (b, 0, 0))
    return pl.pallas_call(
        functools.partial(_dsa_body, topk),
        grid=(bsz, seq // Q_TILE),
        in_specs=[q_spec(A_WIDTH), q_spec(IDX_HEADS * IDX_DIM),
                  pl.BlockSpec((1, SUBLANES, Q_TILE), lambda b, q: (b * n_kc + q // q_per_chunk, 0, q % q_per_chunk)),
                  kv_spec, pl.BlockSpec((1, n_kc, LANES, KEY_CHUNK), lambda b, q: (b, 0, 0, 0)), kv_spec,
                  _resident((RANK_ROWS, RANK_ROWS))],
        out_specs=q_spec(A_WIDTH),
        out_shape=jax.ShapeDtypeStruct((bsz, seq, A_WIDTH), bf16),
        scratch_shapes=[pltpu.VMEM((n_kc, KEY_CHUNK, Q_TILE), i32),
                        pltpu.VMEM((KEY_BITS, seq // PLANE_ROWS, SUBLANES, Q_TILE), i32),
                        pltpu.VMEM((seq // PLANE_ROWS, SUBLANES, Q_TILE), i32),
                        pltpu.VMEM((n_kc, KEY_CHUNK, Q_TILE), f32),
                        pltpu.VMEM((n_kc, KEY_CHUNK, ATT_HEADS * Q_TILE), f32),
                        pltpu.VMEM((SUBLANES, ATT_HEADS * Q_TILE), f32),
                        pltpu.VMEM((SUBLANES, ATT_HEADS * Q_TILE), f32),
                        pltpu.VMEM((LANES, ATT_HEADS * Q_TILE), f32)],
        compiler_params=_params("arbitrary", "arbitrary"),
        name="dsa",
    )(aq, iq, iwt, akk, vvt, ikk, _tie_rank_matrix())


def _cmul(ar, ai, br, bi):
    return ar * br - ai * bi, ar * bi + ai * br


def _s5_prepare(lam_re_ref, lam_im_ref, logdt_ref, bre_ref, bim_ref, cre_ref, cim_ref,
                bmat_re, bmat_im, cmat_re, cmat_im, shift_re, shift_im, pow_re, pow_im):
    lr = jnp.minimum(lam_re_ref[...], -1e-4)
    li = lam_im_ref[...]
    dt = jnp.exp(logdt_ref[...])
    mag = jnp.exp(lr * dt)
    l1r, l1i = mag * jnp.cos(li * dt), mag * jnp.sin(li * dt)
    den = lr * lr + li * li
    cr = ((l1r - 1.0) * lr + l1i * li) / den
    ci = (l1i * lr - (l1r - 1.0) * li) / den

    n_slab = S5_LANES // S5_SLAB
    row_g = lax.broadcasted_iota(i32, (LANES, S5_SLAB), 0) // S5_GROUP
    lane_g = lax.broadcasted_iota(i32, (LANES, S5_SLAB), 1) // S5_STATE
    diag_b = row_g == lane_g
    row_g2 = lax.broadcasted_iota(i32, (S5_SLAB, LANES), 0) // S5_STATE
    lane_g2 = lax.broadcasted_iota(i32, (S5_SLAB, LANES), 1) // S5_GROUP
    diag_c = row_g2 == lane_g2
    for j in range(n_slab):
        sl = slice(j * S5_SLAB, (j + 1) * S5_SLAB)
        bbr, bbi = _cmul(cr[:, sl], ci[:, sl], bre_ref[:, sl], bim_ref[:, sl])
        bmat_re[j] = jnp.where(diag_b, bbr, 0.0).astype(bf16)
        bmat_im[j] = jnp.where(diag_b, bbi, 0.0).astype(bf16)
        cmat_re[j] = jnp.where(diag_c, cre_ref[sl, :], 0.0).astype(bf16)
        cmat_im[j] = jnp.where(diag_c, -cim_ref[sl, :], 0.0).astype(bf16)

    l2r, l2i = _cmul(l1r, l1i, l1r, l1i)
    l3r, l3i = _cmul(l2r, l2i, l1r, l1i)
    l4r, l4i = _cmul(l2r, l2i, l2r, l2i)
    l5r, l5i = _cmul(l4r, l4i, l1r, l1i)
    l6r, l6i = _cmul(l4r, l4i, l2r, l2i)
    l7r, l7i = _cmul(l4r, l4i, l3r, l3i)
    l8r, l8i = _cmul(l4r, l4i, l4r, l4i)
    row = lax.broadcasted_iota(i32, (SUBLANES, S5_LANES), 0)
    for idx, (k, pr, pi) in enumerate(((1, l1r, l1i), (2, l2r, l2i), (4, l4r, l4i))):
        shift_re[idx] = jnp.where(row >= k, pr, 0.0)
        shift_im[idx] = jnp.where(row >= k, pi, 0.0)
    pr_acc = jnp.zeros((SUBLANES, S5_LANES), f32)
    pi_acc = jnp.zeros((SUBLANES, S5_LANES), f32)
    powers = ((l1r, l1i), (l2r, l2i), (l3r, l3i), (l4r, l4i), (l5r, l5i), (l6r, l6i), (l7r, l7i), (l8r, l8i))
    for i, (pr, pi) in enumerate(powers):
        pr_acc = jnp.where(row == i, pr, pr_acc)
        pi_acc = jnp.where(row == i, pi, pi_acc)
    pow_re[...] = pr_acc
    pow_im[...] = pi_acc


def _s5_body(u_ref, lam_re_ref, lam_im_ref, logdt_ref, bre_ref, bim_ref, cre_ref, cim_ref, d_ref,
             glu_a_ref, glu_b_ref, o_ref,
             bmat_re, bmat_im, cmat_re, cmat_im, shift_re, shift_im, pow_re, pow_im,
             carry_re, carry_im, st_re, st_im):
    t_idx = pl.program_id(1)

    @pl.when((pl.program_id(0) == 0) & (t_idx == 0))
    def _():
        _s5_prepare(lam_re_ref, lam_im_ref, logdt_ref, bre_ref, bim_ref, cre_ref, cim_ref,
                    bmat_re, bmat_im, cmat_re, cmat_im, shift_re, shift_im, pow_re, pow_im)

    @pl.when(t_idx == 0)
    def _():
        carry_re[...] = jnp.zeros(carry_re.shape, f32)
        carry_im[...] = jnp.zeros(carry_im.shape, f32)

    u = u_ref[0]
    ub = u.astype(bf16)
    n_slab = S5_LANES // S5_SLAB
    n_tiles = u.shape[0] // SUBLANES
    ys = []
    for j in range(n_slab):
        sl = slice(j * S5_SLAB, (j + 1) * S5_SLAB)
        uj = ub[:, j * LANES:(j + 1) * LANES]
        st_re[:, sl] = _dot(uj, bmat_re[j])
        st_im[:, sl] = _dot(uj, bmat_im[j])

        coef = [(shift_re[i, :, sl], shift_im[i, :, sl]) for i in range(3)]
        pwr, pwi = pow_re[:, sl], pow_im[:, sl]
        c_re, c_im = carry_re[:, sl], carry_im[:, sl]
        for n in range(n_tiles):
            rows = slice(n * SUBLANES, (n + 1) * SUBLANES)
            xr, xi = st_re[rows, sl], st_im[rows, sl]
            for i, k in enumerate((1, 2, 4)):
                dr, di = _cmul(coef[i][0], coef[i][1], pltpu.roll(xr, k, 0), pltpu.roll(xi, k, 0))
                xr, xi = xr + dr, xi + di
            dr, di = _cmul(pwr, pwi, c_re, c_im)
            xr, xi = xr + dr, xi + di
            st_re[rows, sl] = xr
            st_im[rows, sl] = xi
            c_re, c_im = xr[SUBLANES - 1:SUBLANES, :], xi[SUBLANES - 1:SUBLANES, :]
        carry_re[:, sl] = c_re
        carry_im[:, sl] = c_im
        ys.append(_dot(st_re[:, sl].astype(bf16), cmat_re[j]) + _dot(st_im[:, sl].astype(bf16), cmat_im[j]))
    y = jnp.concatenate(ys, axis=1) + d_ref[...] * u
    g = jax.nn.gelu(y).astype(bf16)
    o_ref[0] = (_dot(g, glu_a_ref[...]) * jax.nn.sigmoid(_dot(g, glu_b_ref[...]))).astype(bf16)


def _s5(su, layer, lam_re, lam_im, logdt, b_re, b_im, c_re, c_im, d_skip, glu_a, glu_b):
    bsz, seq, _ = su.shape
    tt = min(S5_TIME_TILE, seq)
    n_slab = S5_LANES // S5_SLAB
    tile = pl.BlockSpec((1, tt, S5_WIDTH), lambda b, t: (b, t, 0))
    return pl.pallas_call(
        _s5_body,
        grid=(bsz, seq // tt),
        in_specs=[tile, _resident((1, S5_LANES)), _resident((1, S5_LANES)), _resident((1, S5_LANES)),
                  _resident((LANES, S5_LANES)), _resident((LANES, S5_LANES)),
                  _resident((S5_LANES, LANES)), _resident((S5_LANES, LANES)), _resident((1, S5_WIDTH)),
                  _resident((S5_WIDTH, S5_WIDTH), layer), _resident((S5_WIDTH, S5_WIDTH), layer)],
        out_specs=tile,
        out_shape=jax.ShapeDtypeStruct((bsz, seq, S5_WIDTH), bf16),
        scratch_shapes=[pltpu.VMEM((n_slab, LANES, S5_SLAB), bf16), pltpu.VMEM((n_slab, LANES, S5_SLAB), bf16),
                        pltpu.VMEM((n_slab, S5_SLAB, LANES), bf16), pltpu.VMEM((n_slab, S5_SLAB, LANES), bf16),
                        pltpu.VMEM((3, SUBLANES, S5_LANES), f32), pltpu.VMEM((3, SUBLANES, S5_LANES), f32),
                        pltpu.VMEM((SUBLANES, S5_LANES), f32), pltpu.VMEM((SUBLANES, S5_LANES), f32),
                        pltpu.VMEM((1, S5_LANES), f32), pltpu.VMEM((1, S5_LANES), f32),
                        pltpu.VMEM((tt, S5_LANES), f32), pltpu.VMEM((tt, S5_LANES), f32)],
        compiler_params=_params("arbitrary", "arbitrary"),
        name="s5",
    )(su, lam_re, lam_im, logdt, b_re, b_im, c_re, c_im, d_skip, glu_a, glu_b)


def _dilated_body(cur_ref, prev_ref, o_ref, lse_ref):
    n = pl.program_id(2)
    rows = cur_ref.shape[1]
    qi = lax.broadcasted_iota(i32, (DIL_SPAN, 2 * DIL_SPAN), 0)
    kj = lax.broadcasted_iota(i32, (DIL_SPAN, 2 * DIL_SPAN), 1)
    band = (kj >= qi) & (kj <= qi + DIL_SPAN)
    band_first = band & ((kj >= DIL_SPAN) | (n > 0))
    low_half = lax.broadcasted_iota(i32, (DIL_SPAN, LANES), 1) < HEAD_DIM
    zero_b = jnp.zeros((), bf16)
    for res in range(cur_ref.shape[2] // C_WIDTH):
        q0, k0, v0, o0 = res * C_WIDTH, res * C_WIDTH + C_OUT, res * C_WIDTH + 2 * C_OUT, res * C_OUT
        for blk in range(rows // DIL_SPAN):
            r0 = blk * DIL_SPAN
            q = cur_ref[0, r0:r0 + DIL_SPAN, q0:q0 + C_OUT]
            if blk == 0:
                k = jnp.concatenate([prev_ref[0, :, k0:k0 + C_OUT], cur_ref[0, 0:DIL_SPAN, k0:k0 + C_OUT]], axis=0)
                v = jnp.concatenate([prev_ref[0, :, v0:v0 + C_OUT], cur_ref[0, 0:DIL_SPAN, v0:v0 + C_OUT]], axis=0)
            else:
                k = cur_ref[0, r0 - DIL_SPAN:r0 + DIL_SPAN, k0:k0 + C_OUT]
                v = cur_ref[0, r0 - DIL_SPAN:r0 + DIL_SPAN, v0:v0 + C_OUT]
            mask = band_first if blk == 0 else band
            for j in range(C_OUT // LANES):
                ls = slice(j * LANES, (j + 1) * LANES)
                out_ls = slice(o0 + j * LANES, o0 + (j + 1) * LANES)
                outs, lses = [], []
                for half in (low_half, ~low_half):
                    s = _dot_nt(jnp.where(half, q[:, ls], zero_b), k[:, ls])
                    s = jnp.where(mask, s, NEG_INF)
                    m = jnp.max(s, axis=1, keepdims=True)
                    p = jnp.exp(s - m)
                    den = jnp.sum(p, axis=1, keepdims=True)
                    outs.append(_dot(p.astype(bf16), v[:, ls]) / den)
                    lses.append(jnp.broadcast_to(m + jnp.log(den), (DIL_SPAN, LANES)))
                o_ref[0, r0:r0 + DIL_SPAN, out_ls] = jnp.where(low_half, outs[0], outs[1]).astype(bf16)
                lse_ref[0, r0:r0 + DIL_SPAN, out_ls] = jnp.where(low_half, lses[0], lses[1])


def _dilated(view, dilation):
    bsz, sub, _ = view.shape
    rows = min(DIL_ROWS, sub)
    n_res = min(DIL_ROWS // rows, dilation)
    blocks_per_step = rows // DIL_SPAN
    cur = pl.BlockSpec((1, rows, n_res * C_WIDTH), lambda b, r, n: (b, n, r))
    prev = pl.BlockSpec((1, DIL_SPAN, n_res * C_WIDTH),
                        lambda b, r, n: (b, jnp.maximum(n * blocks_per_step - 1, 0), r))
    out = pl.BlockSpec((1, rows, n_res * C_OUT), lambda b, r, n: (b, n, r))
    o, lse = pl.pallas_call(
        _dilated_body,
        grid=(bsz, dilation // n_res, sub // rows),
        in_specs=[cur, prev],
        out_specs=[out, out],
        out_shape=[jax.ShapeDtypeStruct((bsz, sub, dilation * C_OUT), bf16),
                   jax.ShapeDtypeStruct((bsz, sub, dilation * C_OUT), f32)],
        compiler_params=_params("parallel", "parallel", "arbitrary"),
        name=f"dilated_{dilation}",
    )(view, view)
    return o.reshape(bsz * sub, dilation * C_OUT), lse.reshape(bsz * sub, dilation * C_OUT)


def _merge_body(x_ref, ya_ref, yb_ref, o0_ref, o1_ref, o2_ref, l0_ref, l1_ref, l2_ref,
                g_ref, wgate_ref, wa_ref, wb_ref, wc_ref, wout_ref, out_ref, *perm_refs):
    x = x_ref[...]
    h = _rmsnorm_rows(x, g_ref[...]).astype(bf16)
    (o0, l0), (o1, l1), (o2, l2) = [
        (o_ref[...].astype(f32), l_ref[...]) if d == 1 else
        (_deinterleave_load(o_ref, d, po_ref), _deinterleave_load(l_ref, d, pl_ref))
        for (o_ref, l_ref, po_ref, pl_ref), (_, d) in zip(
            ((o0_ref, l0_ref, None, None), (o1_ref, l1_ref, perm_refs[0], perm_refs[1]),
             (o2_ref, l2_ref, perm_refs[2], perm_refs[3])), DIL_PAIRS)]
    mx = jnp.maximum(jnp.maximum(l0, l1), l2)
    e0, e1, e2 = jnp.exp(l0 - mx), jnp.exp(l1 - mx), jnp.exp(l2 - mx)
    yc = e0 * o0 + e1 * o1 + e2 * o2
    yc = (yc / (e0 + e1 + e2)).astype(bf16)

    def gate(i):
        return jax.nn.sigmoid(_dot(h, wgate_ref[:, i * D_MODEL:(i + 1) * D_MODEL]))

    merged = gate(0) * _dot(ya_ref[...], wa_ref[...])
    merged = merged + gate(1) * _dot(yb_ref[...], wb_ref[...])
    merged = merged + gate(2) * _dot(yc, wc_ref[...])
    out_ref[...] = x + _dot(merged.astype(bf16), wout_ref[...])


def _merge(x2, layer, ya, yb, os_, lses, gain, w_gates, w_a, w_b, w_c, w_out):
    n = x2.shape[0]
    tm = min(WIDE_TOKEN_TILE, n)

    def tile(width):
        return pl.BlockSpec((tm, width), lambda i: (i, 0))

    return pl.pallas_call(
        _merge_body,
        grid=(n // tm,),
        in_specs=[tile(D_MODEL), tile(A_WIDTH), tile(S5_WIDTH)] + 2 * [
            pl.BlockSpec((tm // d, d * C_OUT), lambda i: (i, 0)) for _, d in DIL_PAIRS] +
                 [_resident((1, D_MODEL), layer), _resident((D_MODEL, 3 * D_MODEL), layer),
                  _resident((A_WIDTH, D_MODEL), layer), _resident((S5_WIDTH, D_MODEL), layer),
                  _resident((C_OUT, D_MODEL), layer), _resident((D_MODEL, D_MODEL), layer)],
        out_specs=tile(D_MODEL),
        out_shape=jax.ShapeDtypeStruct((n, D_MODEL), f32),
        scratch_shapes=[pltpu.VMEM((C_OUT // LANES, tm, LANES), f32)] * 4,
        compiler_params=_params("parallel"),
        name="merge",
    )(x2, ya, yb, *os_, *lses, gain, w_gates, w_a, w_b, w_c, w_out)


def _pack_w_in(w_in):
    offs = [0]
    for s in IN_SIZES:
        offs.append(offs[-1] + s)
    col = lambda i: w_in[:, :, offs[i]:offs[i + 1]]
    aq, ak, av, iq, ik, iw, su, cq, ck, cv, gates = [col(i) for i in range(len(IN_SIZES))]
    parts = [aq, ak, ak, iq, ik, ik, su]
    for g in range(len(DIL_PAIRS)):
        gs = slice(g * C_OUT, (g + 1) * C_OUT)
        parts += [cq[:, :, gs], ck[:, :, gs], cv[:, :, gs]]
    iw_pad = jnp.pad(iw, ((0, 0), (0, 0), (0, SUBLANES - IDX_HEADS)))
    w_t = jnp.transpose(jnp.concatenate([av, av, iw_pad], axis=2), (0, 2, 1))
    return jnp.concatenate(parts, axis=2).astype(bf16), w_t.astype(bf16), gates.astype(bf16)


def _segment_mean_matrix(width):
    seg = jnp.arange(width) // HEAD_DIM
    return jnp.where(seg[:, None] == seg[None, :], 1.0 / HEAD_DIM, 0.0).astype(bf16)


def _tile_gain(g, reps):
    return jnp.tile(g[:, None, :], (1, 1, reps))


def kernel(x, ffn1_norm, ffn1_gate, ffn1_up, ffn1_down, mix_norm, w_in, a_q_norm, a_k_norm, s5_lam_re, s5_lam_im, s5_log_dt, s5_b_re, s5_b_im, s5_c_re, s5_c_im, s5_d, s5_glu_a, s5_glu_b, c_q_norm, c_k_norm, w_branch_a, w_branch_b, w_branch_c, w_out, ffn2_norm, ffn2_gate, ffn2_up, ffn2_down):
    bsz, seq, _ = x.shape
    depth = w_in.shape[0]
    n = bsz * seq
    x2 = x.reshape(n, D_MODEL)
    assert TOKEN_TILE == KEY_CHUNK and seq % KEY_CHUNK == 0
    seg_mean = _segment_mean_matrix(A_WIDTH)
    grp_per_slab = S5_SLAB // S5_STATE
    cast = lambda w: w.astype(bf16)
    row = lambda g: g[:, None, :]
    ffn1 = (row(ffn1_norm), cast(ffn1_gate), cast(ffn1_up), cast(ffn1_down))
    ffn2 = (row(ffn2_norm), cast(ffn2_gate), cast(ffn2_up), cast(ffn2_down))
    w_packed, w_t, w_gates = _pack_w_in(w_in)
    head_gains = (_tile_gain(a_q_norm, A_HEADS), _tile_gain(a_k_norm, 2),
                  _tile_gain(c_q_norm, C_HEADS_PER_GROUP), _tile_gain(c_k_norm, C_HEADS_PER_GROUP))
    glu = (cast(s5_glu_a), cast(s5_glu_b))
    merge_w = (row(mix_norm), w_gates, cast(w_branch_a), cast(w_branch_b), cast(w_branch_c), cast(w_out))
    r3 = lambda a: a.reshape(bsz, seq, a.shape[-1])
    b_t = lambda b: jnp.tile(jnp.transpose(b, (2, 0, 1)).reshape(S5_GROUP, S5_LANES), (grp_per_slab, 1))
    c_t = lambda c: jnp.tile(jnp.transpose(c, (0, 2, 1)).reshape(S5_LANES, S5_GROUP), (1, grp_per_slab))
    for l in range(depth):
        x2 = _ffn(x2, l, *ffn1)
        aq, akk, vvt, iq, ikk, iwt, su, c0, c1, c2 = _in_proj(x2, l, row(mix_norm), w_packed, w_t, seg_mean,
                                                               *head_gains)
        vvt = vvt.reshape(bsz, seq // KEY_CHUNK, LANES, KEY_CHUNK)
        ya = _dsa(r3(aq), r3(iq), iwt, r3(akk), vvt, r3(ikk)).reshape(n, A_WIDTH)
        yb = _s5(r3(su), l, s5_lam_re[l].reshape(1, -1), s5_lam_im[l].reshape(1, -1),
                 jnp.repeat(s5_log_dt[l], S5_STATE).reshape(1, -1),
                 b_t(s5_b_re[l]), b_t(s5_b_im[l]), c_t(s5_c_re[l]), c_t(s5_c_im[l]),
                 s5_d[l].reshape(1, -1), *glu).reshape(n, S5_WIDTH)
        os_, lses = [], []
        for cg, (_, dilation) in zip((c0, c1, c2), DIL_PAIRS):
            o, lse = _dilated(cg.reshape(bsz, seq // dilation, dilation * C_WIDTH), dilation)
            os_.append(o)
            lses.append(lse)
        x2 = _merge(x2, l, ya, yb, os_, lses, *merge_w)
        x2 = _ffn(x2, l, *ffn2)
    return x2.reshape(bsz, seq, D_MODEL)
```
